```python
import jax, jax.numpy as jnp
from jax import lax
import numpy as np

D_MODEL = 1024
BATCH = 8
SEQ = 4096
DEPTH = 1

CHUNK = 64
D_MIX = D_MODEL
D_A = D_MIX // 2
D_B = D_MIX - D_A
SGU_BLOCK = 128
SGU_GROUPS = 4
SGU_DIM = D_A // SGU_GROUPS
N_HEADS_B = 4
HEAD_DIM_B = D_B // N_HEADS_B
IDX_HEADS = 8
IDX_DIM = 64
MAX_TOPK = 256
QBLOCK = 128
NUM_BUCKETS = 32
MAX_DISTANCE = 128
D_FF = ((8 * D_MODEL // 3 + 255) // 256) * 256
EPS = 1e-6
SPLITS = (D_A, D_A, D_B, D_B, D_B, IDX_HEADS * IDX_DIM, IDX_DIM, IDX_HEADS)
D_IN = sum(SPLITS)

kernel_name = "hybrid_sgu_dsa_sandwich_block"


def _rmsnorm(x, g):
    xf = x.astype(jnp.float32)
    y = xf * lax.rsqrt(jnp.mean(xf * xf, axis=-1, keepdims=True) + EPS)
    return (y * g.astype(jnp.float32)).astype(x.dtype)


def _layernorm(x, g, b):
    xf = x.astype(jnp.float32)
    mu = jnp.mean(xf, axis=-1, keepdims=True)
    var = jnp.mean(jnp.square(xf - mu), axis=-1, keepdims=True)
    y = (xf - mu) * lax.rsqrt(var + EPS)
    return (y * g.astype(jnp.float32) + b.astype(jnp.float32)).astype(x.dtype)


def _t5_bucket(rel):
    nb = NUM_BUCKETS // 2
    ret = (rel > 0).astype(jnp.int32) * nb
    n = jnp.abs(rel)
    max_exact = nb // 2
    nf = jnp.maximum(n, 1).astype(jnp.float32)
    large = max_exact + (jnp.log(nf / max_exact) / np.float32(np.log(MAX_DISTANCE / max_exact))
                         * (nb - max_exact)).astype(jnp.int32)
    large = jnp.minimum(large, nb - 1)
    return ret + jnp.where(n < max_exact, n, large)


def _spatial_gating(u, v, ln_g, ln_b, w_s, b_s):
    B, S, _ = v.shape
    vn = _layernorm(v, ln_g, ln_b)
    vb = vn.reshape(B, S // SGU_BLOCK, SGU_BLOCK, SGU_GROUPS, SGU_DIM)
    pos_chunk = jnp.arange(SGU_BLOCK) // CHUNK
    mask = pos_chunk[None, :] <= pos_chunk[:, None]
    w = jnp.where(mask[None], w_s, jnp.zeros_like(w_s))
    mixed = jnp.einsum('gij,bcjgd->bcigd', w, vb) + jnp.transpose(b_s)[None, None, :, :, None]
    return u * mixed.reshape(B, S, D_A)


def _dsa_attention(q, k, v, iq, ik, iw, rel_bias):
    B, S, H, Dh = q.shape
    top_k = min(MAX_TOPK, S // 4)
    nb = S // QBLOCK
    s_chunk = jnp.arange(S) // CHUNK
    ikf = ik.astype(jnp.float32)

    def to_blocks(a):
        return jnp.moveaxis(a.reshape((B, nb, QBLOCK) + a.shape[2:]), 1, 0)

    def one_block(args):
        qb, iqb, iwb, blk = args
        t = blk * QBLOCK + jnp.arange(QBLOCK)
        t_chunk = t // CHUNK
        dots = jax.nn.relu(jnp.einsum('bqhd,bsd->bqhs', iqb.astype(jnp.float32), ikf)
                           * np.float32(IDX_DIM ** -0.5))
        score = jnp.einsum('bqh,bqhs->bqs', iwb.astype(jnp.float32) * np.float32(IDX_HEADS ** -0.5), dots)
        visible = s_chunk[None, :] <= t_chunk[:, None]
        score = jnp.where(visible[None], score, -jnp.inf)
        _, idx = lax.top_k(score, top_k)
        valid = (idx // CHUNK) <= t_chunk[None, :, None]
        k_sel = jax.vmap(lambda kb, ib: kb[ib])(k, idx)
        v_sel = jax.vmap(lambda vb, ib: vb[ib])(v, idx)
        bias = jnp.take(rel_bias, _t5_bucket(idx - t[None, :, None]), axis=0)
        logits = (jnp.einsum('bqhd,bqkhd->bqhk', qb, k_sel).astype(jnp.float32) * np.float32(Dh ** -0.5)
                  + jnp.moveaxis(bias, -1, 2).astype(jnp.float32))
        logits = jnp.where(valid[:, :, None, :], logits, -jnp.inf)
        p = jax.nn.softmax(logits, axis=-1).astype(v.dtype)
        return jnp.einsum('bqhk,bqkhd->bqhd', p, v_sel)

    out = lax.map(one_block, (to_blocks(q), to_blocks(iq), to_blocks(iw), jnp.arange(nb)))
    return jnp.moveaxis(out, 0, 1).reshape(B, S, H * Dh)


def setup_inputs(seed: int = 0) -> dict:
    key = jax.random.key(seed)
    ks = jax.random.split(key, 16)
    f32 = jnp.float32

    def gain(k, shape):
        return jnp.ones(shape, f32) + 0.01 * jax.random.normal(k, shape, f32)

    return {
        "x": jax.random.normal(ks[0], (BATCH, SEQ, D_MODEL), f32),
        "g_pre_mix": gain(ks[1], (DEPTH, D_MODEL)),
        "w_in": jax.random.normal(ks[2], (DEPTH, D_MODEL, D_IN), f32) * D_MODEL ** -0.5,
        "sgu_ln_g": gain(ks[3], (DEPTH, D_A)),
        "sgu_ln_b": 0.01 * jax.random.normal(ks[4], (DEPTH, D_A), f32),
        "sgu_w": jax.random.normal(ks[5], (DEPTH, SGU_GROUPS, SGU_BLOCK, SGU_BLOCK), f32) * SGU_BLOCK ** -0.5,
        "sgu_b": gain(ks[6], (DEPTH, SGU_GROUPS, SGU_BLOCK)),
        "rel_bias": 0.5 * jax.random.normal(ks[7], (NUM_BUCKETS, N_HEADS_B), f32),
        "w_o": jax.random.normal(ks[8], (DEPTH, D_MIX, D_MODEL), f32) * D_MIX ** -0.5,
        "g_post_mix": gain(ks[9], (DEPTH, D_MODEL)),
        "g_pre_ffn": gain(ks[10], (DEPTH, D_MODEL)),
        "w_gate": jax.random.normal(ks[11], (DEPTH, D_MODEL, D_FF), f32) * D_MODEL ** -0.5,
        "w_up": jax.random.normal(ks[12], (DEPTH, D_MODEL, D_FF), f32) * D_MODEL ** -0.5,
        "w_down": jax.random.normal(ks[13], (DEPTH, D_FF, D_MODEL), f32) * D_FF ** -0.5,
        "g_post_ffn": gain(ks[14], (DEPTH, D_MODEL)),
    }


def reference(x, g_pre_mix, w_in, sgu_ln_g, sgu_ln_b, sgu_w, sgu_b, rel_bias,
              w_o, g_post_mix, g_pre_ffn, w_gate, w_up, w_down, g_post_ffn):
    B, S, _ = x.shape
    offsets = np.cumsum(SPLITS)[:-1].tolist()
    for l in range(DEPTH):
        h = _rmsnorm(x, g_pre_mix[l])
        proj = h @ w_in[l]
        a_u, a_v, q, k, v, iq, ik, iw = jnp.split(proj, offsets, axis=-1)
        out_a = _spatial_gating(a_u, a_v, sgu_ln_g[l], sgu_ln_b[l], sgu_w[l], sgu_b[l])
        out_b = _dsa_attention(
            q.reshape(B, S, N_HEADS_B, HEAD_DIM_B),
            k.reshape(B, S, N_HEADS_B, HEAD_DIM_B),
            v.reshape(B, S, N_HEADS_B, HEAD_DIM_B),
            iq.reshape(B, S, IDX_HEADS, IDX_DIM), ik, iw, rel_bias)
        mix = jnp.concatenate([out_a, out_b], axis=-1) @ w_o[l]
        x = x + _rmsnorm(mix, g_post_mix[l])
        h = _rmsnorm(x, g_pre_ffn[l])
        f = (jax.nn.silu(h @ w_gate[l]) * (h @ w_up[l])) @ w_down[l]
        x = x + _rmsnorm(f, g_post_ffn[l])
    return x
```

```python
import functools

import numpy as np
import jax
import jax.numpy as jnp
from jax import lax
from jax.experimental import pallas as pl
from jax.experimental.pallas import tpu as pltpu

CHUNK = 64
SGU_BLOCK = 128
SGU_GROUPS = 4
N_HEADS_B = 4
IDX_HEADS = 8
IDX_DIM = 64
MAX_TOPK = 256
NUM_BUCKETS = 32
MAX_DISTANCE = 128
EPS = 1e-6

V7X_LANES = 128
V7X_VMEM_LIMIT_BYTES = 56 * 1024 * 1024

_INT_MIN = -(2 ** 31)
_MASKED = -1e30

_NT = (((1,), (1,)), ((), ()))

f32 = jnp.float32
bf16 = jnp.bfloat16
i32 = jnp.int32


def _rms(x, g):
    return x * lax.rsqrt(jnp.mean(x * x, axis=-1, keepdims=True) + EPS) * g


def _proj_sgu_body(x_ref, g_ref, w_ref, wik_ref, lng_ref, lnb_ref, sw_ref, sbt_ref,
                   outa_ref, q_ref, k_ref, v_ref, iq_ref, ik_ref, iw_ref, *, d_a, d_b):
    tm = x_ref.shape[0]
    h = _rms(x_ref[...], g_ref[...]).astype(bf16)

    def proj(c0, width):
        return jnp.dot(h, w_ref[:, c0:c0 + width], preferred_element_type=f32)

    u = proj(0, d_a)
    a_v = proj(d_a, d_a)
    c0 = 2 * d_a
    q_ref[...] = (proj(c0, d_b) * np.float32((d_b // N_HEADS_B) ** -0.5)).astype(bf16)
    k_ref[...] = proj(c0 + d_b, d_b).astype(bf16)
    v_ref[...] = proj(c0 + 2 * d_b, d_b).astype(bf16)
    iq_ref[...] = (proj(c0 + 3 * d_b, IDX_HEADS * IDX_DIM) * np.float32(IDX_DIM ** -0.5)).astype(bf16)
    ikw = jnp.dot(h, wik_ref[...], preferred_element_type=f32)
    ik_ref[...] = ikw[:, :IDX_DIM].astype(bf16)
    iw_ref[...] = ikw[:, IDX_DIM:IDX_DIM + IDX_HEADS] * np.float32(IDX_HEADS ** -0.5)

    mu = jnp.mean(a_v, axis=-1, keepdims=True)
    cen = a_v - mu
    var = jnp.mean(cen * cen, axis=-1, keepdims=True)
    vn = (cen * lax.rsqrt(var + EPS) * lng_ref[...] + lnb_ref[...]).astype(bf16)

    row_chunk = lax.broadcasted_iota(i32, (SGU_BLOCK, SGU_BLOCK), 0) // CHUNK
    col_chunk = lax.broadcasted_iota(i32, (SGU_BLOCK, SGU_BLOCK), 1) // CHUNK
    causal = col_chunk <= row_chunk
    gdim = d_a // SGU_GROUPS
    for g in range(SGU_GROUPS):
        w_g = jnp.where(causal, sw_ref[g], 0.0).astype(bf16)
        b_g = sbt_ref[:, g:g + 1]
        for c in range(tm // SGU_BLOCK):
            rows = slice(c * SGU_BLOCK, (c + 1) * SGU_BLOCK)
            cols = slice(g * gdim, (g + 1) * gdim)
            mixed = jnp.dot(w_g, vn[rows, cols], preferred_element_type=f32) + b_g
            outa_ref[rows, cols] = (u[rows, cols] * mixed).astype(bf16)


def _proj_sgu(x2, g_pre, w_main, w_ik, ln_g, ln_b, sgu_w, sgu_bt, *, d_a, d_b, tm):
    m, d = x2.shape
    const = lambda i: (0, 0)
    row = lambda i: (i, 0)
    return pl.pallas_call(
        functools.partial(_proj_sgu_body, d_a=d_a, d_b=d_b),
        grid=(m // tm,),
        in_specs=[
            pl.BlockSpec((tm, d), row),
            pl.BlockSpec((1, d), const),
            pl.BlockSpec(w_main.shape, const),
            pl.BlockSpec(w_ik.shape, const),
            pl.BlockSpec((1, d_a), const),
            pl.BlockSpec((1, d_a), const),
            pl.BlockSpec(sgu_w.shape, lambda i: (0, 0, 0)),
            pl.BlockSpec(sgu_bt.shape, const),
        ],
        out_specs=[
            pl.BlockSpec((tm, d_a), row),
            pl.BlockSpec((tm, d_b), row),
            pl.BlockSpec((tm, d_b), row),
            pl.BlockSpec((tm, d_b), row),
            pl.BlockSpec((tm, IDX_HEADS * IDX_DIM), row),
            pl.BlockSpec((tm, IDX_DIM), row),
            pl.BlockSpec((tm, IDX_HEADS), row),
        ],
        out_shape=[
            jax.ShapeDtypeStruct((m, d_a), bf16),
            jax.ShapeDtypeStruct((m, d_b), bf16),
            jax.ShapeDtypeStruct((m, d_b), bf16),
            jax.ShapeDtypeStruct((m, d_b), bf16),
            jax.ShapeDtypeStruct((m, IDX_HEADS * IDX_DIM), bf16),
            jax.ShapeDtypeStruct((m, IDX_DIM), bf16),
            jax.ShapeDtypeStruct((m, IDX_HEADS), f32),
        ],
        compiler_params=pltpu.CompilerParams(
            dimension_semantics=("arbitrary",), vmem_limit_bytes=V7X_VMEM_LIMIT_BYTES),
        name="proj_sgu",
    )(x2, g_pre, w_main, w_ik, ln_g, ln_b, sgu_w, sgu_bt)


def _dsa_body(q_ref, iq_ref, iw_ref, k_ref, v_ref, ik_ref, bias_ref, out_ref,
              keys_ref, madd_ref, m_ref, l_ref, acc_ref, *, tq, top_k, seq_bits):
    tk = tq
    hd = q_ref.shape[2] // N_HEADS_B
    qi = pl.program_id(1)
    n_tiles = qi + 1
    int_min = jnp.int32(_INT_MIN)
    iw = iw_ref[0]
    iq = iq_ref[0]

    def cols(j):
        return pl.ds(pl.multiple_of(j * tk, tk), tk)

    def score_keys(j):
        ik_j = ik_ref[0, cols(j), :]
        sc = jnp.zeros((tq, tk), f32)
        for h in range(IDX_HEADS):
            d = lax.dot_general(iq[:, h * IDX_DIM:(h + 1) * IDX_DIM], ik_j, _NT,
                                preferred_element_type=f32)
            sc = sc + iw[:, h:h + 1] * jnp.maximum(d, 0.0)
        bits = lax.bitcast_convert_type(sc, i32)
        return bits ^ ((bits >> 31) & jnp.int32(0x7FFFFFFF))

    def far_scores(j, carry):
        keys_ref[:, cols(j)] = score_keys(j)
        return carry

    lax.fori_loop(0, qi, far_scores, 0)
    row_chunk = lax.broadcasted_iota(i32, (tq, tk), 0) // CHUNK
    col_chunk = lax.broadcasted_iota(i32, (tq, tk), 1) // CHUNK
    keys_ref[:, cols(qi)] = jnp.where(col_chunk <= row_chunk, score_keys(qi), int_min)

    def count_where(pred):
        def body(j, acc):
            c0 = pl.multiple_of(j * tk, tk)
            kt = keys_ref[:, pl.ds(c0, tk)]
            hit = pred(kt, c0).astype(i32)
            for s in range(tk // V7X_LANES):
                acc = acc + hit[:, s * V7X_LANES:(s + 1) * V7X_LANES]
            return acc
        acc = lax.fori_loop(0, n_tiles, body, jnp.zeros((tq, V7X_LANES), i32))
        return jnp.sum(acc, axis=1, keepdims=True)

    def value_pass(p, prefix):
        cand = prefix + lax.shift_left(jnp.int32(1), 31 - p)
        cnt = count_where(lambda kt, c0: kt >= cand)
        return jnp.where(cnt >= top_k, cand, prefix)

    thr = lax.fori_loop(0, 32, value_pass, jnp.full((tq, 1), _INT_MIN, i32))

    n_ge = count_where(lambda kt, c0: kt >= thr)
    tie = (n_ge > top_k) & (thr > int_min)

    @pl.when(jnp.max(tie.astype(i32)) > 0)
    def _():
        def col_idx(c0):
            return c0 + lax.broadcasted_iota(i32, (tq, tk), 1)

        need = top_k - count_where(lambda kt, c0: kt > thr)

        def index_pass(p, last):
            cand = last + lax.shift_left(jnp.int32(1), seq_bits - 1 - p)
            cnt = count_where(lambda kt, c0: (kt == thr) & (col_idx(c0) < cand))
            return jnp.where(cnt < need, cand, last)

        last = lax.fori_loop(0, seq_bits, index_pass, jnp.zeros((tq, 1), i32))

        def drop(j, carry):
            c0 = pl.multiple_of(j * tk, tk)
            kt = keys_ref[:, pl.ds(c0, tk)]
            surplus = tie & (kt == thr) & (col_idx(c0) > last)
            keys_ref[:, pl.ds(c0, tk)] = jnp.where(surplus, int_min, kt)
            return carry

        lax.fori_loop(0, n_tiles, drop, 0)

    thr_eff = jnp.maximum(thr, int_min + 1)

    def make_mask(j, carry):
        madd_ref[:, cols(j)] = jnp.where(keys_ref[:, cols(j)] >= thr_eff, 0.0, _MASKED).astype(f32)
        return carry

    lax.fori_loop(0, n_tiles, make_mask, 0)

    m_ref[...] = jnp.full(m_ref.shape, _MASKED, f32)
    l_ref[...] = jnp.zeros(l_ref.shape, f32)
    acc_ref[...] = jnp.zeros(acc_ref.shape, f32)

    for h in range(N_HEADS_B):
        hs = slice(h * hd, (h + 1) * hd)
        qh = q_ref[0, :, hs]

        def update(j, bias, h=h, hs=hs, qh=qh):
            s = lax.dot_general(qh, k_ref[0, cols(j), hs], _NT, preferred_element_type=f32)
            s = s + madd_ref[:, cols(j)]
            if bias is not None:
                s = s + bias
            m_old = m_ref[h]
            m_new = jnp.maximum(m_old, jnp.max(s, axis=1, keepdims=True))
            alpha = jnp.exp(m_old - m_new)
            p = jnp.exp(s - m_new)
            l_ref[h] = alpha * l_ref[h] + jnp.sum(p, axis=1, keepdims=True)
            pv = jnp.dot(p.astype(bf16), v_ref[0, cols(j), hs], preferred_element_type=f32)
            acc_ref[:, hs] = alpha * acc_ref[:, hs] + pv
            m_ref[h] = m_new

        def far_update(j, carry, update=update):
            update(j, None)
            return carry

        lax.fori_loop(0, qi - 1, far_update, 0)

        @pl.when(qi >= 1)
        def _(update=update, h=h):
            update(qi - 1, bias_ref[h, 1])

        update(qi, bias_ref[h, 0])
        out_ref[0, :, hs] = (acc_ref[:, hs] / l_ref[h]).astype(out_ref.dtype)


def _dsa(q, k, v, iq, ik, iw, bias_tabs, *, tq, top_k):
    b, s, d_b = q.shape
    tile = lambda bi, qi: (bi, qi, 0)
    whole = lambda bi, qi: (bi, 0, 0)
    return pl.pallas_call(
        functools.partial(_dsa_body, tq=tq, top_k=top_k, seq_bits=int(np.log2(s))),
        grid=(b, s // tq),
        in_specs=[
            pl.BlockSpec((1, tq, d_b), tile),
            pl.BlockSpec((1, tq, IDX_HEADS * IDX_DIM), tile),
            pl.BlockSpec((1, tq, IDX_HEADS), tile),
            pl.BlockSpec((1, s, d_b), whole),
            pl.BlockSpec((1, s, d_b), whole),
            pl.BlockSpec((1, s, IDX_DIM), whole),
            pl.BlockSpec(bias_tabs.shape, lambda bi, qi: (0, 0, 0, 0)),
        ],
        out_specs=pl.BlockSpec((1, tq, d_b), tile),
        out_shape=jax.ShapeDtypeStruct((b, s, d_b), bf16),
        scratch_shapes=[
            pltpu.VMEM((tq, s), i32),
            pltpu.VMEM((tq, s), f32),
            pltpu.VMEM((N_HEADS_B, tq, 1), f32),
            pltpu.VMEM((N_HEADS_B, tq, 1), f32),
            pltpu.VMEM((tq, d_b), f32),
        ],
        compiler_params=pltpu.CompilerParams(
            dimension_semantics=("arbitrary", "arbitrary"), vmem_limit_bytes=V7X_VMEM_LIMIT_BYTES),
        name="dsa",
    )(q, iq, iw, k, v, ik, bias_tabs)


def _out_ffn_body(x_ref, a_ref, b_ref, woa_ref, wob_ref, gpm_ref, gpf_ref, wg_ref, wu_ref,
                  wd_ref, gpo_ref, o_ref, *, ff_chunk):
    mix = (jnp.dot(a_ref[...], woa_ref[...], preferred_element_type=f32)
           + jnp.dot(b_ref[...], wob_ref[...], preferred_element_type=f32))
    x1 = x_ref[...] + _rms(mix, gpm_ref[...])
    h = _rms(x1, gpf_ref[...]).astype(bf16)
    d_ff = wg_ref.shape[1]
    f = jnp.zeros(x1.shape, f32)
    for c in range(d_ff // ff_chunk):
        cs = slice(c * ff_chunk, (c + 1) * ff_chunk)
        gate = jnp.dot(h, wg_ref[:, cs], preferred_element_type=f32)
        up = jnp.dot(h, wu_ref[:, cs], preferred_element_type=f32)
        act = (gate * jax.nn.sigmoid(gate) * up).astype(bf16)
        f = f + jnp.dot(act, wd_ref[cs, :], preferred_element_type=f32)
    o_ref[...] = x1 + _rms(f, gpo_ref[...])


def _out_ffn(x2, out_a, out_b, w_oa, w_ob, g_pm, g_pf, w_gate, w_up, w_down, g_po, *, tm, ff_chunk):
    m, d = x2.shape
    d_a, d_b, d_ff = out_a.shape[1], out_b.shape[1], w_gate.shape[1]
    row = lambda i: (i, 0)
    const = lambda i: (0, 0)
    resident = functools.partial(pl.BlockSpec, index_map=const, pipeline_mode=pl.Buffered(1))
    return pl.pallas_call(
        functools.partial(_out_ffn_body, ff_chunk=ff_chunk),
        grid=(m // tm,),
        in_specs=[
            pl.BlockSpec((tm, d), row),
            pl.BlockSpec((tm, d_a), row),
            pl.BlockSpec((tm, d_b), row),
            resident((d_a, d)),
            resident((d_b, d)),
            resident((1, d)),
            resident((1, d)),
            resident((d, d_ff)),
            resident((d, d_ff)),
            resident((d_ff, d)),
            resident((1, d)),
        ],
        out_specs=pl.BlockSpec((tm, d), row),
        out_shape=jax.ShapeDtypeStruct((m, d), f32),
        compiler_params=pltpu.CompilerParams(
            dimension_semantics=("arbitrary",), vmem_limit_bytes=V7X_VMEM_LIMIT_BYTES),
        name="out_ffn",
    )(x2, out_a, out_b, w_oa, w_ob, g_pm, g_pf, w_gate, w_up, w_down, g_po)


def _t5_bucket(rel):
    nb = NUM_BUCKETS // 2
    ret = (rel > 0).astype(jnp.int32) * nb
    n = jnp.abs(rel)
    max_exact = nb // 2
    nf = jnp.maximum(n, 1).astype(jnp.float32)
    large = max_exact + (jnp.log(nf / max_exact) / np.float32(np.log(MAX_DISTANCE / max_exact))
                         * (nb - max_exact)).astype(jnp.int32)
    large = jnp.minimum(large, nb - 1)
    return ret + jnp.where(n < max_exact, n, large)


def _bias_tables(rel_bias, tq, seq):
    assert tq + 1 >= MAX_DISTANCE
    rel0 = jnp.arange(tq, dtype=jnp.int32)[None, :] - jnp.arange(tq, dtype=jnp.int32)[:, None]
    rel = jnp.stack([rel0, rel0 - tq])
    tabs = jnp.take(rel_bias, _t5_bucket(rel), axis=0)
    far = jnp.take(rel_bias, _t5_bucket(jnp.int32(-(tq + 1))), axis=0)
    return jnp.transpose(tabs - far, (3, 0, 1, 2)).astype(f32)


def kernel(x, g_pre_mix, w_in, sgu_ln_g, sgu_ln_b, sgu_w, sgu_b, rel_bias, w_o, g_post_mix,
           g_pre_ffn, w_gate, w_up, w_down, g_post_ffn):
    b, s, d = x.shape
    depth = w_in.shape[0]
    d_a = d // 2
    d_b = d - d_a
    d_main = 2 * d_a + 3 * d_b + IDX_HEADS * IDX_DIM
    top_k = min(MAX_TOPK, s // 4)
    tq = 256
    tm = 512
    x2 = x.reshape(b * s, d)
    for l in range(depth):
        w_l = w_in[l]
        w_main = w_l[:, :d_main].astype(bf16)
        w_ik = jnp.pad(w_l[:, d_main:], ((0, 0), (0, V7X_LANES - (IDX_DIM + IDX_HEADS)))).astype(bf16)
        out_a, q, k, v, iq, ik, iw = _proj_sgu(
            x2, g_pre_mix[l][None], w_main, w_ik, sgu_ln_g[l][None], sgu_ln_b[l][None],
            sgu_w[l], jnp.transpose(sgu_b[l]), d_a=d_a, d_b=d_b, tm=tm)
        r3 = lambda a: a.reshape(b, s, a.shape[-1])
        out_b = _dsa(r3(q), r3(k), r3(v), r3(iq), r3(ik), r3(iw), _bias_tables(rel_bias, tq, s),
                     tq=tq, top_k=top_k)
        w_ol = w_o[l].astype(bf16)
        x2 = _out_ffn(x2, out_a, out_b.reshape(b * s, d_b), w_ol[:d_a], w_ol[d_a:],
                      g_post_mix[l][None], g_pre_ffn[l][None], w_gate[l].astype(bf16),
                      w_up[l].astype(bf16), w_down[l].astype(bf16), g_post_ffn[l][None],
                      tm=tm, ff_chunk=256)
    return x2.reshape(b, s, d)
```

```python
import functools

import numpy as np
import jax
import jax.numpy as jnp
from jax import lax
from jax.experimental import pallas as pl
from jax.experimental.pallas import tpu as pltpu

CHUNK = 64
SGU_BLOCK = 128
SGU_GROUPS = 4
N_HEADS_B = 4
IDX_HEADS = 8
IDX_DIM = 64
MAX_TOPK = 256
NUM_BUCKETS = 32
MAX_DISTANCE = 128
EPS = 1e-6

V7X_LANES = 128
V7X_SUBLANES = 8
V7X_VMEM_LIMIT_BYTES = 56 * 1024 * 1024

KEY_BITS = 32
_INT_MIN = -(2 ** 31)
_MASKED = -1e30

_NT = (((1,), (1,)), ((), ()))

f32 = jnp.float32
bf16 = jnp.bfloat16
i32 = jnp.int32


def _rms(x, g):
    return x * lax.rsqrt(jnp.mean(x * x, axis=-1, keepdims=True) + EPS) * g


def _proj_sgu_body(x_ref, g_ref, w_ref, wik_ref, lng_ref, lnb_ref, sw_ref, sbt_ref,
                   outa_ref, q_ref, k_ref, vt_ref, iq_ref, ik_ref, iwt_ref, *, d_a, d_b):
    tm = x_ref.shape[0]
    h = _rms(x_ref[...], g_ref[...]).astype(bf16)

    def proj(c0, width):
        return jnp.dot(h, w_ref[:, c0:c0 + width], preferred_element_type=f32)

    u = proj(0, d_a)
    a_v = proj(d_a, d_a)
    c0 = 2 * d_a
    q_ref[...] = (proj(c0, d_b) * np.float32((d_b // N_HEADS_B) ** -0.5)).astype(bf16)
    k_ref[...] = proj(c0 + d_b, d_b).astype(bf16)
    vt_ref[...] = proj(c0 + 2 * d_b, d_b).astype(bf16).T
    iq = (proj(c0 + 3 * d_b, IDX_HEADS * IDX_DIM) * np.float32(IDX_DIM ** -0.5)).astype(bf16)
    for hh in range(IDX_HEADS):
        iq_ref[hh] = iq[:, hh * IDX_DIM:(hh + 1) * IDX_DIM]
    ikw = jnp.dot(h, wik_ref[...], preferred_element_type=f32)
    ik_ref[...] = ikw[:, :IDX_DIM].astype(bf16)
    iwt_ref[...] = ikw.T[IDX_DIM:IDX_DIM + IDX_HEADS, :] * np.float32(IDX_HEADS ** -0.5)

    mu = jnp.mean(a_v, axis=-1, keepdims=True)
    cen = a_v - mu
    var = jnp.mean(cen * cen, axis=-1, keepdims=True)
    vn = (cen * lax.rsqrt(var + EPS) * lng_ref[...] + lnb_ref[...]).astype(bf16)

    row_chunk = lax.broadcasted_iota(i32, (SGU_BLOCK, SGU_BLOCK), 0) // CHUNK
    col_chunk = lax.broadcasted_iota(i32, (SGU_BLOCK, SGU_BLOCK), 1) // CHUNK
    causal = col_chunk <= row_chunk
    gdim = d_a // SGU_GROUPS
    for g in range(SGU_GROUPS):
        w_g = jnp.where(causal, sw_ref[g], 0.0).astype(bf16)
        b_g = sbt_ref[:, g:g + 1]
        for c in range(tm // SGU_BLOCK):
            rows = slice(c * SGU_BLOCK, (c + 1) * SGU_BLOCK)
            cols = slice(g * gdim, (g + 1) * gdim)
            mixed = jnp.dot(w_g, vn[rows, cols], preferred_element_type=f32) + b_g
            outa_ref[rows, cols] = (u[rows, cols] * mixed).astype(bf16)


def _proj_sgu(x2, g_pre, w_main, w_ik, ln_g, ln_b, sgu_w, sgu_bt, *, d_a, d_b, tm):
    m, d = x2.shape
    const = lambda i: (0, 0)
    row = lambda i: (i, 0)
    col = lambda i: (0, i)
    return pl.pallas_call(
        functools.partial(_proj_sgu_body, d_a=d_a, d_b=d_b),
        grid=(m // tm,),
        in_specs=[
            pl.BlockSpec((tm, d), row),
            pl.BlockSpec((1, d), const),
            pl.BlockSpec(w_main.shape, const),
            pl.BlockSpec(w_ik.shape, const),
            pl.BlockSpec((1, d_a), const),
            pl.BlockSpec((1, d_a), const),
            pl.BlockSpec(sgu_w.shape, lambda i: (0, 0, 0)),
            pl.BlockSpec(sgu_bt.shape, const),
        ],
        out_specs=[
            pl.BlockSpec((tm, d_a), row),
            pl.BlockSpec((tm, d_b), row),
            pl.BlockSpec((tm, d_b), row),
            pl.BlockSpec((d_b, tm), col),
            pl.BlockSpec((IDX_HEADS, tm, IDX_DIM), lambda i: (0, i, 0)),
            pl.BlockSpec((tm, IDX_DIM), row),
            pl.BlockSpec((IDX_HEADS, tm), col),
        ],
        out_shape=[
            jax.ShapeDtypeStruct((m, d_a), bf16),
            jax.ShapeDtypeStruct((m, d_b), bf16),
            jax.ShapeDtypeStruct((m, d_b), bf16),
            jax.ShapeDtypeStruct((d_b, m), bf16),
            jax.ShapeDtypeStruct((IDX_HEADS, m, IDX_DIM), bf16),
            jax.ShapeDtypeStruct((m, IDX_DIM), bf16),
            jax.ShapeDtypeStruct((IDX_HEADS, m), f32),
        ],
        compiler_params=pltpu.CompilerParams(
            dimension_semantics=("arbitrary",), vmem_limit_bytes=V7X_VMEM_LIMIT_BYTES),
        name="proj_sgu",
    )(x2, g_pre, w_main, w_ik, ln_g, ln_b, sgu_w, sgu_bt)


def _bit_planes(words):
    a = list(words)
    j, m = 16, 0x0000FFFF
    while j:
        k = 0
        while k < KEY_BITS:
            t = (a[k] ^ lax.shift_right_logical(a[k + j], i32(j))) & i32(m)
            a[k] = a[k] ^ t
            a[k + j] = a[k + j] ^ lax.shift_left(t, i32(j))
            k = (k + j + 1) & ~j
        j >>= 1
        m = (m ^ (m << j)) & 0xFFFFFFFF
    return a


def _dsa_body(relb_ref, q_ref, iq_ref, iwt_ref, k_ref, vt_ref, ik_ref, bkt_ref, out_ref,
              keys_ref, planes_ref, madd_ref, bias_ref, m_ref, l_ref, acc_ref,
              *, tq, top_k, seq_bits, far_bucket):
    tk = tq
    group = KEY_BITS * V7X_SUBLANES
    assert tk == group
    hd = q_ref.shape[1] // N_HEADS_B
    qi = pl.program_id(1)
    n_tiles = qi + 1
    int_min = i32(_INT_MIN)

    def rows(j):
        return pl.ds(pl.multiple_of(j * tk, tk), tk)

    @pl.when((pl.program_id(0) == 0) & (qi == 0))
    def _():
        for d in range(2):
            bucket = bkt_ref[d]
            for h in range(N_HEADS_B):
                far = relb_ref[far_bucket, h]
                tab = jnp.zeros((tk, tq), f32)
                for b in range(NUM_BUCKETS):
                    tab = jnp.where(bucket == b, relb_ref[b, h] - far, tab)
                bias_ref[h, d] = tab

    def score_tile(j, diagonal):
        ik_j = ik_ref[rows(j), :]
        sc = jnp.zeros((tk, tq), f32)
        for h in range(IDX_HEADS):
            d = lax.dot_general(ik_j, iq_ref[h], _NT, preferred_element_type=f32)
            sc = sc + iwt_ref[h:h + 1, :] * jnp.maximum(d, 0.0)
        bits = lax.bitcast_convert_type(sc, i32)
        key = bits ^ ((bits >> 31) & i32(0x7FFFFFFF))
        if diagonal:
            key_chunk = lax.broadcasted_iota(i32, (tk, tq), 0) // CHUNK
            qry_chunk = lax.broadcasted_iota(i32, (tk, tq), 1) // CHUNK
            key = jnp.where(key_chunk <= qry_chunk, key, int_min)
        keys_ref[rows(j), :] = key
        prow = pl.ds(pl.multiple_of(j * V7X_SUBLANES, V7X_SUBLANES), V7X_SUBLANES)
        for lt in range(tq // V7X_LANES):
            lanes = slice(lt * V7X_LANES, (lt + 1) * V7X_LANES)
            planes = _bit_planes([key[V7X_SUBLANES * i:V7X_SUBLANES * (i + 1), lanes]
                                  for i in range(KEY_BITS)])
            planes_ref[0, prow, lanes] = ~planes[0]
            for p in range(1, KEY_BITS):
                planes_ref[p, prow, lanes] = planes[p]

    def far_scores(j, carry):
        score_tile(j, False)
        return carry

    lax.fori_loop(0, qi, far_scores, 0)
    score_tile(qi, True)

    n_prow = planes_ref.shape[1]
    live_rows = lax.broadcasted_iota(i32, (n_prow, tq), 0) < n_tiles * V7X_SUBLANES

    def radix_pass(p, carry):
        alive, n_above, thr_bits = carry
        ones = alive & planes_ref[p]
        n_ones = jnp.sum(lax.population_count(ones), axis=0, keepdims=True)
        take = (n_above + n_ones) >= top_k
        alive = jnp.where(take, ones, alive ^ ones)
        n_above = jnp.where(take, n_above, n_above + n_ones)
        thr_bits = thr_bits | jnp.where(take, lax.shift_left(i32(1), 31 - p), 0)
        return alive, n_above, thr_bits

    zero_row = jnp.zeros((1, tq), i32)
    alive, n_above, thr_bits = lax.fori_loop(
        0, KEY_BITS, radix_pass, (jnp.where(live_rows, i32(-1), 0), zero_row, zero_row))
    thr = thr_bits ^ int_min
    n_equal = jnp.sum(lax.population_count(alive), axis=0, keepdims=True)
    need = top_k - n_above

    tie = (n_equal > need) & (thr > int_min)

    @pl.when(jnp.max(tie.astype(i32)) > 0)
    def _():
        def key_idx(r0):
            return r0 + lax.broadcasted_iota(i32, (tk, tq), 0)

        def count_where(pred):
            def body(j, acc):
                r0 = pl.multiple_of(j * tk, tk)
                hit = pred(keys_ref[pl.ds(r0, tk), :], r0).astype(i32)
                return acc + hit.reshape(tk // V7X_SUBLANES, V7X_SUBLANES, tq).sum(axis=0)
            acc = lax.fori_loop(0, n_tiles, body, jnp.zeros((V7X_SUBLANES, tq), i32))
            return jnp.sum(acc, axis=0, keepdims=True)

        def index_pass(p, last):
            cand = last + lax.shift_left(i32(1), seq_bits - 1 - p)
            cnt = count_where(lambda kt, r0: (kt == thr) & (key_idx(r0) < cand))
            return jnp.where(cnt < need, cand, last)

        last = lax.fori_loop(0, seq_bits, index_pass, zero_row)

        def drop(j, carry):
            r0 = pl.multiple_of(j * tk, tk)
            kt = keys_ref[pl.ds(r0, tk), :]
            surplus = tie & (kt == thr) & (key_idx(r0) > last)
            keys_ref[pl.ds(r0, tk), :] = jnp.where(surplus, int_min, kt)
            return carry

        lax.fori_loop(0, n_tiles, drop, 0)

    thr_eff = jnp.maximum(thr, int_min + 1)

    def make_mask(j, carry):
        madd_ref[rows(j), :] = jnp.where(keys_ref[rows(j), :] >= thr_eff, 0.0, _MASKED).astype(f32)
        return carry

    lax.fori_loop(0, n_tiles, make_mask, 0)

    m_ref[...] = jnp.full(m_ref.shape, _MASKED, f32)
    l_ref[...] = jnp.zeros(l_ref.shape, f32)
    acc_ref[...] = jnp.zeros(acc_ref.shape, f32)

    def attend(j, near):
        madd = madd_ref[rows(j), :]
        for h in range(N_HEADS_B):
            hs = slice(h * hd, (h + 1) * hd)
            s = lax.dot_general(k_ref[rows(j), hs], q_ref[:, hs], _NT, preferred_element_type=f32)
            s = s + madd
            if near is not None:
                s = s + bias_ref[h, near]
            m_old = m_ref[h]
            m_new = jnp.maximum(m_old, jnp.max(s, axis=0, keepdims=True))
            alpha = jnp.exp(m_old - m_new)
            p = jnp.exp(s - m_new)
            l_ref[h] = alpha * l_ref[h] + jnp.sum(p, axis=0, keepdims=True)
            pv = jnp.dot(vt_ref[hs, rows(j)], p.astype(bf16), preferred_element_type=f32)
            acc_ref[h] = alpha * acc_ref[h] + pv
            m_ref[h] = m_new

    def far_attend(j, carry):
        attend(j, None)
        return carry

    lax.fori_loop(0, qi - 1, far_attend, 0)

    @pl.when(qi >= 1)
    def _():
        attend(qi - 1, 1)

    attend(qi, 0)
    for h in range(N_HEADS_B):
        out_ref[:, h * hd:(h + 1) * hd] = (acc_ref[h] / l_ref[h]).T.astype(out_ref.dtype)


def _dsa(rel_bias, q, iq, iwt, k, vt, ik, buckets, *, batch, tq, top_k):
    m, d_b = q.shape
    s = m // batch
    nq = s // tq
    hd = d_b // N_HEADS_B
    tile_rows = lambda bi, qi: (bi * nq + qi, 0)
    batch_rows = lambda bi, qi: (bi, 0)
    return pl.pallas_call(
        functools.partial(_dsa_body, tq=tq, top_k=top_k, seq_bits=int(np.log2(s)),
                          far_bucket=NUM_BUCKETS // 2 - 1),
        grid=(batch, nq),
        in_specs=[
            pl.BlockSpec(memory_space=pltpu.SMEM),
            pl.BlockSpec((tq, d_b), tile_rows),
            pl.BlockSpec((IDX_HEADS, tq, IDX_DIM), lambda bi, qi: (0, bi * nq + qi, 0)),
            pl.BlockSpec((IDX_HEADS, tq), lambda bi, qi: (0, bi * nq + qi)),
            pl.BlockSpec((s, d_b), batch_rows),
            pl.BlockSpec((d_b, s), lambda bi, qi: (0, bi)),
            pl.BlockSpec((s, IDX_DIM), batch_rows),
            pl.BlockSpec(buckets.shape, lambda bi, qi: (0, 0, 0)),
        ],
        out_specs=pl.BlockSpec((tq, d_b), tile_rows),
        out_shape=jax.ShapeDtypeStruct((m, d_b), bf16),
        scratch_shapes=[
            pltpu.VMEM((s, tq), i32),
            pltpu.VMEM((KEY_BITS, s // KEY_BITS, tq), i32),
            pltpu.VMEM((s, tq), f32),
            pltpu.VMEM((N_HEADS_B, 2, tq, tq), f32),
            pltpu.VMEM((N_HEADS_B, 1, tq), f32),
            pltpu.VMEM((N_HEADS_B, 1, tq), f32),
            pltpu.VMEM((N_HEADS_B, hd, tq), f32),
        ],
        compiler_params=pltpu.CompilerParams(
            dimension_semantics=("arbitrary", "arbitrary"), vmem_limit_bytes=V7X_VMEM_LIMIT_BYTES),
        name="dsa",
    )(rel_bias, q, iq, iwt, k, vt, ik, buckets)


def _out_ffn_body(x_ref, a_ref, b_ref, woa_ref, wob_ref, gpm_ref, gpf_ref, wg_ref, wu_ref,
                  wd_ref, gpo_ref, o_ref, *, ff_chunk):
    mix = (jnp.dot(a_ref[...], woa_ref[...], preferred_element_type=f32)
           + jnp.dot(b_ref[...], wob_ref[...], preferred_element_type=f32))
    x1 = x_ref[...] + _rms(mix, gpm_ref[...])
    h = _rms(x1, gpf_ref[...]).astype(bf16)
    d_ff = wg_ref.shape[1]
    f = jnp.zeros(x1.shape, f32)
    for c in range(d_ff // ff_chunk):
        cs = slice(c * ff_chunk, (c + 1) * ff_chunk)
        gate = jnp.dot(h, wg_ref[:, cs], preferred_element_type=f32)
        up = jnp.dot(h, wu_ref[:, cs], preferred_element_type=f32)
        act = (gate * jax.nn.sigmoid(gate) * up).astype(bf16)
        f = f + jnp.dot(act, wd_ref[cs, :], preferred_element_type=f32)
    o_ref[...] = x1 + _rms(f, gpo_ref[...])


def _out_ffn(x2, out_a, out_b, w_oa, w_ob, g_pm, g_pf, w_gate, w_up, w_down, g_po, *, tm, ff_chunk):
    m, d = x2.shape
    d_a, d_b, d_ff = out_a.shape[1], out_b.shape[1], w_gate.shape[1]
    row = lambda i: (i, 0)
    const = lambda i: (0, 0)
    resident = functools.partial(pl.BlockSpec, index_map=const, pipeline_mode=pl.Buffered(1))
    return pl.pallas_call(
        functools.partial(_out_ffn_body, ff_chunk=ff_chunk),
        grid=(m // tm,),
        in_specs=[
            pl.BlockSpec((tm, d), row),
            pl.BlockSpec((tm, d_a), row),
            pl.BlockSpec((tm, d_b), row),
            resident((d_a, d)),
            resident((d_b, d)),
            resident((1, d)),
            resident((1, d)),
            resident((d, d_ff)),
            resident((d, d_ff)),
            resident((d_ff, d)),
            resident((1, d)),
        ],
        out_specs=pl.BlockSpec((tm, d), row),
        out_shape=jax.ShapeDtypeStruct((m, d), f32),
        compiler_params=pltpu.CompilerParams(
            dimension_semantics=("arbitrary",), vmem_limit_bytes=V7X_VMEM_LIMIT_BYTES),
        name="out_ffn",
    )(x2, out_a, out_b, w_oa, w_ob, g_pm, g_pf, w_gate, w_up, w_down, g_po)


def _t5_bucket(rel):
    nb = NUM_BUCKETS // 2
    ret = (rel > 0).astype(jnp.int32) * nb
    n = jnp.abs(rel)
    max_exact = nb // 2
    nf = jnp.maximum(n, 1).astype(jnp.float32)
    large = max_exact + (jnp.log(nf / max_exact) / np.float32(np.log(MAX_DISTANCE / max_exact))
                         * (nb - max_exact)).astype(jnp.int32)
    large = jnp.minimum(large, nb - 1)
    return ret + jnp.where(n < max_exact, n, large)


def _near_buckets(tq):
    assert tq + 1 >= MAX_DISTANCE
    rel0 = jnp.arange(tq, dtype=jnp.int32)[:, None] - jnp.arange(tq, dtype=jnp.int32)[None, :]
    return _t5_bucket(jnp.stack([rel0, rel0 - tq]))


def kernel(x, g_pre_mix, w_in, sgu_ln_g, sgu_ln_b, sgu_w, sgu_b, rel_bias, w_o, g_post_mix,
           g_pre_ffn, w_gate, w_up, w_down, g_post_ffn):
    b, s, d = x.shape
    depth = w_in.shape[0]
    d_a = d // 2
    d_b = d - d_a
    d_main = 2 * d_a + 3 * d_b + IDX_HEADS * IDX_DIM
    top_k = min(MAX_TOPK, s // 4)
    tq = KEY_BITS * V7X_SUBLANES
    tm = 512
    x2 = x.reshape(b * s, d)
    for l in range(depth):
        w_l = w_in[l]
        w_main = w_l[:, :d_main].astype(bf16)
        w_ik = jnp.pad(w_l[:, d_main:], ((0, 0), (0, V7X_LANES - (IDX_DIM + IDX_HEADS)))).astype(bf16)
        out_a, q, k, vt, iq, ik, iwt = _proj_sgu(
            x2, g_pre_mix[l][None], w_main, w_ik, sgu_ln_g[l][None], sgu_ln_b[l][None],
            sgu_w[l], jnp.transpose(sgu_b[l]), d_a=d_a, d_b=d_b, tm=tm)
        out_b = _dsa(rel_bias, q, iq, iwt, k, vt, ik, _near_buckets(tq), batch=b, tq=tq, top_k=top_k)
        w_ol = w_o[l].astype(bf16)
        x2 = _out_ffn(x2, out_a, out_b, w_ol[:d_a], w_ol[d_a:],
                      g_post_mix[l][None], g_pre_ffn[l][None], w_gate[l].astype(bf16),
                      w_up[l].astype(bf16), w_down[l].astype(bf16), g_post_ffn[l][None],
                      tm=tm, ff_chunk=256)
    return x2.reshape(b, s, d)
```

```python
import functools

import numpy as np
import jax
import jax.numpy as jnp
from jax import lax
from jax.experimental import pallas as pl
from jax.experimental.pallas import tpu as pltpu

CHUNK = 64
SGU_BLOCK = 128
SGU_GROUPS = 4
N_HEADS_B = 4
IDX_HEADS = 8
IDX_DIM = 64
MAX_TOPK = 256
NUM_BUCKETS = 32
MAX_DISTANCE = 128
EPS = 1e-6

V7X_LANES = 128
V7X_SUBLANES = 8
V7X_VMEM_LIMIT_BYTES = 56 * 1024 * 1024

KEY_BITS = 32
_INT_MIN = -(2 ** 31)
_MASKED = -1e30
SUM_FLOOR = 1e-30

_NT = (((1,), (1,)), ((), ()))

f32 = jnp.float32
bf16 = jnp.bfloat16
i32 = jnp.int32


def _rms(x, g):
    return x * lax.rsqrt(jnp.mean(x * x, axis=-1, keepdims=True) + EPS) * g


def _proj_sgu_body(x_ref, g_ref, w_ref, wik_ref, lng_ref, lnb_ref, sw_ref, sbt_ref,
                   outa_ref, q_ref, k_ref, vt_ref, iq_ref, ik_ref, iwt_ref, *, d_a, d_b):
    tm = x_ref.shape[0]
    h = _rms(x_ref[...], g_ref[...]).astype(bf16)

    def proj(c0, width):
        return jnp.dot(h, w_ref[:, c0:c0 + width], preferred_element_type=f32)

    u = proj(0, d_a)
    a_v = proj(d_a, d_a)
    c0 = 2 * d_a
    q_ref[...] = (proj(c0, d_b) * np.float32((d_b // N_HEADS_B) ** -0.5)).astype(bf16)
    k_ref[...] = proj(c0 + d_b, d_b).astype(bf16)
    vt_ref[...] = proj(c0 + 2 * d_b, d_b).astype(bf16).T
    iq = (proj(c0 + 3 * d_b, IDX_HEADS * IDX_DIM) * np.float32(IDX_DIM ** -0.5)).astype(bf16)
    for hh in range(IDX_HEADS):
        iq_ref[hh] = iq[:, hh * IDX_DIM:(hh + 1) * IDX_DIM]
    ikw = jnp.dot(h, wik_ref[...], preferred_element_type=f32)
    ik_ref[...] = ikw[:, :IDX_DIM].astype(bf16)
    iwt_ref[...] = ikw.T[IDX_DIM:IDX_DIM + IDX_HEADS, :] * np.float32(IDX_HEADS ** -0.5)

    mu = jnp.mean(a_v, axis=-1, keepdims=True)
    cen = a_v - mu
    var = jnp.mean(cen * cen, axis=-1, keepdims=True)
    vn = (cen * lax.rsqrt(var + EPS) * lng_ref[...] + lnb_ref[...]).astype(bf16)

    row_chunk = lax.broadcasted_iota(i32, (SGU_BLOCK, SGU_BLOCK), 0) // CHUNK
    col_chunk = lax.broadcasted_iota(i32, (SGU_BLOCK, SGU_BLOCK), 1) // CHUNK
    causal = col_chunk <= row_chunk
    gdim = d_a // SGU_GROUPS
    for g in range(SGU_GROUPS):
        w_g = jnp.where(causal, sw_ref[g], 0.0).astype(bf16)
        b_g = sbt_ref[:, g:g + 1]
        for c in range(tm // SGU_BLOCK):
            rows = slice(c * SGU_BLOCK, (c + 1) * SGU_BLOCK)
            cols = slice(g * gdim, (g + 1) * gdim)
            mixed = jnp.dot(w_g, vn[rows, cols], preferred_element_type=f32) + b_g
            outa_ref[rows, cols] = (u[rows, cols] * mixed).astype(bf16)


def _proj_sgu(x2, g_pre, w_main, w_ik, ln_g, ln_b, sgu_w, sgu_bt, *, d_a, d_b, tm):
    m, d = x2.shape
    const = lambda i: (0, 0)
    row = lambda i: (i, 0)
    col = lambda i: (0, i)
    return pl.pallas_call(
        functools.partial(_proj_sgu_body, d_a=d_a, d_b=d_b),
        grid=(m // tm,),
        in_specs=[
            pl.BlockSpec((tm, d), row),
            pl.BlockSpec((1, d), const),
            pl.BlockSpec(w_main.shape, const),
            pl.BlockSpec(w_ik.shape, const),
            pl.BlockSpec((1, d_a), const),
            pl.BlockSpec((1, d_a), const),
            pl.BlockSpec(sgu_w.shape, lambda i: (0, 0, 0)),
            pl.BlockSpec(sgu_bt.shape, const),
        ],
        out_specs=[
            pl.BlockSpec((tm, d_a), row),
            pl.BlockSpec((tm, d_b), row),
            pl.BlockSpec((tm, d_b), row),
            pl.BlockSpec((d_b, tm), col),
            pl.BlockSpec((IDX_HEADS, tm, IDX_DIM), lambda i: (0, i, 0)),
            pl.BlockSpec((tm, IDX_DIM), row),
            pl.BlockSpec((IDX_HEADS, tm), col),
        ],
        out_shape=[
            jax.ShapeDtypeStruct((m, d_a), bf16),
            jax.ShapeDtypeStruct((m, d_b), bf16),
            jax.ShapeDtypeStruct((m, d_b), bf16),
            jax.ShapeDtypeStruct((d_b, m), bf16),
            jax.ShapeDtypeStruct((IDX_HEADS, m, IDX_DIM), bf16),
            jax.ShapeDtypeStruct((m, IDX_DIM), bf16),
            jax.ShapeDtypeStruct((IDX_HEADS, m), f32),
        ],
        compiler_params=pltpu.CompilerParams(
            dimension_semantics=("arbitrary",), vmem_limit_bytes=V7X_VMEM_LIMIT_BYTES),
        name="proj_sgu",
    )(x2, g_pre, w_main, w_ik, ln_g, ln_b, sgu_w, sgu_bt)


def _bit_planes(words):
    a = list(words)
    j, m = 16, 0x0000FFFF
    while j:
        k = 0
        while k < KEY_BITS:
            t = (a[k] ^ lax.shift_right_logical(a[k + j], i32(j))) & i32(m)
            a[k] = a[k] ^ t
            a[k + j] = a[k + j] ^ lax.shift_left(t, i32(j))
            k = (k + j + 1) & ~j
        j >>= 1
        m = (m ^ (m << j)) & 0xFFFFFFFF
    return a


def _dsa_body(relb_ref, q_ref, iq_ref, iwt_ref, k_ref, vt_ref, ik_ref, bkt_ref, out_ref,
              keys_ref, planes_ref, madd_ref, bias_ref, knorm_ref, m_ref, l_ref, acc_ref,
              *, tq, top_k, seq_bits, far_bucket):
    tk = tq
    group = KEY_BITS * V7X_SUBLANES
    assert tk == group
    hd = q_ref.shape[1] // N_HEADS_B
    qi = pl.program_id(1)
    n_tiles = qi + 1
    int_min = i32(_INT_MIN)

    def rows(j):
        return pl.ds(pl.multiple_of(j * tk, tk), tk)

    @pl.when((pl.program_id(0) == 0) & (qi == 0))
    def _():
        for d in range(2):
            bucket = bkt_ref[d]
            for h in range(N_HEADS_B):
                far = relb_ref[far_bucket, h]
                tab = jnp.zeros((tk, tq), f32)
                for b in range(NUM_BUCKETS):
                    tab = jnp.where(bucket == b, relb_ref[b, h] - far, tab)
                bias_ref[h, d] = tab

    def score_tile(j, diagonal):
        ik_j = ik_ref[rows(j), :]
        sc = jnp.zeros((tk, tq), f32)
        for h in range(IDX_HEADS):
            d = lax.dot_general(ik_j, iq_ref[h], _NT, preferred_element_type=f32)
            sc = sc + iwt_ref[h:h + 1, :] * jnp.maximum(d, 0.0)
        bits = lax.bitcast_convert_type(sc, i32)
        key = bits ^ ((bits >> 31) & i32(0x7FFFFFFF))
        if diagonal:
            key_chunk = lax.broadcasted_iota(i32, (tk, tq), 0) // CHUNK
            qry_chunk = lax.broadcasted_iota(i32, (tk, tq), 1) // CHUNK
            key = jnp.where(key_chunk <= qry_chunk, key, int_min)
        keys_ref[rows(j), :] = key
        prow = pl.ds(pl.multiple_of(j * V7X_SUBLANES, V7X_SUBLANES), V7X_SUBLANES)
        for lt in range(tq // V7X_LANES):
            lanes = slice(lt * V7X_LANES, (lt + 1) * V7X_LANES)
            planes = _bit_planes([key[V7X_SUBLANES * i:V7X_SUBLANES * (i + 1), lanes]
                                  for i in range(KEY_BITS)])
            planes_ref[0, prow, lanes] = ~planes[0]
            for p in range(1, KEY_BITS):
                planes_ref[p, prow, lanes] = planes[p]

    def far_scores(j, carry):
        score_tile(j, False)
        return carry

    lax.fori_loop(0, qi, far_scores, 0)
    score_tile(qi, True)

    n_prow = planes_ref.shape[1]
    live_rows = lax.broadcasted_iota(i32, (n_prow, tq), 0) < n_tiles * V7X_SUBLANES

    def radix_pass(p, carry):
        alive, n_above, thr_bits = carry
        ones = alive & planes_ref[p]
        n_ones = jnp.sum(lax.population_count(ones), axis=0, keepdims=True)
        take = (n_above + n_ones) >= top_k
        alive = jnp.where(take, ones, alive ^ ones)
        n_above = jnp.where(take, n_above, n_above + n_ones)
        thr_bits = thr_bits | jnp.where(take, lax.shift_left(i32(1), 31 - p), 0)
        return alive, n_above, thr_bits

    zero_row = jnp.zeros((1, tq), i32)
    alive, n_above, thr_bits = lax.fori_loop(
        0, KEY_BITS, radix_pass, (jnp.where(live_rows, i32(-1), 0), zero_row, zero_row))
    thr = thr_bits ^ int_min
    n_equal = jnp.sum(lax.population_count(alive), axis=0, keepdims=True)
    need = top_k - n_above

    tie = (n_equal > need) & (thr > int_min)

    @pl.when(jnp.max(tie.astype(i32)) > 0)
    def _():
        def key_idx(r0):
            return r0 + lax.broadcasted_iota(i32, (tk, tq), 0)

        def count_where(pred):
            def body(j, acc):
                r0 = pl.multiple_of(j * tk, tk)
                hit = pred(keys_ref[pl.ds(r0, tk), :], r0).astype(i32)
                return acc + hit.reshape(tk // V7X_SUBLANES, V7X_SUBLANES, tq).sum(axis=0)
            acc = lax.fori_loop(0, n_tiles, body, jnp.zeros((V7X_SUBLANES, tq), i32))
            return jnp.sum(acc, axis=0, keepdims=True)

        def index_pass(p, last):
            cand = last + lax.shift_left(i32(1), seq_bits - 1 - p)
            cnt = count_where(lambda kt, r0: (kt == thr) & (key_idx(r0) < cand))
            return jnp.where(cnt < need, cand, last)

        last = lax.fori_loop(0, seq_bits, index_pass, zero_row)

        def drop(j, carry):
            r0 = pl.multiple_of(j * tk, tk)
            kt = keys_ref[pl.ds(r0, tk), :]
            surplus = tie & (kt == thr) & (key_idx(r0) > last)
            keys_ref[pl.ds(r0, tk), :] = jnp.where(surplus, int_min, kt)
            return carry

        lax.fori_loop(0, n_tiles, drop, 0)

    thr_eff = jnp.maximum(thr, int_min + 1)

    def make_mask(j, carry):
        madd_ref[rows(j), :] = jnp.where(keys_ref[rows(j), :] >= thr_eff, 0.0, _MASKED).astype(f32)
        return carry

    lax.fori_loop(0, n_tiles, make_mask, 0)

    heads = [slice(h * hd, (h + 1) * hd) for h in range(N_HEADS_B)]

    @pl.when(qi == 0)
    def _():
        for h, hs in enumerate(heads):
            def longest(c, best, hs=hs):
                kc = k_ref[rows(c), hs].astype(f32)
                return jnp.maximum(best, jnp.sum(kc * kc, axis=1, keepdims=True))
            best = lax.fori_loop(0, k_ref.shape[0] // tk, longest, jnp.zeros((tk, 1), f32))
            knorm_ref[h] = jnp.full((1, tq), jnp.sqrt(jnp.max(best)), f32)

    bounds = []
    for h, hs in enumerate(heads):
        qt = q_ref[:, hs].astype(f32).T
        far = relb_ref[far_bucket, h]
        bias_max = jnp.float32(0.0)
        for b in range(NUM_BUCKETS):
            bias_max = jnp.maximum(bias_max, relb_ref[b, h] - far)
        bounds.append(jnp.sqrt(jnp.sum(qt * qt, axis=0, keepdims=True)) * knorm_ref[h] + bias_max)

    def fold(x):
        return x.reshape(tk // V7X_SUBLANES, V7X_SUBLANES, tq).sum(axis=0)

    def logits(j, h, near):
        s = lax.dot_general(k_ref[rows(j), heads[h]], q_ref[:, heads[h]], _NT,
                            preferred_element_type=f32)
        s = s + madd_ref[rows(j), :]
        return s if near is None else s + bias_ref[h, near]

    acc_ref[...] = jnp.zeros(acc_ref.shape, f32)
    l_ref[...] = jnp.zeros(l_ref.shape, f32)

    def attend(j, near):
        s = [logits(j, h, near) for h in range(N_HEADS_B)]
        e = [jnp.exp(s[h] - bounds[h]) for h in range(N_HEADS_B)]
        for h in range(N_HEADS_B):
            l_ref[h] += fold(e[h])
            acc_ref[h] += jnp.dot(vt_ref[heads[h], rows(j)], e[h].astype(bf16),
                                  preferred_element_type=f32)

    def far_attend(j, carry):
        attend(j, None)
        return carry

    lax.fori_loop(0, qi - 1, far_attend, 0)

    @pl.when(qi >= 1)
    def _():
        attend(qi - 1, 1)

    attend(qi, 0)

    smallest = jnp.min(jnp.sum(l_ref[...], axis=1))

    @pl.when(jnp.logical_not(smallest > SUM_FLOOR))
    def _():
        acc_ref[...] = jnp.zeros(acc_ref.shape, f32)
        first_row = lax.broadcasted_iota(i32, (V7X_SUBLANES, tq), 0) == 0

        def attend_online(j, near):
            for h in range(N_HEADS_B):
                s = logits(j, h, near)
                m_old = m_ref[h]
                m_new = jnp.maximum(m_old, jnp.max(s, axis=0, keepdims=True))
                alpha = jnp.exp(m_old - m_new)
                p = jnp.exp(s - m_new)
                l_new = (alpha * jnp.sum(l_ref[h], axis=0, keepdims=True)
                         + jnp.sum(p, axis=0, keepdims=True))
                l_ref[h] = jnp.where(first_row, l_new, 0.0)
                pv = jnp.dot(vt_ref[heads[h], rows(j)], p.astype(bf16), preferred_element_type=f32)
                acc_ref[h] = alpha * acc_ref[h] + pv
                m_ref[h] = m_new

        m_ref[...] = jnp.full(m_ref.shape, _MASKED, f32)
        l_ref[...] = jnp.zeros(l_ref.shape, f32)

        def far_online(j, carry):
            attend_online(j, None)
            return carry

        lax.fori_loop(0, qi - 1, far_online, 0)

        @pl.when(qi >= 1)
        def _():
            attend_online(qi - 1, 1)

        attend_online(qi, 0)

    for h in range(N_HEADS_B):
        total = jnp.sum(l_ref[h], axis=0, keepdims=True)
        out_ref[:, heads[h]] = (acc_ref[h] / total).T.astype(out_ref.dtype)


def _dsa(rel_bias, q, iq, iwt, k, vt, ik, buckets, *, batch, tq, top_k):
    m, d_b = q.shape
    s = m // batch
    nq = s // tq
    hd = d_b // N_HEADS_B
    tile_rows = lambda bi, qi: (bi * nq + qi, 0)
    batch_rows = lambda bi, qi: (bi, 0)
    return pl.pallas_call(
        functools.partial(_dsa_body, tq=tq, top_k=top_k, seq_bits=int(np.log2(s)),
                          far_bucket=NUM_BUCKETS // 2 - 1),
        grid=(batch, nq),
        in_specs=[
            pl.BlockSpec(memory_space=pltpu.SMEM),
            pl.BlockSpec((tq, d_b), tile_rows),
            pl.BlockSpec((IDX_HEADS, tq, IDX_DIM), lambda bi, qi: (0, bi * nq + qi, 0)),
            pl.BlockSpec((IDX_HEADS, tq), lambda bi, qi: (0, bi * nq + qi)),
            pl.BlockSpec((s, d_b), batch_rows),
            pl.BlockSpec((d_b, s), lambda bi, qi: (0, bi)),
            pl.BlockSpec((s, IDX_DIM), batch_rows),
            pl.BlockSpec(buckets.shape, lambda bi, qi: (0, 0, 0)),
        ],
        out_specs=pl.BlockSpec((tq, d_b), tile_rows),
        out_shape=jax.ShapeDtypeStruct((m, d_b), bf16),
        scratch_shapes=[
            pltpu.VMEM((s, tq), i32),
            pltpu.VMEM((KEY_BITS, s // KEY_BITS, tq), i32),
            pltpu.VMEM((s, tq), f32),
            pltpu.VMEM((N_HEADS_B, 2, tq, tq), f32),
            pltpu.VMEM((N_HEADS_B, 1, tq), f32),
            pltpu.VMEM((N_HEADS_B, 1, tq), f32),
            pltpu.VMEM((N_HEADS_B, V7X_SUBLANES, tq), f32),
            pltpu.VMEM((N_HEADS_B, hd, tq), f32),
        ],
        compiler_params=pltpu.CompilerParams(
            dimension_semantics=("arbitrary", "arbitrary"), vmem_limit_bytes=V7X_VMEM_LIMIT_BYTES),
        name="dsa",
    )(rel_bias, q, iq, iwt, k, vt, ik, buckets)


def _out_ffn_body(x_ref, a_ref, b_ref, woa_ref, wob_ref, gpm_ref, gpf_ref, wg_ref, wu_ref,
                  wd_ref, gpo_ref, o_ref, *, ff_chunk):
    mix = (jnp.dot(a_ref[...], woa_ref[...], preferred_element_type=f32)
           + jnp.dot(b_ref[...], wob_ref[...], preferred_element_type=f32))
    x1 = x_ref[...] + _rms(mix, gpm_ref[...])
    h = _rms(x1, gpf_ref[...]).astype(bf16)
    d_ff = wg_ref.shape[1]
    f = jnp.zeros(x1.shape, f32)
    for c in range(d_ff // ff_chunk):
        cs = slice(c * ff_chunk, (c + 1) * ff_chunk)
        gate = jnp.dot(h, wg_ref[:, cs], preferred_element_type=f32)
        up = jnp.dot(h, wu_ref[:, cs], preferred_element_type=f32)
        act = (gate * jax.nn.sigmoid(gate) * up).astype(bf16)
        f = f + jnp.dot(act, wd_ref[cs, :], preferred_element_type=f32)
    o_ref[...] = x1 + _rms(f, gpo_ref[...])


def _out_ffn(x2, out_a, out_b, w_oa, w_ob, g_pm, g_pf, w_gate, w_up, w_down, g_po, *, tm, ff_chunk):
    m, d = x2.shape
    d_a, d_b, d_ff = out_a.shape[1], out_b.shape[1], w_gate.shape[1]
    row = lambda i: (i, 0)
    const = lambda i: (0, 0)
    resident = functools.partial(pl.BlockSpec, index_map=const, pipeline_mode=pl.Buffered(1))
    return pl.pallas_call(
        functools.partial(_out_ffn_body, ff_chunk=ff_chunk),
        grid=(m // tm,),
        in_specs=[
            pl.BlockSpec((tm, d), row),
            pl.BlockSpec((tm, d_a), row),
            pl.BlockSpec((tm, d_b), row),
            resident((d_a, d)),
            resident((d_b, d)),
            resident((1, d)),
            resident((1, d)),
            resident((d, d_ff)),
            resident((d, d_ff)),
            resident((d_ff, d)),
            resident((1, d)),
        ],
        out_specs=pl.BlockSpec((tm, d), row),
        out_shape=jax.ShapeDtypeStruct((m, d), f32),
        compiler_params=pltpu.CompilerParams(
            dimension_semantics=("arbitrary",), vmem_limit_bytes=V7X_VMEM_LIMIT_BYTES),
        name="out_ffn",
    )(x2, out_a, out_b, w_oa, w_ob, g_pm, g_pf, w_gate, w_up, w_down, g_po)


def _t5_bucket(rel):
    nb = NUM_BUCKETS // 2
    ret = (rel > 0).astype(jnp.int32) * nb
    n = jnp.abs(rel)
    max_exact = nb // 2
    nf = jnp.maximum(n, 1).astype(jnp.float32)
    large = max_exact + (jnp.log(nf / max_exact) / np.float32(np.log(MAX_DISTANCE / max_exact))
                         * (nb - max_exact)).astype(jnp.int32)
    large = jnp.minimum(large, nb - 1)
    return ret + jnp.where(n < max_exact, n, large)


def _near_buckets(tq):
    assert tq + 1 >= MAX_DISTANCE
    rel0 = jnp.arange(tq, dtype=jnp.int32)[:, None] - jnp.arange(tq, dtype=jnp.int32)[None, :]
    return _t5_bucket(jnp.stack([rel0, rel0 - tq]))


def kernel(x, g_pre_mix, w_in, sgu_ln_g, sgu_ln_b, sgu_w, sgu_b, rel_bias, w_o, g_post_mix,
           g_pre_ffn, w_gate, w_up, w_down, g_post_ffn):
    b, s, d = x.shape
    depth = w_in.shape[0]
    d_a = d // 2
    d_b = d - d_a
    d_main = 2 * d_a + 3 * d_b + IDX_HEADS * IDX_DIM
    top_k = min(MAX_TOPK, s // 4)
    tq = KEY_BITS * V7X_SUBLANES
    tm = 512
    x2 = x.reshape(b * s, d)
    for l in range(depth):
        w_l = w_in[l]
        w_main = w_l[:, :d_main].astype(bf16)
        w_ik = jnp.pad(w_l[:, d_main:], ((0, 0), (0, V7X_LANES - (IDX_DIM + IDX_HEADS)))).astype(bf16)
        out_a, q, k, vt, iq, ik, iwt = _proj_sgu(
            x2, g_pre_mix[l][None], w_main, w_ik, sgu_ln_g[l][None], sgu_ln_b[l][None],
            sgu_w[l], jnp.transpose(sgu_b[l]), d_a=d_a, d_b=d_b, tm=tm)
        out_b = _dsa(rel_bias, q, iq, iwt, k, vt, ik, _near_buckets(tq), batch=b, tq=tq, top_k=top_k)
        w_ol = w_o[l].astype(bf16)
        x2 = _out_ffn(x2, out_a, out_b, w_ol[:d_a], w_ol[d_a:],
                      g_post_mix[l][None], g_pre_ffn[l][None], w_gate[l].astype(bf16),
                      w_up[l].astype(bf16), w_down[l].astype(bf16), g_post_ffn[l][None],
                      tm=tm, ff_chunk=256)
    return x2.reshape(b, s, d)
```

```python
import functools

import numpy as np
import jax
import jax.numpy as jnp
from jax import lax
from jax.experimental import pallas as pl
from jax.experimental.pallas import tpu as pltpu

CHUNK = 64
SGU_BLOCK = 128
SGU_GROUPS = 4
N_HEADS_B = 4
IDX_HEADS = 8
IDX_DIM = 64
MAX_TOPK = 256
NUM_BUCKETS = 32
MAX_DISTANCE = 128
EPS = 1e-6

V7X_LANES = 128
V7X_SUBLANES = 8
V7X_VMEM_LIMIT_BYTES = 56 * 1024 * 1024

KEY_BITS = 32
_INT_MIN = -(2 ** 31)
_MASKED = -1e30
LOG2_E = float(np.log2(np.e))
SUM_FLOOR = 1e-30

_NT = (((1,), (1,)), ((), ()))

f32 = jnp.float32
bf16 = jnp.bfloat16
i32 = jnp.int32


def _rms(x, g):
    return x * lax.rsqrt(jnp.mean(x * x, axis=-1, keepdims=True) + EPS) * g


def _proj_sgu_body(x_ref, g_ref, w_ref, wik_ref, lng_ref, lnb_ref, sw_ref, sbt_ref,
                   outa_ref, q_ref, k_ref, vt_ref, iq_ref, ik_ref, iwt_ref, ksq_ref, *, d_a, d_b):
    tm = x_ref.shape[0]
    h = _rms(x_ref[...], g_ref[...]).astype(bf16)

    def proj(c0, width):
        return jnp.dot(h, w_ref[:, c0:c0 + width], preferred_element_type=f32)

    u = proj(0, d_a)
    a_v = proj(d_a, d_a)
    c0 = 2 * d_a
    q_ref[...] = (proj(c0, d_b) * np.float32((d_b // N_HEADS_B) ** -0.5 * LOG2_E)).astype(bf16)
    k = proj(c0 + d_b, d_b).astype(bf16)
    k_ref[...] = k
    kt = k.astype(f32).T
    hd = d_b // N_HEADS_B
    ksq = [jnp.sum(kt[a * hd:(a + 1) * hd] ** 2, axis=0, keepdims=True) for a in range(N_HEADS_B)]
    ksq_ref[...] = jnp.concatenate(ksq + [jnp.zeros((V7X_SUBLANES - N_HEADS_B, tm), f32)], axis=0)
    vt_ref[...] = proj(c0 + 2 * d_b, d_b).astype(bf16).T
    iq = (proj(c0 + 3 * d_b, IDX_HEADS * IDX_DIM) * np.float32(IDX_DIM ** -0.5)).astype(bf16)
    for hh in range(IDX_HEADS):
        iq_ref[hh] = iq[:, hh * IDX_DIM:(hh + 1) * IDX_DIM]
    ikw = jnp.dot(h, wik_ref[...], preferred_element_type=f32)
    ik_ref[...] = ikw[:, :IDX_DIM].astype(bf16)
    iwt_ref[...] = ikw.T[IDX_DIM:IDX_DIM + IDX_HEADS, :] * np.float32(IDX_HEADS ** -0.5)

    mu = jnp.mean(a_v, axis=-1, keepdims=True)
    cen = a_v - mu
    var = jnp.mean(cen * cen, axis=-1, keepdims=True)
    vn = (cen * lax.rsqrt(var + EPS) * lng_ref[...] + lnb_ref[...]).astype(bf16)

    row_chunk = lax.broadcasted_iota(i32, (SGU_BLOCK, SGU_BLOCK), 0) // CHUNK
    col_chunk = lax.broadcasted_iota(i32, (SGU_BLOCK, SGU_BLOCK), 1) // CHUNK
    causal = col_chunk <= row_chunk
    gdim = d_a // SGU_GROUPS
    for g in range(SGU_GROUPS):
        w_g = jnp.where(causal, sw_ref[g], 0.0).astype(bf16)
        b_g = sbt_ref[:, g:g + 1]
        for c in range(tm // SGU_BLOCK):
            rows = slice(c * SGU_BLOCK, (c + 1) * SGU_BLOCK)
            cols = slice(g * gdim, (g + 1) * gdim)
            mixed = jnp.dot(w_g, vn[rows, cols], preferred_element_type=f32) + b_g
            outa_ref[rows, cols] = (u[rows, cols] * mixed).astype(bf16)


def _proj_sgu(x2, g_pre, w_main, w_ik, ln_g, ln_b, sgu_w, sgu_bt, *, d_a, d_b, tm):
    m, d = x2.shape
    const = lambda i: (0, 0)
    row = lambda i: (i, 0)
    col = lambda i: (0, i)
    return pl.pallas_call(
        functools.partial(_proj_sgu_body, d_a=d_a, d_b=d_b),
        grid=(m // tm,),
        in_specs=[
            pl.BlockSpec((tm, d), row),
            pl.BlockSpec((1, d), const),
            pl.BlockSpec(w_main.shape, const),
            pl.BlockSpec(w_ik.shape, const),
            pl.BlockSpec((1, d_a), const),
            pl.BlockSpec((1, d_a), const),
            pl.BlockSpec(sgu_w.shape, lambda i: (0, 0, 0)),
            pl.BlockSpec(sgu_bt.shape, const),
        ],
        out_specs=[
            pl.BlockSpec((tm, d_a), row),
            pl.BlockSpec((tm, d_b), row),
            pl.BlockSpec((tm, d_b), row),
            pl.BlockSpec((d_b, tm), col),
            pl.BlockSpec((IDX_HEADS, tm, IDX_DIM), lambda i: (0, i, 0)),
            pl.BlockSpec((tm, IDX_DIM), row),
            pl.BlockSpec((IDX_HEADS, tm), col),
            pl.BlockSpec((V7X_SUBLANES, tm), col),
        ],
        out_shape=[
            jax.ShapeDtypeStruct((m, d_a), bf16),
            jax.ShapeDtypeStruct((m, d_b), bf16),
            jax.ShapeDtypeStruct((m, d_b), bf16),
            jax.ShapeDtypeStruct((d_b, m), bf16),
            jax.ShapeDtypeStruct((IDX_HEADS, m, IDX_DIM), bf16),
            jax.ShapeDtypeStruct((m, IDX_DIM), bf16),
            jax.ShapeDtypeStruct((IDX_HEADS, m), f32),
            jax.ShapeDtypeStruct((V7X_SUBLANES, m), f32),
        ],
        compiler_params=pltpu.CompilerParams(
            dimension_semantics=("arbitrary",), vmem_limit_bytes=V7X_VMEM_LIMIT_BYTES),
        name="proj_sgu",
    )(x2, g_pre, w_main, w_ik, ln_g, ln_b, sgu_w, sgu_bt)


def _bit_planes(words):
    a = list(words)
    j, m = 16, 0x0000FFFF
    while j:
        k = 0
        while k < KEY_BITS:
            t = (a[k] ^ lax.shift_right_logical(a[k + j], i32(j))) & i32(m)
            a[k] = a[k] ^ t
            a[k + j] = a[k + j] ^ lax.shift_left(t, i32(j))
            k = (k + j + 1) & ~j
        j >>= 1
        m = (m ^ (m << j)) & 0xFFFFFFFF
    return a


def _dsa_body(relb_ref, q_ref, iq_ref, iwt_ref, k_ref, vt_ref, ik_ref, ksq_ref, bkt_ref, out_ref,
              keys_ref, planes_ref, madd_ref, bias_ref, sa_ref, sb_ref, m_ref, l_ref, acc_ref,
              *, tq, top_k, seq_bits, far_bucket):
    tk = tq
    group = KEY_BITS * V7X_SUBLANES
    assert tk == group
    hd = q_ref.shape[1] // N_HEADS_B
    qi = pl.program_id(1)
    n_tiles = qi + 1
    int_min = i32(_INT_MIN)

    def rows(j):
        return pl.ds(pl.multiple_of(j * tk, tk), tk)

    @pl.when((pl.program_id(0) == 0) & (qi == 0))
    def _():
        planes_ref[...] = jnp.zeros(planes_ref.shape, i32)
        for d in range(2):
            bucket = bkt_ref[d]
            for h in range(N_HEADS_B):
                far = relb_ref[far_bucket, h]
                tab = jnp.zeros((tk, tq), f32)
                for b in range(NUM_BUCKETS):
                    tab = jnp.where(bucket == b, (relb_ref[b, h] - far) * LOG2_E, tab)
                bias_ref[h, d] = tab

    def score_tile(j, diagonal):
        ik_j = ik_ref[rows(j), :]
        sc = jnp.zeros((tk, tq), f32)
        for h in range(IDX_HEADS):
            d = lax.dot_general(ik_j, iq_ref[h], _NT, preferred_element_type=f32)
            sc = sc + iwt_ref[h:h + 1, :] * jnp.maximum(d, 0.0)
        bits = lax.bitcast_convert_type(sc, i32)
        key = bits ^ ((bits >> 31) & i32(0x7FFFFFFF))
        if diagonal:
            key_chunk = lax.broadcasted_iota(i32, (tk, tq), 0) // CHUNK
            qry_chunk = lax.broadcasted_iota(i32, (tk, tq), 1) // CHUNK
            key = jnp.where(key_chunk <= qry_chunk, key, int_min)
        keys_ref[rows(j), :] = key
        return key

    def plane_tile(j, key):
        prow = pl.ds(pl.multiple_of(j * V7X_SUBLANES, V7X_SUBLANES), V7X_SUBLANES)
        for lt in range(tq // V7X_LANES):
            lanes = slice(lt * V7X_LANES, (lt + 1) * V7X_LANES)
            planes = _bit_planes([key[V7X_SUBLANES * i:V7X_SUBLANES * (i + 1), lanes]
                                  for i in range(KEY_BITS)])
            planes_ref[0, prow, lanes] = ~planes[0]
            for p in range(1, KEY_BITS):
                planes_ref[p, prow, lanes] = planes[p]

    def far_scores(j, prev):
        key = score_tile(j, False)
        plane_tile(j - 1, prev)
        return key

    prev = lax.fori_loop(1, qi, far_scores, score_tile(0, False))
    last = score_tile(qi, True)
    plane_tile(jnp.maximum(qi - 1, 0), prev)
    plane_tile(qi, last)

    n_prow = planes_ref.shape[1]
    live_rows = lax.broadcasted_iota(i32, (n_prow, tq), 0) < n_tiles * V7X_SUBLANES

    def radix_pass(p, carry):
        alive, n_above, thr_bits = carry
        ones = alive & planes_ref[p]
        n_ones = jnp.sum(lax.population_count(ones), axis=0, keepdims=True)
        take = (n_above + n_ones) >= top_k
        alive = jnp.where(take, ones, alive ^ ones)
        n_above = jnp.where(take, n_above, n_above + n_ones)
        thr_bits = thr_bits | jnp.where(take, lax.shift_left(i32(1), 31 - p), 0)
        return alive, n_above, thr_bits

    zero_row = jnp.zeros((1, tq), i32)
    alive, n_above, thr_bits = lax.fori_loop(
        0, KEY_BITS, radix_pass, (jnp.where(live_rows, i32(-1), 0), zero_row, zero_row))
    thr = thr_bits ^ int_min
    n_equal = jnp.sum(lax.population_count(alive), axis=0, keepdims=True)
    need = top_k - n_above

    tie = (n_equal > need) & (thr > int_min)

    @pl.when(jnp.max(tie.astype(i32)) > 0)
    def _():
        def key_idx(r0):
            return r0 + lax.broadcasted_iota(i32, (tk, tq), 0)

        def count_where(pred):
            def body(j, acc):
                r0 = pl.multiple_of(j * tk, tk)
                hit = pred(keys_ref[pl.ds(r0, tk), :], r0).astype(i32)
                return acc + hit.reshape(tk // V7X_SUBLANES, V7X_SUBLANES, tq).sum(axis=0)
            acc = lax.fori_loop(0, n_tiles, body, jnp.zeros((V7X_SUBLANES, tq), i32))
            return jnp.sum(acc, axis=0, keepdims=True)

        def index_pass(p, last):
            cand = last + lax.shift_left(i32(1), seq_bits - 1 - p)
            cnt = count_where(lambda kt, r0: (kt == thr) & (key_idx(r0) < cand))
            return jnp.where(cnt < need, cand, last)

        last = lax.fori_loop(0, seq_bits, index_pass, zero_row)

        def drop(j, carry):
            r0 = pl.multiple_of(j * tk, tk)
            kt = keys_ref[pl.ds(r0, tk), :]
            surplus = tie & (kt == thr) & (key_idx(r0) > last)
            keys_ref[pl.ds(r0, tk), :] = jnp.where(surplus, int_min, kt)
            return carry

        lax.fori_loop(0, n_tiles, drop, 0)

    thr_eff = jnp.maximum(thr, int_min + 1)

    def make_mask(j, carry):
        madd_ref[rows(j), :] = jnp.where(keys_ref[rows(j), :] >= thr_eff, 0.0, _MASKED).astype(f32)
        return carry

    lax.fori_loop(0, n_tiles, make_mask, 0)

    heads = [slice(h * hd, (h + 1) * hd) for h in range(N_HEADS_B)]

    bounds = []
    for h, hs in enumerate(heads):
        qt = q_ref[:, hs].astype(f32).T
        far = relb_ref[far_bucket, h]
        bias_max = jnp.float32(0.0)
        for b in range(NUM_BUCKETS):
            bias_max = jnp.maximum(bias_max, (relb_ref[b, h] - far) * LOG2_E)
        longest_key = jnp.sqrt(jnp.max(ksq_ref[h:h + 1, :]))
        bounds.append(jnp.sqrt(jnp.sum(qt * qt, axis=0, keepdims=True)) * longest_key + bias_max)

    def fold(x):
        return x.reshape(tk // V7X_SUBLANES, V7X_SUBLANES, tq).sum(axis=0)

    def qk(j, h):
        return lax.dot_general(k_ref[rows(j), heads[h]], q_ref[:, heads[h]], _NT,
                               preferred_element_type=f32)

    def logits(j, h, near):
        s = qk(j, h) + madd_ref[rows(j), :]
        return s if near is None else s + bias_ref[h, near]

    acc_ref[...] = jnp.zeros(acc_ref.shape, f32)
    l_ref[...] = jnp.zeros(l_ref.shape, f32)

    def issue(j, buf):
        for h in range(N_HEADS_B):
            buf[h] = qk(j, h)

    def consume(j, buf, near):
        madd = madd_ref[rows(j), :]
        for h in range(N_HEADS_B):
            x = buf[h] + madd
            if near is not None:
                x = x + bias_ref[h, near]
            e = jnp.exp2(x - bounds[h])
            l_ref[h] += fold(e)
            acc_ref[h] += jnp.dot(vt_ref[heads[h], rows(j)], e.astype(bf16),
                                  preferred_element_type=f32)

    n_far = jnp.maximum(qi - 1, 0)
    issue(0, sa_ref)

    def far_pair(i, carry):
        j = 2 * i
        issue(j + 1, sb_ref)
        consume(j, sa_ref, None)
        issue(j + 2, sa_ref)
        consume(j + 1, sb_ref, None)
        return carry

    lax.fori_loop(0, n_far // 2, far_pair, 0)
    j0 = 2 * (n_far // 2)

    @pl.when(n_far % 2 == 1)
    def _():
        issue(j0 + 1, sb_ref)
        consume(j0, sa_ref, None)
        issue(j0 + 2, sa_ref)
        consume(j0 + 1, sb_ref, 1)
        consume(j0 + 2, sa_ref, 0)

    @pl.when((n_far % 2 == 0) & (qi >= 1))
    def _():
        issue(j0 + 1, sb_ref)
        consume(j0, sa_ref, 1)
        consume(j0 + 1, sb_ref, 0)

    @pl.when(qi == 0)
    def _():
        consume(0, sa_ref, 0)

    smallest = jnp.min(jnp.sum(l_ref[...], axis=1))

    @pl.when(jnp.logical_not(smallest > SUM_FLOOR))
    def _():
        acc_ref[...] = jnp.zeros(acc_ref.shape, f32)
        first_row = lax.broadcasted_iota(i32, (V7X_SUBLANES, tq), 0) == 0

        def attend_online(j, near):
            for h in range(N_HEADS_B):
                s = logits(j, h, near)
                m_old = m_ref[h]
                m_new = jnp.maximum(m_old, jnp.max(s, axis=0, keepdims=True))
                alpha = jnp.exp2(m_old - m_new)
                p = jnp.exp2(s - m_new)
                l_new = (alpha * jnp.sum(l_ref[h], axis=0, keepdims=True)
                         + jnp.sum(p, axis=0, keepdims=True))
                l_ref[h] = jnp.where(first_row, l_new, 0.0)
                pv = jnp.dot(vt_ref[heads[h], rows(j)], p.astype(bf16), preferred_element_type=f32)
                acc_ref[h] = alpha * acc_ref[h] + pv
                m_ref[h] = m_new

        m_ref[...] = jnp.full(m_ref.shape, _MASKED, f32)
        l_ref[...] = jnp.zeros(l_ref.shape, f32)

        def far_online(j, carry):
            attend_online(j, None)
            return carry

        lax.fori_loop(0, qi - 1, far_online, 0)

        @pl.when(qi >= 1)
        def _():
            attend_online(qi - 1, 1)

        attend_online(qi, 0)

    for h in range(N_HEADS_B):
        total = jnp.sum(l_ref[h], axis=0, keepdims=True)
        out_ref[:, heads[h]] = (acc_ref[h] / total).T.astype(out_ref.dtype)


def _dsa(rel_bias, q, iq, iwt, k, vt, ik, ksq, buckets, *, batch, tq, top_k):
    m, d_b = q.shape
    s = m // batch
    nq = s // tq
    hd = d_b // N_HEADS_B
    tile_rows = lambda bi, qi: (bi * nq + qi, 0)
    batch_rows = lambda bi, qi: (bi, 0)
    return pl.pallas_call(
        functools.partial(_dsa_body, tq=tq, top_k=top_k, seq_bits=int(np.log2(s)),
                          far_bucket=NUM_BUCKETS // 2 - 1),
        grid=(batch, nq),
        in_specs=[
            pl.BlockSpec(memory_space=pltpu.SMEM),
            pl.BlockSpec((tq, d_b), tile_rows),
            pl.BlockSpec((IDX_HEADS, tq, IDX_DIM), lambda bi, qi: (0, bi * nq + qi, 0)),
            pl.BlockSpec((IDX_HEADS, tq), lambda bi, qi: (0, bi * nq + qi)),
            pl.BlockSpec((s, d_b), batch_rows),
            pl.BlockSpec((d_b, s), lambda bi, qi: (0, bi)),
            pl.BlockSpec((s, IDX_DIM), batch_rows),
            pl.BlockSpec((V7X_SUBLANES, s), lambda bi, qi: (0, bi)),
            pl.BlockSpec(buckets.shape, lambda bi, qi: (0, 0, 0)),
        ],
        out_specs=pl.BlockSpec((tq, d_b), tile_rows),
        out_shape=jax.ShapeDtypeStruct((m, d_b), bf16),
        scratch_shapes=[
            pltpu.VMEM((s, tq), i32),
            pltpu.VMEM((KEY_BITS, s // KEY_BITS, tq), i32),
            pltpu.VMEM((s, tq), f32),
            pltpu.VMEM((N_HEADS_B, 2, tq, tq), f32),
            pltpu.VMEM((N_HEADS_B, tq, tq), f32),
            pltpu.VMEM((N_HEADS_B, tq, tq), f32),
            pltpu.VMEM((N_HEADS_B, 1, tq), f32),
            pltpu.VMEM((N_HEADS_B, V7X_SUBLANES, tq), f32),
            pltpu.VMEM((N_HEADS_B, hd, tq), f32),
        ],
        compiler_params=pltpu.CompilerParams(
            dimension_semantics=("arbitrary", "arbitrary"), vmem_limit_bytes=V7X_VMEM_LIMIT_BYTES),
        name="dsa",
    )(rel_bias, q, iq, iwt, k, vt, ik, ksq, buckets)


def _out_ffn_body(x_ref, a_ref, b_ref, woa_ref, wob_ref, gpm_ref, gpf_ref, wg_ref, wu_ref,
                  wd_ref, gpo_ref, o_ref, *, ff_chunk):
    mix = (jnp.dot(a_ref[...], woa_ref[...], preferred_element_type=f32)
           + jnp.dot(b_ref[...], wob_ref[...], preferred_element_type=f32))
    x1 = x_ref[...] + _rms(mix, gpm_ref[...])
    h = _rms(x1, gpf_ref[...]).astype(bf16)
    d_ff = wg_ref.shape[1]
    f = jnp.zeros(x1.shape, f32)
    for c in range(d_ff // ff_chunk):
        cs = slice(c * ff_chunk, (c + 1) * ff_chunk)
        gate = jnp.dot(h, wg_ref[:, cs], preferred_element_type=f32)
        up = jnp.dot(h, wu_ref[:, cs], preferred_element_type=f32)
        act = (gate * jax.nn.sigmoid(gate) * up).astype(bf16)
        f = f + jnp.dot(act, wd_ref[cs, :], preferred_element_type=f32)
    o_ref[...] = x1 + _rms(f, gpo_ref[...])


def _out_ffn(x2, out_a, out_b, w_oa, w_ob, g_pm, g_pf, w_gate, w_up, w_down, g_po, *, tm, ff_chunk):
    m, d = x2.shape
    d_a, d_b, d_ff = out_a.shape[1], out_b.shape[1], w_gate.shape[1]
    row = lambda i: (i, 0)
    const = lambda i: (0, 0)
    resident = functools.partial(pl.BlockSpec, index_map=const, pipeline_mode=pl.Buffered(1))
    return pl.pallas_call(
        functools.partial(_out_ffn_body, ff_chunk=ff_chunk),
        grid=(m // tm,),
        in_specs=[
            pl.BlockSpec((tm, d), row),
            pl.BlockSpec((tm, d_a), row),
            pl.BlockSpec((tm, d_b), row),
            resident((d_a, d)),
            resident((d_b, d)),
            resident((1, d)),
            resident((1, d)),
            resident((d, d_ff)),
            resident((d, d_ff)),
            resident((d_ff, d)),
            resident((1, d)),
        ],
        out_specs=pl.BlockSpec((tm, d), row),
        out_shape=jax.ShapeDtypeStruct((m, d), f32),
        compiler_params=pltpu.CompilerParams(
            dimension_semantics=("arbitrary",), vmem_limit_bytes=V7X_VMEM_LIMIT_BYTES),
        name="out_ffn",
    )(x2, out_a, out_b, w_oa, w_ob, g_pm, g_pf, w_gate, w_up, w_down, g_po)


def _t5_bucket(rel):
    nb = NUM_BUCKETS // 2
    ret = (rel > 0).astype(jnp.int32) * nb
    n = jnp.abs(rel)
    max_exact = nb // 2
    nf = jnp.maximum(n, 1).astype(jnp.float32)
    large = max_exact + (jnp.log(nf / max_exact) / np.float32(np.log(MAX_DISTANCE / max_exact))
                         * (nb - max_exact)).astype(jnp.int32)
    large = jnp.minimum(large, nb - 1)
    return ret + jnp.where(n < max_exact, n, large)


def _near_buckets(tq):
    assert tq + 1 >= MAX_DISTANCE
    rel0 = jnp.arange(tq, dtype=jnp.int32)[:, None] - jnp.arange(tq, dtype=jnp.int32)[None, :]
    return _t5_bucket(jnp.stack([rel0, rel0 - tq]))


def kernel(x, g_pre_mix, w_in, sgu_ln_g, sgu_ln_b, sgu_w, sgu_b, rel_bias, w_o, g_post_mix,
           g_pre_ffn, w_gate, w_up, w_down, g_post_ffn):
    b, s, d = x.shape
    depth = w_in.shape[0]
    d_a = d // 2
    d_b = d - d_a
    d_main = 2 * d_a + 3 * d_b + IDX_HEADS * IDX_DIM
    top_k = min(MAX_TOPK, s // 4)
    tq = KEY_BITS * V7X_SUBLANES
    tm = 512
    x2 = x.reshape(b * s, d)
    for l in range(depth):
        w_l = w_in[l]
        w_main = w_l[:, :d_main].astype(bf16)
        w_ik = jnp.pad(w_l[:, d_main:], ((0, 0), (0, V7X_LANES - (IDX_DIM + IDX_HEADS)))).astype(bf16)
        out_a, q, k, vt, iq, ik, iwt, ksq = _proj_sgu(
            x2, g_pre_mix[l][None], w_main, w_ik, sgu_ln_g[l][None], sgu_ln_b[l][None],
            sgu_w[l], jnp.transpose(sgu_b[l]), d_a=d_a, d_b=d_b, tm=tm)
        out_b = _dsa(rel_bias, q, iq, iwt, k, vt, ik, ksq, _near_buckets(tq), batch=b, tq=tq, top_k=top_k)
        w_ol = w_o[l].astype(bf16)
        x2 = _out_ffn(x2, out_a, out_b, w_ol[:d_a], w_ol[d_a:],
                      g_post_mix[l][None], g_pre_ffn[l][None], w_gate[l].astype(bf16),
                      w_up[l].astype(bf16), w_down[l].astype(bf16), g_post_ffn[l][None],
                      tm=tm, ff_chunk=256)
    return x2.reshape(b, s, d)
```

```python
import functools

import numpy as np
import jax
import jax.numpy as jnp
from jax import lax
from jax.experimental import pallas as pl
from jax.experimental.pallas import tpu as pltpu

CHUNK = 64
SGU_BLOCK = 128
SGU_GROUPS = 4
N_HEADS_B = 4
IDX_HEADS = 8
IDX_DIM = 64
MAX_TOPK = 256
NUM_BUCKETS = 32
MAX_DISTANCE = 128
EPS = 1e-6

V7X_LANES = 128
V7X_SUBLANES = 8
V7X_VMEM_LIMIT_BYTES = 56 * 1024 * 1024

KEY_BITS = 32
_INT_MIN = -(2 ** 31)
_MASKED = -1e30
LOG2_E = float(np.log2(np.e))
SUM_FLOOR = 1e-30

_NT = (((1,), (1,)), ((), ()))

f32 = jnp.float32
bf16 = jnp.bfloat16
i32 = jnp.int32


def _rms(x, g):
    return x * lax.rsqrt(jnp.mean(x * x, axis=-1, keepdims=True) + EPS) * g


def _proj_sgu_body(x_ref, g_ref, w_ref, wik_ref, lng_ref, lnb_ref, sw_ref, sbt_ref,
                   outa_ref, q_ref, k_ref, vt_ref, iq_ref, ik_ref, iwt_ref, ksq_ref, *, d_a, d_b):
    tm = x_ref.shape[0]
    h = _rms(x_ref[...], g_ref[...]).astype(bf16)

    def proj(c0, width):
        return jnp.dot(h, w_ref[:, c0:c0 + width], preferred_element_type=f32)

    u = proj(0, d_a)
    a_v = proj(d_a, d_a)
    c0 = 2 * d_a
    q_ref[...] = (proj(c0, d_b) * np.float32((d_b // N_HEADS_B) ** -0.5 * LOG2_E)).astype(bf16)
    k = proj(c0 + d_b, d_b).astype(bf16)
    k_ref[...] = k
    kt = k.astype(f32).T
    hd = d_b // N_HEADS_B
    ksq = [jnp.sum(kt[a * hd:(a + 1) * hd] ** 2, axis=0, keepdims=True) for a in range(N_HEADS_B)]
    ksq_ref[...] = jnp.concatenate(ksq + [jnp.zeros((V7X_SUBLANES - N_HEADS_B, tm), f32)], axis=0)
    vt_ref[...] = proj(c0 + 2 * d_b, d_b).astype(bf16).T
    iq = (proj(c0 + 3 * d_b, IDX_HEADS * IDX_DIM) * np.float32(IDX_DIM ** -0.5)).astype(bf16)
    for hh in range(IDX_HEADS):
        iq_ref[hh] = iq[:, hh * IDX_DIM:(hh + 1) * IDX_DIM]
    ikw = jnp.dot(h, wik_ref[...], preferred_element_type=f32)
    ik_ref[...] = ikw[:, :IDX_DIM].astype(bf16)
    iwt_ref[...] = ikw.T[IDX_DIM:IDX_DIM + IDX_HEADS, :] * np.float32(IDX_HEADS ** -0.5)

    mu = jnp.mean(a_v, axis=-1, keepdims=True)
    cen = a_v - mu
    var = jnp.mean(cen * cen, axis=-1, keepdims=True)
    vn = (cen * lax.rsqrt(var + EPS) * lng_ref[...] + lnb_ref[...]).astype(bf16)

    row_chunk = lax.broadcasted_iota(i32, (SGU_BLOCK, SGU_BLOCK), 0) // CHUNK
    col_chunk = lax.broadcasted_iota(i32, (SGU_BLOCK, SGU_BLOCK), 1) // CHUNK
    causal = col_chunk <= row_chunk
    gdim = d_a // SGU_GROUPS
    for g in range(SGU_GROUPS):
        w_g = jnp.where(causal, sw_ref[g], 0.0).astype(bf16)
        b_g = sbt_ref[:, g:g + 1]
        for c in range(tm // SGU_BLOCK):
            rows = slice(c * SGU_BLOCK, (c + 1) * SGU_BLOCK)
            cols = slice(g * gdim, (g + 1) * gdim)
            mixed = jnp.dot(w_g, vn[rows, cols], preferred_element_type=f32) + b_g
            outa_ref[rows, cols] = (u[rows, cols] * mixed).astype(bf16)


def _proj_sgu(x2, g_pre, w_main, w_ik, ln_g, ln_b, sgu_w, sgu_bt, *, d_a, d_b, tm):
    m, d = x2.shape
    const = lambda i: (0, 0)
    row = lambda i: (i, 0)
    col = lambda i: (0, i)
    return pl.pallas_call(
        functools.partial(_proj_sgu_body, d_a=d_a, d_b=d_b),
        grid=(m // tm,),
        in_specs=[
            pl.BlockSpec((tm, d), row),
            pl.BlockSpec((1, d), const),
            pl.BlockSpec(w_main.shape, const),
            pl.BlockSpec(w_ik.shape, const),
            pl.BlockSpec((1, d_a), const),
            pl.BlockSpec((1, d_a), const),
            pl.BlockSpec(sgu_w.shape, lambda i: (0, 0, 0)),
            pl.BlockSpec(sgu_bt.shape, const),
        ],
        out_specs=[
            pl.BlockSpec((tm, d_a), row),
            pl.BlockSpec((tm, d_b), row),
            pl.BlockSpec((tm, d_b), row),
            pl.BlockSpec((d_b, tm), col),
            pl.BlockSpec((IDX_HEADS, tm, IDX_DIM), lambda i: (0, i, 0)),
            pl.BlockSpec((tm, IDX_DIM), row),
            pl.BlockSpec((IDX_HEADS, tm), col),
            pl.BlockSpec((V7X_SUBLANES, tm), col),
        ],
        out_shape=[
            jax.ShapeDtypeStruct((m, d_a), bf16),
            jax.ShapeDtypeStruct((m, d_b), bf16),
            jax.ShapeDtypeStruct((m, d_b), bf16),
            jax.ShapeDtypeStruct((d_b, m), bf16),
            jax.ShapeDtypeStruct((IDX_HEADS, m, IDX_DIM), bf16),
            jax.ShapeDtypeStruct((m, IDX_DIM), bf16),
            jax.ShapeDtypeStruct((IDX_HEADS, m), f32),
            jax.ShapeDtypeStruct((V7X_SUBLANES, m), f32),
        ],
        compiler_params=pltpu.CompilerParams(
            dimension_semantics=("arbitrary",), vmem_limit_bytes=V7X_VMEM_LIMIT_BYTES),
        name="proj_sgu",
    )(x2, g_pre, w_main, w_ik, ln_g, ln_b, sgu_w, sgu_bt)


def _bit_planes(words):
    a = list(words)
    j, m = 16, 0x0000FFFF
    while j:
        k = 0
        while k < KEY_BITS:
            t = (a[k] ^ lax.shift_right_logical(a[k + j], i32(j))) & i32(m)
            a[k] = a[k] ^ t
            a[k + j] = a[k + j] ^ lax.shift_left(t, i32(j))
            k = (k + j + 1) & ~j
        j >>= 1
        m = (m ^ (m << j)) & 0xFFFFFFFF
    return a


def _dsa_body(relb_ref, q_ref, iq_ref, iwt_ref, k_ref, vt_ref, ik_ref, ksq_ref, bkt_ref, out_ref,
              keys_ref, planes_ref, madd_ref, bias_ref, sa_ref, sb_ref, m_ref, l_ref, acc_ref,
              *, tq, top_k, seq_bits, far_bucket):
    tk = tq
    group = KEY_BITS * V7X_SUBLANES
    assert tk == group
    hd = q_ref.shape[1] // N_HEADS_B
    qi = pl.program_id(1)
    n_tiles = qi + 1
    int_min = i32(_INT_MIN)

    def rows(j):
        return pl.ds(pl.multiple_of(j * tk, tk), tk)

    @pl.when((pl.program_id(0) == 0) & (qi == 0))
    def _():
        planes_ref[...] = jnp.zeros(planes_ref.shape, i32)
        for d in range(2):
            bucket = bkt_ref[d]
            for h in range(N_HEADS_B):
                far = relb_ref[far_bucket, h]
                tab = jnp.zeros((tk, tq), f32)
                for b in range(NUM_BUCKETS):
                    tab = jnp.where(bucket == b, (relb_ref[b, h] - far) * LOG2_E, tab)
                bias_ref[h, d] = tab

    def score_tile(j, diagonal):
        ik_j = ik_ref[rows(j), :]
        sc = jnp.zeros((tk, tq), f32)
        for h in range(IDX_HEADS):
            d = lax.dot_general(ik_j, iq_ref[h], _NT, preferred_element_type=f32)
            sc = sc + iwt_ref[h:h + 1, :] * jnp.maximum(d, 0.0)
        bits = lax.bitcast_convert_type(sc, i32)
        key = bits ^ ((bits >> 31) & i32(0x7FFFFFFF))
        if diagonal:
            key_chunk = lax.broadcasted_iota(i32, (tk, tq), 0) // CHUNK
            qry_chunk = lax.broadcasted_iota(i32, (tk, tq), 1) // CHUNK
            key = jnp.where(key_chunk <= qry_chunk, key, int_min)
        keys_ref[rows(j), :] = key
        return key

    def plane_tile(j, key):
        prow = pl.ds(pl.multiple_of(j * V7X_SUBLANES, V7X_SUBLANES), V7X_SUBLANES)
        for lt in range(tq // V7X_LANES):
            lanes = slice(lt * V7X_LANES, (lt + 1) * V7X_LANES)
            planes = _bit_planes([key[V7X_SUBLANES * i:V7X_SUBLANES * (i + 1), lanes]
                                  for i in range(KEY_BITS)])
            planes_ref[0, prow, lanes] = ~planes[0]
            for p in range(1, KEY_BITS):
                planes_ref[p, prow, lanes] = planes[p]

    def far_scores(j, prev):
        key = score_tile(j, False)
        plane_tile(j - 1, prev)
        return key

    prev = lax.fori_loop(1, qi, far_scores, score_tile(0, False))
    last = score_tile(qi, True)
    plane_tile(jnp.maximum(qi - 1, 0), prev)
    plane_tile(qi, last)

    n_prow = planes_ref.shape[1]
    live_rows = lax.broadcasted_iota(i32, (n_prow, tq), 0) < n_tiles * V7X_SUBLANES

    zero_row = jnp.zeros((1, tq), i32)
    alive, n_above, thr_bits = jnp.where(live_rows, i32(-1), 0), zero_row, zero_row
    for p in range(KEY_BITS):
        ones = alive & planes_ref[p]
        n_ones = jnp.sum(lax.population_count(ones), axis=0, keepdims=True)
        take = (n_above + n_ones) >= top_k
        alive = jnp.where(take, ones, alive ^ ones)
        n_above = jnp.where(take, n_above, n_above + n_ones)
        thr_bits = thr_bits | jnp.where(take, i32(1 << (31 - p) if p else _INT_MIN), 0)
    thr = thr_bits ^ int_min
    n_equal = jnp.sum(lax.population_count(alive), axis=0, keepdims=True)
    need = top_k - n_above

    tie = (n_equal > need) & (thr > int_min)

    @pl.when(jnp.max(tie.astype(i32)) > 0)
    def _():
        def key_idx(r0):
            return r0 + lax.broadcasted_iota(i32, (tk, tq), 0)

        def count_where(pred):
            def body(j, acc):
                r0 = pl.multiple_of(j * tk, tk)
                hit = pred(keys_ref[pl.ds(r0, tk), :], r0).astype(i32)
                return acc + hit.reshape(tk // V7X_SUBLANES, V7X_SUBLANES, tq).sum(axis=0)
            acc = lax.fori_loop(0, n_tiles, body, jnp.zeros((V7X_SUBLANES, tq), i32))
            return jnp.sum(acc, axis=0, keepdims=True)

        def index_pass(p, last):
            cand = last + lax.shift_left(i32(1), seq_bits - 1 - p)
            cnt = count_where(lambda kt, r0: (kt == thr) & (key_idx(r0) < cand))
            return jnp.where(cnt < need, cand, last)

        last = lax.fori_loop(0, seq_bits, index_pass, zero_row)

        def drop(j, carry):
            r0 = pl.multiple_of(j * tk, tk)
            kt = keys_ref[pl.ds(r0, tk), :]
            surplus = tie & (kt == thr) & (key_idx(r0) > last)
            keys_ref[pl.ds(r0, tk), :] = jnp.where(surplus, int_min, kt)
            return carry

        lax.fori_loop(0, n_tiles, drop, 0)

    thr_eff = jnp.maximum(thr, int_min + 1)

    def make_mask(j, carry):
        madd_ref[rows(j), :] = jnp.where(keys_ref[rows(j), :] >= thr_eff, 0.0, _MASKED).astype(f32)
        return carry

    lax.fori_loop(0, n_tiles, make_mask, 0)

    heads = [slice(h * hd, (h + 1) * hd) for h in range(N_HEADS_B)]

    bounds = []
    for h, hs in enumerate(heads):
        qt = q_ref[:, hs].astype(f32).T
        far = relb_ref[far_bucket, h]
        bias_max = jnp.float32(0.0)
        for b in range(NUM_BUCKETS):
            bias_max = jnp.maximum(bias_max, (relb_ref[b, h] - far) * LOG2_E)
        longest_key = jnp.sqrt(jnp.max(ksq_ref[h:h + 1, :]))
        bounds.append(jnp.sqrt(jnp.sum(qt * qt, axis=0, keepdims=True)) * longest_key + bias_max)

    def fold(x):
        return x.reshape(tk // V7X_SUBLANES, V7X_SUBLANES, tq).sum(axis=0)

    def qk(j, h):
        return lax.dot_general(k_ref[rows(j), heads[h]], q_ref[:, heads[h]], _NT,
                               preferred_element_type=f32)

    def logits(j, h, near):
        s = qk(j, h) + madd_ref[rows(j), :]
        return s if near is None else s + bias_ref[h, near]

    acc_ref[...] = jnp.zeros(acc_ref.shape, f32)
    l_ref[...] = jnp.zeros(l_ref.shape, f32)

    def issue(j, buf):
        for h in range(N_HEADS_B):
            buf[h] = qk(j, h)

    def consume(j, buf, near):
        madd = madd_ref[rows(j), :]
        for h in range(N_HEADS_B):
            x = buf[h] + madd
            if near is not None:
                x = x + bias_ref[h, near]
            e = jnp.exp2(x - bounds[h])
            l_ref[h] += fold(e)
            acc_ref[h] += jnp.dot(vt_ref[heads[h], rows(j)], e.astype(bf16),
                                  preferred_element_type=f32)

    n_far = jnp.maximum(qi - 1, 0)
    issue(0, sa_ref)

    def far_pair(i, carry):
        j = 2 * i
        issue(j + 1, sb_ref)
        consume(j, sa_ref, None)
        issue(j + 2, sa_ref)
        consume(j + 1, sb_ref, None)
        return carry

    lax.fori_loop(0, n_far // 2, far_pair, 0)
    j0 = 2 * (n_far // 2)

    @pl.when(n_far % 2 == 1)
    def _():
        issue(j0 + 1, sb_ref)
        consume(j0, sa_ref, None)
        issue(j0 + 2, sa_ref)
        consume(j0 + 1, sb_ref, 1)
        consume(j0 + 2, sa_ref, 0)

    @pl.when((n_far % 2 == 0) & (qi >= 1))
    def _():
        issue(j0 + 1, sb_ref)
        consume(j0, sa_ref, 1)
        consume(j0 + 1, sb_ref, 0)

    @pl.when(qi == 0)
    def _():
        consume(0, sa_ref, 0)

    smallest = jnp.min(jnp.sum(l_ref[...], axis=1))

    @pl.when(jnp.logical_not(smallest > SUM_FLOOR))
    def _():
        acc_ref[...] = jnp.zeros(acc_ref.shape, f32)
        first_row = lax.broadcasted_iota(i32, (V7X_SUBLANES, tq), 0) == 0

        def attend_online(j, near):
            for h in range(N_HEADS_B):
                s = logits(j, h, near)
                m_old = m_ref[h]
                m_new = jnp.maximum(m_old, jnp.max(s, axis=0, keepdims=True))
                alpha = jnp.exp2(m_old - m_new)
                p = jnp.exp2(s - m_new)
                l_new = (alpha * jnp.sum(l_ref[h], axis=0, keepdims=True)
                         + jnp.sum(p, axis=0, keepdims=True))
                l_ref[h] = jnp.where(first_row, l_new, 0.0)
                pv = jnp.dot(vt_ref[heads[h], rows(j)], p.astype(bf16), preferred_element_type=f32)
                acc_ref[h] = alpha * acc_ref[h] + pv
                m_ref[h] = m_new

        m_ref[...] = jnp.full(m_ref.shape, _MASKED, f32)
        l_ref[...] = jnp.zeros(l_ref.shape, f32)

        def far_online(j, carry):
            attend_online(j, None)
            return carry

        lax.fori_loop(0, qi - 1, far_online, 0)

        @pl.when(qi >= 1)
        def _():
            attend_online(qi - 1, 1)

        attend_online(qi, 0)

    for h in range(N_HEADS_B):
        total = jnp.sum(l_ref[h], axis=0, keepdims=True)
        out_ref[:, heads[h]] = (acc_ref[h] / total).T.astype(out_ref.dtype)


def _dsa(rel_bias, q, iq, iwt, k, vt, ik, ksq, buckets, *, batch, tq, top_k):
    m, d_b = q.shape
    s = m // batch
    nq = s // tq
    hd = d_b // N_HEADS_B
    tile_rows = lambda bi, qi: (bi * nq + qi, 0)
    batch_rows = lambda bi, qi: (bi, 0)
    return pl.pallas_call(
        functools.partial(_dsa_body, tq=tq, top_k=top_k, seq_bits=int(np.log2(s)),
                          far_bucket=NUM_BUCKETS // 2 - 1),
        grid=(batch, nq),
        in_specs=[
            pl.BlockSpec(memory_space=pltpu.SMEM),
            pl.BlockSpec((tq, d_b), tile_rows),
            pl.BlockSpec((IDX_HEADS, tq, IDX_DIM), lambda bi, qi: (0, bi * nq + qi, 0)),
            pl.BlockSpec((IDX_HEADS, tq), lambda bi, qi: (0, bi * nq + qi)),
            pl.BlockSpec((s, d_b), batch_rows),
            pl.BlockSpec((d_b, s), lambda bi, qi: (0, bi)),
            pl.BlockSpec((s, IDX_DIM), batch_rows),
            pl.BlockSpec((V7X_SUBLANES, s), lambda bi, qi: (0, bi)),
            pl.BlockSpec(buckets.shape, lambda bi, qi: (0, 0, 0)),
        ],
        out_specs=pl.BlockSpec((tq, d_b), tile_rows),
        out_shape=jax.ShapeDtypeStruct((m, d_b), bf16),
        scratch_shapes=[
            pltpu.VMEM((s, tq), i32),
            pltpu.VMEM((KEY_BITS, s // KEY_BITS, tq), i32),
            pltpu.VMEM((s, tq), f32),
            pltpu.VMEM((N_HEADS_B, 2, tq, tq), f32),
            pltpu.VMEM((N_HEADS_B, tq, tq), f32),
            pltpu.VMEM((N_HEADS_B, tq, tq), f32),
            pltpu.VMEM((N_HEADS_B, 1, tq), f32),
            pltpu.VMEM((N_HEADS_B, V7X_SUBLANES, tq), f32),
            pltpu.VMEM((N_HEADS_B, hd, tq), f32),
        ],
        compiler_params=pltpu.CompilerParams(
            dimension_semantics=("arbitrary", "arbitrary"), vmem_limit_bytes=V7X_VMEM_LIMIT_BYTES),
        name="dsa",
    )(rel_bias, q, iq, iwt, k, vt, ik, ksq, buckets)


def _out_ffn_body(x_ref, a_ref, b_ref, woa_ref, wob_ref, gpm_ref, gpf_ref, wg_ref, wu_ref,
                  wd_ref, gpo_ref, o_ref, *, ff_chunk):
    mix = (jnp.dot(a_ref[...], woa_ref[...], preferred_element_type=f32)
           + jnp.dot(b_ref[...], wob_ref[...], preferred_element_type=f32))
    x1 = x_ref[...] + _rms(mix, gpm_ref[...])
    h = _rms(x1, gpf_ref[...]).astype(bf16)
    d_ff = wg_ref.shape[1]
    f = jnp.zeros(x1.shape, f32)
    for c in range(d_ff // ff_chunk):
        cs = slice(c * ff_chunk, (c + 1) * ff_chunk)
        gate = jnp.dot(h, wg_ref[:, cs], preferred_element_type=f32)
        up = jnp.dot(h, wu_ref[:, cs], preferred_element_type=f32)
        act = (gate * jax.nn.sigmoid(gate) * up).astype(bf16)
        f = f + jnp.dot(act, wd_ref[cs, :], preferred_element_type=f32)
    o_ref[...] = x1 + _rms(f, gpo_ref[...])


def _out_ffn(x2, out_a, out_b, w_oa, w_ob, g_pm, g_pf, w_gate, w_up, w_down, g_po, *, tm, ff_chunk):
    m, d = x2.shape
    d_a, d_b, d_ff = out_a.shape[1], out_b.shape[1], w_gate.shape[1]
    row = lambda i: (i, 0)
    const = lambda i: (0, 0)
    resident = functools.partial(pl.BlockSpec, index_map=const, pipeline_mode=pl.Buffered(1))
    return pl.pallas_call(
        functools.partial(_out_ffn_body, ff_chunk=ff_chunk),
        grid=(m // tm,),
        in_specs=[
            pl.BlockSpec((tm, d), row),
            pl.BlockSpec((tm, d_a), row),
            pl.BlockSpec((tm, d_b), row),
            resident((d_a, d)),
            resident((d_b, d)),
            resident((1, d)),
            resident((1, d)),
            resident((d, d_ff)),
            resident((d, d_ff)),
            resident((d_ff, d)),
            resident((1, d)),
        ],
        out_specs=pl.BlockSpec((tm, d), row),
        out_shape=jax.ShapeDtypeStruct((m, d), f32),
        compiler_params=pltpu.CompilerParams(
            dimension_semantics=("arbitrary",), vmem_limit_bytes=V7X_VMEM_LIMIT_BYTES),
        name="out_ffn",
    )(x2, out_a, out_b, w_oa, w_ob, g_pm, g_pf, w_gate, w_up, w_down, g_po)


def _t5_bucket(rel):
    nb = NUM_BUCKETS // 2
    ret = (rel > 0).astype(jnp.int32) * nb
    n = jnp.abs(rel)
    max_exact = nb // 2
    nf = jnp.maximum(n, 1).astype(jnp.float32)
    large = max_exact + (jnp.log(nf / max_exact) / np.float32(np.log(MAX_DISTANCE / max_exact))
                         * (nb - max_exact)).astype(jnp.int32)
    large = jnp.minimum(large, nb - 1)
    return ret + jnp.where(n < max_exact, n, large)


def _near_buckets(tq):
    assert tq + 1 >= MAX_DISTANCE
    rel0 = jnp.arange(tq, dtype=jnp.int32)[:, None] - jnp.arange(tq, dtype=jnp.int32)[None, :]
    return _t5_bucket(jnp.stack([rel0, rel0 - tq]))


def kernel(x, g_pre_mix, w_in, sgu_ln_g, sgu_ln_b, sgu_w, sgu_b, rel_bias, w_o, g_post_mix,
           g_pre_ffn, w_gate, w_up, w_down, g_post_ffn):
    b, s, d = x.shape
    depth = w_in.shape[0]
    d_a = d // 2
    d_b = d - d_a
    d_main = 2 * d_a + 3 * d_b + IDX_HEADS * IDX_DIM
    top_k = min(MAX_TOPK, s // 4)
    tq = KEY_BITS * V7X_SUBLANES
    tm = 512
    x2 = x.reshape(b * s, d)
    for l in range(depth):
        w_l = w_in[l]
        w_main = w_l[:, :d_main].astype(bf16)
        w_ik = jnp.pad(w_l[:, d_main:], ((0, 0), (0, V7X_LANES - (IDX_DIM + IDX_HEADS)))).astype(bf16)
        out_a, q, k, vt, iq, ik, iwt, ksq = _proj_sgu(
            x2, g_pre_mix[l][None], w_main, w_ik, sgu_ln_g[l][None], sgu_ln_b[l][None],
            sgu_w[l], jnp.transpose(sgu_b[l]), d_a=d_a, d_b=d_b, tm=tm)
        out_b = _dsa(rel_bias, q, iq, iwt, k, vt, ik, ksq, _near_buckets(tq), batch=b, tq=tq, top_k=top_k)
        w_ol = w_o[l].astype(bf16)
        x2 = _out_ffn(x2, out_a, out_b, w_ol[:d_a], w_ol[d_a:],
                      g_post_mix[l][None], g_pre_ffn[l][None], w_gate[l].astype(bf16),
                      w_up[l].astype(bf16), w_down[l].astype(bf16), g_post_ffn[l][None],
                      tm=tm, ff_chunk=256)
    return x2.reshape(b, s, d)
```

```python
import functools

import numpy as np
import jax
import jax.numpy as jnp
from jax import lax
from jax.experimental import pallas as pl
from jax.experimental.pallas import tpu as pltpu

CHUNK = 64
SGU_BLOCK = 128
SGU_GROUPS = 4
N_HEADS_B = 4
IDX_HEADS = 8
IDX_DIM = 64
MAX_TOPK = 256
NUM_BUCKETS = 32
MAX_DISTANCE = 128
EPS = 1e-6

V7X_LANES = 128
V7X_SUBLANES = 8
V7X_VMEM_LIMIT_BYTES = 56 * 1024 * 1024

KEY_BITS = 32
_INT_MIN = -(2 ** 31)
_MASKED = -1e30
LOG2_E = float(np.log2(np.e))
SUM_FLOOR = 1e-30

_NT = (((1,), (1,)), ((), ()))

f32 = jnp.float32
bf16 = jnp.bfloat16
i32 = jnp.int32


def _rms(x, g):
    return x * lax.rsqrt(jnp.mean(x * x, axis=-1, keepdims=True) + EPS) * g


def _proj_sgu_body(x_ref, g_ref, w_ref, wik_ref, lng_ref, lnb_ref, sw_ref, sbt_ref,
                   outa_ref, q_ref, k_ref, vt_ref, iq_ref, ik_ref, iwt_ref, ksq_ref, *, d_a, d_b):
    tm = x_ref.shape[0]
    h = _rms(x_ref[...], g_ref[...]).astype(bf16)

    def proj(c0, width):
        return jnp.dot(h, w_ref[:, c0:c0 + width], preferred_element_type=f32)

    u = proj(0, d_a)
    a_v = proj(d_a, d_a)
    c0 = 2 * d_a
    q_ref[...] = (proj(c0, d_b) * np.float32((d_b // N_HEADS_B) ** -0.5 * LOG2_E)).astype(bf16)
    k = proj(c0 + d_b, d_b).astype(bf16)
    k_ref[...] = k
    kt = k.astype(f32).T
    hd = d_b // N_HEADS_B
    ksq = [jnp.sum(kt[a * hd:(a + 1) * hd] ** 2, axis=0, keepdims=True) for a in range(N_HEADS_B)]
    ksq_ref[...] = jnp.concatenate(ksq + [jnp.zeros((V7X_SUBLANES - N_HEADS_B, tm), f32)], axis=0)
    vt_ref[...] = proj(c0 + 2 * d_b, d_b).astype(bf16).T
    iq = (proj(c0 + 3 * d_b, IDX_HEADS * IDX_DIM) * np.float32(IDX_DIM ** -0.5)).astype(bf16)
    for hh in range(IDX_HEADS):
        iq_ref[hh] = iq[:, hh * IDX_DIM:(hh + 1) * IDX_DIM]
    ikw = jnp.dot(h, wik_ref[...], preferred_element_type=f32)
    ik_ref[...] = ikw[:, :IDX_DIM].astype(bf16)
    iwt_ref[...] = ikw.T[IDX_DIM:IDX_DIM + IDX_HEADS, :] * np.float32(IDX_HEADS ** -0.5)

    mu = jnp.mean(a_v, axis=-1, keepdims=True)
    cen = a_v - mu
    var = jnp.mean(cen * cen, axis=-1, keepdims=True)
    vn = (cen * lax.rsqrt(var + EPS) * lng_ref[...] + lnb_ref[...]).astype(bf16)

    row_chunk = lax.broadcasted_iota(i32, (SGU_BLOCK, SGU_BLOCK), 0) // CHUNK
    col_chunk = lax.broadcasted_iota(i32, (SGU_BLOCK, SGU_BLOCK), 1) // CHUNK
    causal = col_chunk <= row_chunk
    gdim = d_a // SGU_GROUPS
    for g in range(SGU_GROUPS):
        w_g = jnp.where(causal, sw_ref[g], 0.0).astype(bf16)
        b_g = sbt_ref[:, g:g + 1]
        for c in range(tm // SGU_BLOCK):
            rows = slice(c * SGU_BLOCK, (c + 1) * SGU_BLOCK)
            cols = slice(g * gdim, (g + 1) * gdim)
            mixed = jnp.dot(w_g, vn[rows, cols], preferred_element_type=f32) + b_g
            outa_ref[rows, cols] = (u[rows, cols] * mixed).astype(bf16)


def _proj_sgu(x2, g_pre, w_main, w_ik, ln_g, ln_b, sgu_w, sgu_bt, *, d_a, d_b, tm):
    m, d = x2.shape
    const = lambda i: (0, 0)
    row = lambda i: (i, 0)
    col = lambda i: (0, i)
    return pl.pallas_call(
        functools.partial(_proj_sgu_body, d_a=d_a, d_b=d_b),
        grid=(m // tm,),
        in_specs=[
            pl.BlockSpec((tm, d), row),
            pl.BlockSpec((1, d), const),
            pl.BlockSpec(w_main.shape, const),
            pl.BlockSpec(w_ik.shape, const),
            pl.BlockSpec((1, d_a), const),
            pl.BlockSpec((1, d_a), const),
            pl.BlockSpec(sgu_w.shape, lambda i: (0, 0, 0)),
            pl.BlockSpec(sgu_bt.shape, const),
        ],
        out_specs=[
            pl.BlockSpec((tm, d_a), row),
            pl.BlockSpec((tm, d_b), row),
            pl.BlockSpec((tm, d_b), row),
            pl.BlockSpec((d_b, tm), col),
            pl.BlockSpec((IDX_HEADS, tm, IDX_DIM), lambda i: (0, i, 0)),
            pl.BlockSpec((tm, IDX_DIM), row),
            pl.BlockSpec((IDX_HEADS, tm), col),
            pl.BlockSpec((V7X_SUBLANES, tm), col),
        ],
        out_shape=[
            jax.ShapeDtypeStruct((m, d_a), bf16),
            jax.ShapeDtypeStruct((m, d_b), bf16),
            jax.ShapeDtypeStruct((m, d_b), bf16),
            jax.ShapeDtypeStruct((d_b, m), bf16),
            jax.ShapeDtypeStruct((IDX_HEADS, m, IDX_DIM), bf16),
            jax.ShapeDtypeStruct((m, IDX_DIM), bf16),
            jax.ShapeDtypeStruct((IDX_HEADS, m), f32),
            jax.ShapeDtypeStruct((V7X_SUBLANES, m), f32),
        ],
        compiler_params=pltpu.CompilerParams(
            dimension_semantics=("arbitrary",), vmem_limit_bytes=V7X_VMEM_LIMIT_BYTES),
        name="proj_sgu",
    )(x2, g_pre, w_main, w_ik, ln_g, ln_b, sgu_w, sgu_bt)


def _bit_planes(words):
    a = list(words)
    j, m = 16, 0x0000FFFF
    while j:
        k = 0
        while k < KEY_BITS:
            t = (a[k] ^ lax.shift_right_logical(a[k + j], i32(j))) & i32(m)
            a[k] = a[k] ^ t
            a[k + j] = a[k + j] ^ lax.shift_left(t, i32(j))
            k = (k + j + 1) & ~j
        j >>= 1
        m = (m ^ (m << j)) & 0xFFFFFFFF
    return a


def _dsa_body(relb_ref, q_ref, iq_ref, iwt_ref, k_ref, vt_ref, ik_ref, ksq_ref, bkt_ref, out_ref,
              keys_ref, planes_ref, sel_ref, madd_ref, bias_ref, sa_ref, sb_ref, m_ref, l_ref, acc_ref,
              *, tq, top_k, seq_bits, far_bucket):
    tk = tq
    group = KEY_BITS * V7X_SUBLANES
    assert tk == group
    hd = q_ref.shape[1] // N_HEADS_B
    qi = pl.program_id(1)
    n_tiles = qi + 1
    int_min = i32(_INT_MIN)

    def rows(j):
        return pl.ds(pl.multiple_of(j * tk, tk), tk)

    @pl.when((pl.program_id(0) == 0) & (qi == 0))
    def _():
        planes_ref[...] = jnp.zeros(planes_ref.shape, i32)
        for d in range(2):
            bucket = bkt_ref[d]
            for h in range(N_HEADS_B):
                far = relb_ref[far_bucket, h]
                tab = jnp.zeros((tk, tq), f32)
                for b in range(NUM_BUCKETS):
                    tab = jnp.where(bucket == b, (relb_ref[b, h] - far) * LOG2_E, tab)
                bias_ref[h, d] = tab

    def score_tile(j, diagonal):
        ik_j = ik_ref[rows(j), :]
        sc = jnp.zeros((tk, tq), f32)
        for h in range(IDX_HEADS):
            d = lax.dot_general(ik_j, iq_ref[h], _NT, preferred_element_type=f32)
            sc = sc + iwt_ref[h:h + 1, :] * jnp.maximum(d, 0.0)
        bits = lax.bitcast_convert_type(sc, i32)
        key = bits ^ ((bits >> 31) & i32(0x7FFFFFFF))
        if diagonal:
            key_chunk = lax.broadcasted_iota(i32, (tk, tq), 0) // CHUNK
            qry_chunk = lax.broadcasted_iota(i32, (tk, tq), 1) // CHUNK
            key = jnp.where(key_chunk <= qry_chunk, key, int_min)
        keys_ref[rows(j), :] = key
        return key

    def plane_tile(j, key):
        prow = pl.ds(pl.multiple_of(j * V7X_SUBLANES, V7X_SUBLANES), V7X_SUBLANES)
        for lt in range(tq // V7X_LANES):
            lanes = slice(lt * V7X_LANES, (lt + 1) * V7X_LANES)
            planes = _bit_planes([key[V7X_SUBLANES * i:V7X_SUBLANES * (i + 1), lanes]
                                  for i in range(KEY_BITS)])
            planes_ref[0, prow, lanes] = ~planes[0]
            for p in range(1, KEY_BITS):
                planes_ref[p, prow, lanes] = planes[p]

    def far_scores(j, prev):
        key = score_tile(j, False)
        plane_tile(j - 1, prev)
        return key

    prev = lax.fori_loop(1, qi, far_scores, score_tile(0, False))
    last = score_tile(qi, True)
    plane_tile(jnp.maximum(qi - 1, 0), prev)
    plane_tile(qi, last)

    n_prow = planes_ref.shape[1]

    zero_row = jnp.zeros((1, tq), i32)

    def radix_select(n_rows):
        live = lax.broadcasted_iota(i32, (n_rows, tq), 0) < n_tiles * V7X_SUBLANES
        alive = jnp.where(live, i32(-1), 0)
        n_above, thr_bits = zero_row, zero_row
        for p in range(KEY_BITS):
            ones = alive & planes_ref[p, :n_rows, :]
            n_ones = jnp.sum(lax.population_count(ones), axis=0, keepdims=True)
            take = (n_above + n_ones) >= top_k
            alive = jnp.where(take, ones, alive ^ ones)
            n_above = jnp.where(take, n_above, n_above + n_ones)
            thr_bits = thr_bits | jnp.where(take, i32(1 << (31 - p) if p else _INT_MIN), 0)
        n_equal = jnp.sum(lax.population_count(alive), axis=0, keepdims=True)
        sel_ref[...] = jnp.concatenate(
            [thr_bits, n_above, n_equal, jnp.zeros((V7X_SUBLANES - 3, tq), i32)], axis=0)

    n_classes = 4
    per_class = n_prow // n_classes
    for c in range(n_classes):
        lo, hi = c * per_class, (c + 1) * per_class

        @pl.when((n_tiles * V7X_SUBLANES > lo) & (n_tiles * V7X_SUBLANES <= hi))
        def _(hi=hi):
            radix_select(hi)

    sel = sel_ref[...]
    thr = sel[0:1] ^ int_min
    n_above = sel[1:2]
    n_equal = sel[2:3]
    need = top_k - n_above

    tie = (n_equal > need) & (thr > int_min)

    @pl.when(jnp.max(tie.astype(i32)) > 0)
    def _():
        def key_idx(r0):
            return r0 + lax.broadcasted_iota(i32, (tk, tq), 0)

        def count_where(pred):
            def body(j, acc):
                r0 = pl.multiple_of(j * tk, tk)
                hit = pred(keys_ref[pl.ds(r0, tk), :], r0).astype(i32)
                return acc + hit.reshape(tk // V7X_SUBLANES, V7X_SUBLANES, tq).sum(axis=0)
            acc = lax.fori_loop(0, n_tiles, body, jnp.zeros((V7X_SUBLANES, tq), i32))
            return jnp.sum(acc, axis=0, keepdims=True)

        def index_pass(p, last):
            cand = last + lax.shift_left(i32(1), seq_bits - 1 - p)
            cnt = count_where(lambda kt, r0: (kt == thr) & (key_idx(r0) < cand))
            return jnp.where(cnt < need, cand, last)

        last = lax.fori_loop(0, seq_bits, index_pass, zero_row)

        def drop(j, carry):
            r0 = pl.multiple_of(j * tk, tk)
            kt = keys_ref[pl.ds(r0, tk), :]
            surplus = tie & (kt == thr) & (key_idx(r0) > last)
            keys_ref[pl.ds(r0, tk), :] = jnp.where(surplus, int_min, kt)
            return carry

        lax.fori_loop(0, n_tiles, drop, 0)

    thr_eff = jnp.maximum(thr, int_min + 1)

    def make_mask(j, carry):
        madd_ref[rows(j), :] = jnp.where(keys_ref[rows(j), :] >= thr_eff, 0.0, _MASKED).astype(f32)
        return carry

    lax.fori_loop(0, n_tiles, make_mask, 0)

    heads = [slice(h * hd, (h + 1) * hd) for h in range(N_HEADS_B)]

    bounds = []
    for h, hs in enumerate(heads):
        qt = q_ref[:, hs].astype(f32).T
        far = relb_ref[far_bucket, h]
        bias_max = jnp.float32(0.0)
        for b in range(NUM_BUCKETS):
            bias_max = jnp.maximum(bias_max, (relb_ref[b, h] - far) * LOG2_E)
        longest_key = jnp.sqrt(jnp.max(ksq_ref[h:h + 1, :]))
        bounds.append(jnp.sqrt(jnp.sum(qt * qt, axis=0, keepdims=True)) * longest_key + bias_max)

    def fold(x):
        return x.reshape(tk // V7X_SUBLANES, V7X_SUBLANES, tq).sum(axis=0)

    def qk(j, h):
        return lax.dot_general(k_ref[rows(j), heads[h]], q_ref[:, heads[h]], _NT,
                               preferred_element_type=f32)

    def logits(j, h, near):
        s = qk(j, h) + madd_ref[rows(j), :]
        return s if near is None else s + bias_ref[h, near]

    acc_ref[...] = jnp.zeros(acc_ref.shape, f32)
    l_ref[...] = jnp.zeros(l_ref.shape, f32)

    def issue(j, buf):
        for h in range(N_HEADS_B):
            buf[h] = qk(j, h)

    def consume(j, buf, near):
        madd = madd_ref[rows(j), :]
        for h in range(N_HEADS_B):
            x = buf[h] + madd
            if near is not None:
                x = x + bias_ref[h, near]
            e = jnp.exp2(x - bounds[h])
            l_ref[h] += fold(e)
            acc_ref[h] += jnp.dot(vt_ref[heads[h], rows(j)], e.astype(bf16),
                                  preferred_element_type=f32)

    n_far = jnp.maximum(qi - 1, 0)
    issue(0, sa_ref)

    def far_pair(i, carry):
        j = 2 * i
        issue(j + 1, sb_ref)
        consume(j, sa_ref, None)
        issue(j + 2, sa_ref)
        consume(j + 1, sb_ref, None)
        return carry

    lax.fori_loop(0, n_far // 2, far_pair, 0)
    j0 = 2 * (n_far // 2)

    @pl.when(n_far % 2 == 1)
    def _():
        issue(j0 + 1, sb_ref)
        consume(j0, sa_ref, None)
        issue(j0 + 2, sa_ref)
        consume(j0 + 1, sb_ref, 1)
        consume(j0 + 2, sa_ref, 0)

    @pl.when((n_far % 2 == 0) & (qi >= 1))
    def _():
        issue(j0 + 1, sb_ref)
        consume(j0, sa_ref, 1)
        consume(j0 + 1, sb_ref, 0)

    @pl.when(qi == 0)
    def _():
        consume(0, sa_ref, 0)

    smallest = jnp.min(jnp.sum(l_ref[...], axis=1))

    @pl.when(jnp.logical_not(smallest > SUM_FLOOR))
    def _():
        acc_ref[...] = jnp.zeros(acc_ref.shape, f32)
        first_row = lax.broadcasted_iota(i32, (V7X_SUBLANES, tq), 0) == 0

        def attend_online(j, near):
            for h in range(N_HEADS_B):
                s = logits(j, h, near)
                m_old = m_ref[h]
                m_new = jnp.maximum(m_old, jnp.max(s, axis=0, keepdims=True))
                alpha = jnp.exp2(m_old - m_new)
                p = jnp.exp2(s - m_new)
                l_new = (alpha * jnp.sum(l_ref[h], axis=0, keepdims=True)
                         + jnp.sum(p, axis=0, keepdims=True))
                l_ref[h] = jnp.where(first_row, l_new, 0.0)
                pv = jnp.dot(vt_ref[heads[h], rows(j)], p.astype(bf16), preferred_element_type=f32)
                acc_ref[h] = alpha * acc_ref[h] + pv
                m_ref[h] = m_new

        m_ref[...] = jnp.full(m_ref.shape, _MASKED, f32)
        l_ref[...] = jnp.zeros(l_ref.shape, f32)

        def far_online(j, carry):
            attend_online(j, None)
            return carry

        lax.fori_loop(0, qi - 1, far_online, 0)

        @pl.when(qi >= 1)
        def _():
            attend_online(qi - 1, 1)

        attend_online(qi, 0)

    for h in range(N_HEADS_B):
        total = jnp.sum(l_ref[h], axis=0, keepdims=True)
        out_ref[:, heads[h]] = (acc_ref[h] / total).T.astype(out_ref.dtype)


def _dsa(rel_bias, q, iq, iwt, k, vt, ik, ksq, buckets, *, batch, tq, top_k):
    m, d_b = q.shape
    s = m // batch
    nq = s // tq
    hd = d_b // N_HEADS_B
    tile_rows = lambda bi, qi: (bi * nq + qi, 0)
    batch_rows = lambda bi, qi: (bi, 0)
    return pl.pallas_call(
        functools.partial(_dsa_body, tq=tq, top_k=top_k, seq_bits=int(np.log2(s)),
                          far_bucket=NUM_BUCKETS // 2 - 1),
        grid=(batch, nq),
        in_specs=[
            pl.BlockSpec(memory_space=pltpu.SMEM),
            pl.BlockSpec((tq, d_b), tile_rows),
            pl.BlockSpec((IDX_HEADS, tq, IDX_DIM), lambda bi, qi: (0, bi * nq + qi, 0)),
            pl.BlockSpec((IDX_HEADS, tq), lambda bi, qi: (0, bi * nq + qi)),
            pl.BlockSpec((s, d_b), batch_rows),
            pl.BlockSpec((d_b, s), lambda bi, qi: (0, bi)),
            pl.BlockSpec((s, IDX_DIM), batch_rows),
            pl.BlockSpec((V7X_SUBLANES, s), lambda bi, qi: (0, bi)),
            pl.BlockSpec(buckets.shape, lambda bi, qi: (0, 0, 0)),
        ],
        out_specs=pl.BlockSpec((tq, d_b), tile_rows),
        out_shape=jax.ShapeDtypeStruct((m, d_b), bf16),
        scratch_shapes=[
            pltpu.VMEM((s, tq), i32),
            pltpu.VMEM((KEY_BITS, s // KEY_BITS, tq), i32),
            pltpu.VMEM((V7X_SUBLANES, tq), i32),
            pltpu.VMEM((s, tq), f32),
            pltpu.VMEM((N_HEADS_B, 2, tq, tq), f32),
            pltpu.VMEM((N_HEADS_B, tq, tq), f32),
            pltpu.VMEM((N_HEADS_B, tq, tq), f32),
            pltpu.VMEM((N_HEADS_B, 1, tq), f32),
            pltpu.VMEM((N_HEADS_B, V7X_SUBLANES, tq), f32),
            pltpu.VMEM((N_HEADS_B, hd, tq), f32),
        ],
        compiler_params=pltpu.CompilerParams(
            dimension_semantics=("arbitrary", "arbitrary"), vmem_limit_bytes=V7X_VMEM_LIMIT_BYTES),
        name="dsa",
    )(rel_bias, q, iq, iwt, k, vt, ik, ksq, buckets)


def _out_ffn_body(x_ref, a_ref, b_ref, woa_ref, wob_ref, gpm_ref, gpf_ref, wg_ref, wu_ref,
                  wd_ref, gpo_ref, o_ref, *, ff_chunk):
    mix = (jnp.dot(a_ref[...], woa_ref[...], preferred_element_type=f32)
           + jnp.dot(b_ref[...], wob_ref[...], preferred_element_type=f32))
    x1 = x_ref[...] + _rms(mix, gpm_ref[...])
    h = _rms(x1, gpf_ref[...]).astype(bf16)
    d_ff = wg_ref.shape[1]
    f = jnp.zeros(x1.shape, f32)
    for c in range(d_ff // ff_chunk):
        cs = slice(c * ff_chunk, (c + 1) * ff_chunk)
        gate = jnp.dot(h, wg_ref[:, cs], preferred_element_type=f32)
        up = jnp.dot(h, wu_ref[:, cs], preferred_element_type=f32)
        act = (gate * jax.nn.sigmoid(gate) * up).astype(bf16)
        f = f + jnp.dot(act, wd_ref[cs, :], preferred_element_type=f32)
    o_ref[...] = x1 + _rms(f, gpo_ref[...])


def _out_ffn(x2, out_a, out_b, w_oa, w_ob, g_pm, g_pf, w_gate, w_up, w_down, g_po, *, tm, ff_chunk):
    m, d = x2.shape
    d_a, d_b, d_ff = out_a.shape[1], out_b.shape[1], w_gate.shape[1]
    row = lambda i: (i, 0)
    const = lambda i: (0, 0)
    resident = functools.partial(pl.BlockSpec, index_map=const, pipeline_mode=pl.Buffered(1))
    return pl.pallas_call(
        functools.partial(_out_ffn_body, ff_chunk=ff_chunk),
        grid=(m // tm,),
        in_specs=[
            pl.BlockSpec((tm, d), row),
            pl.BlockSpec((tm, d_a), row),
            pl.BlockSpec((tm, d_b), row),
            resident((d_a, d)),
            resident((d_b, d)),
            resident((1, d)),
            resident((1, d)),
            resident((d, d_ff)),
            resident((d, d_ff)),
            resident((d_ff, d)),
            resident((1, d)),
        ],
        out_specs=pl.BlockSpec((tm, d), row),
        out_shape=jax.ShapeDtypeStruct((m, d), f32),
        compiler_params=pltpu.CompilerParams(
            dimension_semantics=("arbitrary",), vmem_limit_bytes=V7X_VMEM_LIMIT_BYTES),
        name="out_ffn",
    )(x2, out_a, out_b, w_oa, w_ob, g_pm, g_pf, w_gate, w_up, w_down, g_po)


def _t5_bucket(rel):
    nb = NUM_BUCKETS // 2
    ret = (rel > 0).astype(jnp.int32) * nb
    n = jnp.abs(rel)
    max_exact = nb // 2
    nf = jnp.maximum(n, 1).astype(jnp.float32)
    large = max_exact + (jnp.log(nf / max_exact) / np.float32(np.log(MAX_DISTANCE / max_exact))
                         * (nb - max_exact)).astype(jnp.int32)
    large = jnp.minimum(large, nb - 1)
    return ret + jnp.where(n < max_exact, n, large)


def _near_buckets(tq):
    assert tq + 1 >= MAX_DISTANCE
    rel0 = jnp.arange(tq, dtype=jnp.int32)[:, None] - jnp.arange(tq, dtype=jnp.int32)[None, :]
    return _t5_bucket(jnp.stack([rel0, rel0 - tq]))


def kernel(x, g_pre_mix, w_in, sgu_ln_g, sgu_ln_b, sgu_w, sgu_b, rel_bias, w_o, g_post_mix,
           g_pre_ffn, w_gate, w_up, w_down, g_post_ffn):
    b, s, d = x.shape
    depth = w_in.shape[0]
    d_a = d // 2
    d_b = d - d_a
    d_main = 2 * d_a + 3 * d_b + IDX_HEADS * IDX_DIM
    top_k = min(MAX_TOPK, s // 4)
    tq = KEY_BITS * V7X_SUBLANES
    tm = 512
    x2 = x.reshape(b * s, d)
    for l in range(depth):
        w_l = w_in[l]
        w_main = w_l[:, :d_main].astype(bf16)
        w_ik = jnp.pad(w_l[:, d_main:], ((0, 0), (0, V7X_LANES - (IDX_DIM + IDX_HEADS)))).astype(bf16)
        out_a, q, k, vt, iq, ik, iwt, ksq = _proj_sgu(
            x2, g_pre_mix[l][None], w_main, w_ik, sgu_ln_g[l][None], sgu_ln_b[l][None],
            sgu_w[l], jnp.transpose(sgu_b[l]), d_a=d_a, d_b=d_b, tm=tm)
        out_b = _dsa(rel_bias, q, iq, iwt, k, vt, ik, ksq, _near_buckets(tq), batch=b, tq=tq, top_k=top_k)
        w_ol = w_o[l].astype(bf16)
        x2 = _out_ffn(x2, out_a, out_b, w_ol[:d_a], w_ol[d_a:],
                      g_post_mix[l][None], g_pre_ffn[l][None], w_gate[l].astype(bf16),
                      w_up[l].astype(bf16), w_down[l].astype(bf16), g_post_ffn[l][None],
                      tm=tm, ff_chunk=256)
    return x2.reshape(b, s, d)
```

```python
import functools

import numpy as np
import jax
import jax.numpy as jnp
from jax import lax
from jax.experimental import pallas as pl
from jax.experimental.pallas import tpu as pltpu

CHUNK = 64
SGU_BLOCK = 128
SGU_GROUPS = 4
N_HEADS_B = 4
IDX_HEADS = 8
IDX_DIM = 64
MAX_TOPK = 256
NUM_BUCKETS = 32
MAX_DISTANCE = 128
EPS = 1e-6

V7X_LANES = 128
V7X_SUBLANES = 8
V7X_VMEM_LIMIT_BYTES = 56 * 1024 * 1024

KEY_BITS = 32
GAP_BITS = 30
_INT_MIN = -(2 ** 31)
_MASKED = -(2.0 ** 100)
LOG2_E = float(np.log2(np.e))
SUM_FLOOR = 1e-30

_NT = (((1,), (1,)), ((), ()))

f32 = jnp.float32
bf16 = jnp.bfloat16
i32 = jnp.int32


def _rms(x, g):
    return x * lax.rsqrt(jnp.mean(x * x, axis=-1, keepdims=True) + EPS) * g


def _proj_sgu_body(x_ref, g_ref, w_ref, wik_ref, lng_ref, lnb_ref, sw_ref, sbt_ref,
                   outa_ref, q_ref, k_ref, vt_ref, iq_ref, ik_ref, iwt_ref, ksq_ref, *, d_a, d_b):
    tm = x_ref.shape[0]
    h = _rms(x_ref[...], g_ref[...]).astype(bf16)

    def proj(c0, width):
        return jnp.dot(h, w_ref[:, c0:c0 + width], preferred_element_type=f32)

    u = proj(0, d_a)
    a_v = proj(d_a, d_a)
    c0 = 2 * d_a
    q_ref[...] = (proj(c0, d_b) * np.float32((d_b // N_HEADS_B) ** -0.5 * LOG2_E)).astype(bf16)
    k = proj(c0 + d_b, d_b).astype(bf16)
    k_ref[...] = k
    kt = k.astype(f32).T
    hd = d_b // N_HEADS_B
    ksq = [jnp.sum(kt[a * hd:(a + 1) * hd] ** 2, axis=0, keepdims=True) for a in range(N_HEADS_B)]
    ksq_ref[...] = jnp.concatenate(ksq + [jnp.zeros((V7X_SUBLANES - N_HEADS_B, tm), f32)], axis=0)
    vt_ref[...] = proj(c0 + 2 * d_b, d_b).astype(bf16).T
    iq = (proj(c0 + 3 * d_b, IDX_HEADS * IDX_DIM) * np.float32(IDX_DIM ** -0.5)).astype(bf16)
    for hh in range(IDX_HEADS):
        iq_ref[hh] = iq[:, hh * IDX_DIM:(hh + 1) * IDX_DIM]
    ikw = jnp.dot(h, wik_ref[...], preferred_element_type=f32)
    ik_ref[...] = ikw[:, :IDX_DIM].astype(bf16)
    iwt_ref[...] = ikw.T[IDX_DIM:IDX_DIM + IDX_HEADS, :] * np.float32(IDX_HEADS ** -0.5)

    mu = jnp.mean(a_v, axis=-1, keepdims=True)
    cen = a_v - mu
    var = jnp.mean(cen * cen, axis=-1, keepdims=True)
    vn = (cen * lax.rsqrt(var + EPS) * lng_ref[...] + lnb_ref[...]).astype(bf16)

    row_chunk = lax.broadcasted_iota(i32, (SGU_BLOCK, SGU_BLOCK), 0) // CHUNK
    col_chunk = lax.broadcasted_iota(i32, (SGU_BLOCK, SGU_BLOCK), 1) // CHUNK
    causal = col_chunk <= row_chunk
    gdim = d_a // SGU_GROUPS
    for g in range(SGU_GROUPS):
        w_g = jnp.where(causal, sw_ref[g], 0.0).astype(bf16)
        b_g = sbt_ref[:, g:g + 1]
        for c in range(tm // SGU_BLOCK):
            rows = slice(c * SGU_BLOCK, (c + 1) * SGU_BLOCK)
            cols = slice(g * gdim, (g + 1) * gdim)
            mixed = jnp.dot(w_g, vn[rows, cols], preferred_element_type=f32) + b_g
            outa_ref[rows, cols] = (u[rows, cols] * mixed).astype(bf16)


def _proj_sgu(x2, g_pre, w_main, w_ik, ln_g, ln_b, sgu_w, sgu_bt, *, d_a, d_b, tm):
    m, d = x2.shape
    const = lambda i: (0, 0)
    row = lambda i: (i, 0)
    col = lambda i: (0, i)
    return pl.pallas_call(
        functools.partial(_proj_sgu_body, d_a=d_a, d_b=d_b),
        grid=(m // tm,),
        in_specs=[
            pl.BlockSpec((tm, d), row),
            pl.BlockSpec((1, d), const),
            pl.BlockSpec(w_main.shape, const),
            pl.BlockSpec(w_ik.shape, const),
            pl.BlockSpec((1, d_a), const),
            pl.BlockSpec((1, d_a), const),
            pl.BlockSpec(sgu_w.shape, lambda i: (0, 0, 0)),
            pl.BlockSpec(sgu_bt.shape, const),
        ],
        out_specs=[
            pl.BlockSpec((tm, d_a), row),
            pl.BlockSpec((tm, d_b), row),
            pl.BlockSpec((tm, d_b), row),
            pl.BlockSpec((d_b, tm), col),
            pl.BlockSpec((IDX_HEADS, tm, IDX_DIM), lambda i: (0, i, 0)),
            pl.BlockSpec((tm, IDX_DIM), row),
            pl.BlockSpec((IDX_HEADS, tm), col),
            pl.BlockSpec((V7X_SUBLANES, tm), col),
        ],
        out_shape=[
            jax.ShapeDtypeStruct((m, d_a), bf16),
            jax.ShapeDtypeStruct((m, d_b), bf16),
            jax.ShapeDtypeStruct((m, d_b), bf16),
            jax.ShapeDtypeStruct((d_b, m), bf16),
            jax.ShapeDtypeStruct((IDX_HEADS, m, IDX_DIM), bf16),
            jax.ShapeDtypeStruct((m, IDX_DIM), bf16),
            jax.ShapeDtypeStruct((IDX_HEADS, m), f32),
            jax.ShapeDtypeStruct((V7X_SUBLANES, m), f32),
        ],
        compiler_params=pltpu.CompilerParams(
            dimension_semantics=("arbitrary",), vmem_limit_bytes=V7X_VMEM_LIMIT_BYTES),
        name="proj_sgu",
    )(x2, g_pre, w_main, w_ik, ln_g, ln_b, sgu_w, sgu_bt)


def _bit_planes(words):
    a = list(words)
    j, m = 16, 0x0000FFFF
    while j:
        k = 0
        while k < KEY_BITS:
            t = (a[k] ^ lax.shift_right_logical(a[k + j], i32(j))) & i32(m)
            a[k] = a[k] ^ t
            a[k + j] = a[k + j] ^ lax.shift_left(t, i32(j))
            k = (k + j + 1) & ~j
        j >>= 1
        m = (m ^ (m << j)) & 0xFFFFFFFF
    return a


def _dsa_body(bmax_ref, q_ref, iq_ref, iwt_ref, k_ref, vt_ref, ik_ref, ksq_ref, bias_ref, out_ref,
              sc_ref, planes_ref, madd_ref, sa_ref, sb_ref, m_ref, l_ref, acc_ref,
              *, tq, top_k, seq_bits):
    tk = tq
    group = KEY_BITS * V7X_SUBLANES
    assert tk == group
    hd = q_ref.shape[1] // N_HEADS_B
    qi = pl.program_id(1)
    n_tiles = qi + 1
    int_min = i32(_INT_MIN)

    def rows(j):
        return pl.ds(pl.multiple_of(j * tk, tk), tk)

    @pl.when((pl.program_id(0) == 0) & (qi == 0))
    def _():
        planes_ref[...] = jnp.zeros(planes_ref.shape, i32)

    def to_key(x):
        bits = lax.bitcast_convert_type(x, i32)
        return bits ^ ((bits >> 31) & i32(0x7FFFFFFF))

    def from_key(key):
        return lax.bitcast_convert_type(key ^ ((key >> 31) & i32(0x7FFFFFFF)), f32)

    def score_tile(j, diagonal):
        ik_j = ik_ref[rows(j), :]
        sc = jnp.zeros((tk, tq), f32)
        for h in range(IDX_HEADS):
            d = lax.dot_general(ik_j, iq_ref[h], _NT, preferred_element_type=f32)
            sc = sc + iwt_ref[h:h + 1, :] * jnp.maximum(d, 0.0)
        key = to_key(sc)
        if diagonal:
            key_chunk = lax.broadcasted_iota(i32, (tk, tq), 0) // CHUNK
            qry_chunk = lax.broadcasted_iota(i32, (tk, tq), 1) // CHUNK
            visible = key_chunk <= qry_chunk
            sc = jnp.where(visible, sc, -jnp.inf)
            key = jnp.where(visible, key, int_min)
        sc_ref[rows(j), :] = sc
        return key

    def plane_tile(j, key):
        prow = pl.ds(pl.multiple_of(j * V7X_SUBLANES, V7X_SUBLANES), V7X_SUBLANES)
        for lt in range(tq // V7X_LANES):
            lanes = slice(lt * V7X_LANES, (lt + 1) * V7X_LANES)
            planes = _bit_planes([key[V7X_SUBLANES * i:V7X_SUBLANES * (i + 1), lanes]
                                  for i in range(KEY_BITS)])
            planes_ref[0, prow, lanes] = ~planes[0]
            for p in range(1, KEY_BITS):
                planes_ref[p, prow, lanes] = planes[p]

    def far_scores(j, prev):
        key = score_tile(j, False)
        plane_tile(j - 1, prev)
        return key

    prev = lax.fori_loop(1, qi, far_scores, score_tile(0, False))
    last = score_tile(qi, True)
    plane_tile(jnp.maximum(qi - 1, 0), prev)
    plane_tile(qi, last)

    n_prow = planes_ref.shape[1]
    zero_row = jnp.zeros((1, tq), i32)
    live = lax.broadcasted_iota(i32, (n_prow, tq), 0) < n_tiles * V7X_SUBLANES
    alive, n_above, thr_bits = jnp.where(live, i32(-1), 0), zero_row, zero_row
    for p in range(KEY_BITS):
        ones = alive & planes_ref[p]
        n_ones = jnp.sum(lax.population_count(ones), axis=0, keepdims=True)
        take = (n_above + n_ones) >= top_k
        alive = jnp.where(take, ones, alive ^ ones)
        n_above = jnp.where(take, n_above, n_above + n_ones)
        thr_bits = thr_bits | jnp.where(take, i32(1 << (31 - p) if p else _INT_MIN), 0)
    thr_key = thr_bits ^ int_min
    lowest = jnp.float32(np.finfo(np.float32).min)
    thr = jnp.where(thr_key > int_min, from_key(thr_key), lowest)

    def fold(x):
        return x.reshape(tk // V7X_SUBLANES, V7X_SUBLANES, tq).sum(axis=0)

    def make_mask(thr_f):
        def body(j, acc):
            m = jnp.where(sc_ref[rows(j), :] >= thr_f, 0.0, _MASKED).astype(f32)
            madd_ref[rows(j), :] = m
            return acc + fold(m)
        acc = lax.fori_loop(0, n_tiles, body, jnp.zeros((V7X_SUBLANES, tq), f32))
        n_masked = jnp.sum(acc, axis=0, keepdims=True) * np.float32(1.0 / _MASKED)
        return n_tiles * tk - n_masked.astype(i32)

    qry = qi * tq + lax.broadcasted_iota(i32, (1, tq), 1)
    expected = jnp.minimum(top_k, (qry // CHUNK + 1) * CHUNK)
    n_kept = make_mask(thr)

    @pl.when(jnp.max((n_kept != expected).astype(i32)) > 0)
    def _():
        def key_idx(j):
            return j * tk + lax.broadcasted_iota(i32, (tk, tq), 0)

        def count_where(pred):
            def body(j, acc):
                return acc + fold(pred(sc_ref[rows(j), :], j).astype(i32))
            acc = lax.fori_loop(0, n_tiles, body, jnp.zeros((V7X_SUBLANES, tq), i32))
            return jnp.sum(acc, axis=0, keepdims=True)

        def value_pass(p, prefix):
            cand = prefix + lax.shift_left(i32(1), 31 - p)
            cand_f = from_key(cand)
            cnt = count_where(lambda st, j: st >= cand_f)
            return jnp.where(cnt >= top_k, cand, prefix)

        base_key = lax.fori_loop(0, KEY_BITS, value_pass, jnp.full((1, tq), _INT_MIN, i32))
        short = (expected < top_k) | (base_key == int_min)
        base = jnp.where(short, lowest, from_key(base_key))
        gap = from_key(base_key + 1) - base

        def gap_pass(p, delta):
            cand = delta + gap / lax.shift_left(i32(1), p + 1).astype(f32)
            cnt = count_where(lambda st, j: (st - base) >= cand)
            return jnp.where(cnt >= top_k, cand, delta)

        delta = lax.fori_loop(0, GAP_BITS, gap_pass, jnp.zeros((1, tq), f32))
        delta = jnp.where(short, 0.0, delta)
        n_keep = top_k - count_where(lambda st, j: (st - base) > delta)
        tied = (count_where(lambda st, j: (st - base) >= delta) > top_k) & jnp.logical_not(short)

        def index_pass(p, last):
            cand = last + lax.shift_left(i32(1), seq_bits - 1 - p)
            cnt = count_where(lambda st, j: ((st - base) == delta) & (key_idx(j) < cand))
            return jnp.where(cnt < n_keep, cand, last)

        last = lax.fori_loop(0, seq_bits, index_pass, jnp.zeros((1, tq), i32))

        def remask(j, carry):
            above = sc_ref[rows(j), :] - base
            keep = (above > delta) | ((above == delta) & (jnp.logical_not(tied) | (key_idx(j) <= last)))
            madd_ref[rows(j), :] = jnp.where(keep, 0.0, _MASKED).astype(f32)
            return carry

        lax.fori_loop(0, n_tiles, remask, 0)


    heads = [slice(h * hd, (h + 1) * hd) for h in range(N_HEADS_B)]

    bounds = []
    for h, hs in enumerate(heads):
        qt = q_ref[:, hs].astype(f32).T
        longest_key = jnp.sqrt(jnp.max(ksq_ref[h:h + 1, :]))
        bounds.append(jnp.sqrt(jnp.sum(qt * qt, axis=0, keepdims=True)) * longest_key + bmax_ref[h])

    def qk(j, h):
        return lax.dot_general(k_ref[rows(j), heads[h]], q_ref[:, heads[h]], _NT,
                               preferred_element_type=f32)

    def logits(j, h, near):
        s = qk(j, h) + madd_ref[rows(j), :]
        return s if near is None else s + bias_ref[h, near]

    acc_ref[...] = jnp.zeros(acc_ref.shape, f32)
    l_ref[...] = jnp.zeros(l_ref.shape, f32)

    def issue(j, buf):
        for h in range(N_HEADS_B):
            buf[h] = qk(j, h)

    def consume(j, buf, near):
        madd = madd_ref[rows(j), :]
        for h in range(N_HEADS_B):
            x = buf[h] + madd
            if near is not None:
                x = x + bias_ref[h, near]
            e = jnp.exp2(x - bounds[h])
            l_ref[h] += fold(e)
            acc_ref[h] += jnp.dot(vt_ref[heads[h], rows(j)], e.astype(bf16),
                                  preferred_element_type=f32)

    n_far = jnp.maximum(qi - 1, 0)
    issue(0, sa_ref)

    def far_pair(i, carry):
        j = 2 * i
        issue(j + 1, sb_ref)
        consume(j, sa_ref, None)
        issue(j + 2, sa_ref)
        consume(j + 1, sb_ref, None)
        return carry

    lax.fori_loop(0, n_far // 2, far_pair, 0)
    j0 = 2 * (n_far // 2)

    @pl.when(n_far % 2 == 1)
    def _():
        issue(j0 + 1, sb_ref)
        consume(j0, sa_ref, None)
        issue(j0 + 2, sa_ref)
        consume(j0 + 1, sb_ref, 1)
        consume(j0 + 2, sa_ref, 0)

    @pl.when((n_far % 2 == 0) & (qi >= 1))
    def _():
        issue(j0 + 1, sb_ref)
        consume(j0, sa_ref, 1)
        consume(j0 + 1, sb_ref, 0)

    @pl.when(qi == 0)
    def _():
        consume(0, sa_ref, 0)

    smallest = jnp.min(jnp.sum(l_ref[...], axis=1))

    @pl.when(jnp.logical_not(smallest > SUM_FLOOR))
    def _():
        acc_ref[...] = jnp.zeros(acc_ref.shape, f32)
        first_row = lax.broadcasted_iota(i32, (V7X_SUBLANES, tq), 0) == 0

        def attend_online(j, near):
            for h in range(N_HEADS_B):
                s = logits(j, h, near)
                m_old = m_ref[h]
                m_new = jnp.maximum(m_old, jnp.max(s, axis=0, keepdims=True))
                alpha = jnp.exp2(m_old - m_new)
                p = jnp.exp2(s - m_new)
                l_new = (alpha * jnp.sum(l_ref[h], axis=0, keepdims=True)
                         + jnp.sum(p, axis=0, keepdims=True))
                l_ref[h] = jnp.where(first_row, l_new, 0.0)
                pv = jnp.dot(vt_ref[heads[h], rows(j)], p.astype(bf16), preferred_element_type=f32)
                acc_ref[h] = alpha * acc_ref[h] + pv
                m_ref[h] = m_new

        m_ref[...] = jnp.full(m_ref.shape, _MASKED, f32)
        l_ref[...] = jnp.zeros(l_ref.shape, f32)

        def far_online(j, carry):
            attend_online(j, None)
            return carry

        lax.fori_loop(0, qi - 1, far_online, 0)

        @pl.when(qi >= 1)
        def _():
            attend_online(qi - 1, 1)

        attend_online(qi, 0)

    for h in range(N_HEADS_B):
        total = jnp.sum(l_ref[h], axis=0, keepdims=True)
        out_ref[:, heads[h]] = (acc_ref[h] / total).T.astype(out_ref.dtype)


def _dsa(bias_max, q, iq, iwt, k, vt, ik, ksq, bias_tabs, *, batch, tq, top_k):
    m, d_b = q.shape
    s = m // batch
    nq = s // tq
    hd = d_b // N_HEADS_B
    tile_rows = lambda bi, qi: (bi * nq + qi, 0)
    batch_rows = lambda bi, qi: (bi, 0)
    return pl.pallas_call(
        functools.partial(_dsa_body, tq=tq, top_k=top_k, seq_bits=int(np.log2(s))),
        grid=(batch, nq),
        in_specs=[
            pl.BlockSpec(memory_space=pltpu.SMEM),
            pl.BlockSpec((tq, d_b), tile_rows),
            pl.BlockSpec((IDX_HEADS, tq, IDX_DIM), lambda bi, qi: (0, bi * nq + qi, 0)),
            pl.BlockSpec((IDX_HEADS, tq), lambda bi, qi: (0, bi * nq + qi)),
            pl.BlockSpec((s, d_b), batch_rows),
            pl.BlockSpec((d_b, s), lambda bi, qi: (0, bi)),
            pl.BlockSpec((s, IDX_DIM), batch_rows),
            pl.BlockSpec((V7X_SUBLANES, s), lambda bi, qi: (0, bi)),
            pl.BlockSpec(bias_tabs.shape, lambda bi, qi: (0, 0, 0, 0)),
        ],
        out_specs=pl.BlockSpec((tq, d_b), tile_rows),
        out_shape=jax.ShapeDtypeStruct((m, d_b), bf16),
        scratch_shapes=[
            pltpu.VMEM((s, tq), f32),
            pltpu.VMEM((KEY_BITS, s // KEY_BITS, tq), i32),
            pltpu.VMEM((s, tq), f32),
            pltpu.VMEM((N_HEADS_B, tq, tq), f32),
            pltpu.VMEM((N_HEADS_B, tq, tq), f32),
            pltpu.VMEM((N_HEADS_B, 1, tq), f32),
            pltpu.VMEM((N_HEADS_B, V7X_SUBLANES, tq), f32),
            pltpu.VMEM((N_HEADS_B, hd, tq), f32),
        ],
        compiler_params=pltpu.CompilerParams(
            dimension_semantics=("arbitrary", "arbitrary"), vmem_limit_bytes=V7X_VMEM_LIMIT_BYTES),
        name="dsa",
    )(bias_max, q, iq, iwt, k, vt, ik, ksq, bias_tabs)


def _out_ffn_body(x_ref, a_ref, b_ref, woa_ref, wob_ref, gpm_ref, gpf_ref, wg_ref, wu_ref,
                  wd_ref, gpo_ref, o_ref, *, ff_chunk):
    mix = (jnp.dot(a_ref[...], woa_ref[...], preferred_element_type=f32)
           + jnp.dot(b_ref[...], wob_ref[...], preferred_element_type=f32))
    x1 = x_ref[...] + _rms(mix, gpm_ref[...])
    h = _rms(x1, gpf_ref[...]).astype(bf16)
    d_ff = wg_ref.shape[1]
    f = jnp.zeros(x1.shape, f32)
    for c in range(d_ff // ff_chunk):
        cs = slice(c * ff_chunk, (c + 1) * ff_chunk)
        gate = jnp.dot(h, wg_ref[:, cs], preferred_element_type=f32)
        up = jnp.dot(h, wu_ref[:, cs], preferred_element_type=f32)
        act = (gate * jax.nn.sigmoid(gate) * up).astype(bf16)
        f = f + jnp.dot(act, wd_ref[cs, :], preferred_element_type=f32)
    o_ref[...] = x1 + _rms(f, gpo_ref[...])


def _out_ffn(x2, out_a, out_b, w_oa, w_ob, g_pm, g_pf, w_gate, w_up, w_down, g_po, *, tm, ff_chunk):
    m, d = x2.shape
    d_a, d_b, d_ff = out_a.shape[1], out_b.shape[1], w_gate.shape[1]
    row = lambda i: (i, 0)
    const = lambda i: (0, 0)
    resident = functools.partial(pl.BlockSpec, index_map=const, pipeline_mode=pl.Buffered(1))
    return pl.pallas_call(
        functools.partial(_out_ffn_body, ff_chunk=ff_chunk),
        grid=(m // tm,),
        in_specs=[
            pl.BlockSpec((tm, d), row),
            pl.BlockSpec((tm, d_a), row),
            pl.BlockSpec((tm, d_b), row),
            resident((d_a, d)),
            resident((d_b, d)),
            resident((1, d)),
            resident((1, d)),
            resident((d, d_ff)),
            resident((d, d_ff)),
            resident((d_ff, d)),
            resident((1, d)),
        ],
        out_specs=pl.BlockSpec((tm, d), row),
        out_shape=jax.ShapeDtypeStruct((m, d), f32),
        compiler_params=pltpu.CompilerParams(
            dimension_semantics=("arbitrary",), vmem_limit_bytes=V7X_VMEM_LIMIT_BYTES),
        name="out_ffn",
    )(x2, out_a, out_b, w_oa, w_ob, g_pm, g_pf, w_gate, w_up, w_down, g_po)


def _t5_bucket(rel):
    nb = NUM_BUCKETS // 2
    ret = (rel > 0).astype(jnp.int32) * nb
    n = jnp.abs(rel)
    max_exact = nb // 2
    nf = jnp.maximum(n, 1).astype(jnp.float32)
    large = max_exact + (jnp.log(nf / max_exact) / np.float32(np.log(MAX_DISTANCE / max_exact))
                         * (nb - max_exact)).astype(jnp.int32)
    large = jnp.minimum(large, nb - 1)
    return ret + jnp.where(n < max_exact, n, large)


def _near_bias(rel_bias, tq):
    assert tq + 1 >= MAX_DISTANCE
    rel0 = jnp.arange(tq, dtype=jnp.int32)[:, None] - jnp.arange(tq, dtype=jnp.int32)[None, :]
    buckets = _t5_bucket(jnp.stack([rel0, rel0 - tq])) & (NUM_BUCKETS - 1)
    shifted = (rel_bias - rel_bias[NUM_BUCKETS // 2 - 1][None, :]) * np.float32(LOG2_E)
    tabs = jnp.zeros((rel_bias.shape[1],) + buckets.shape, f32)
    for b in range(NUM_BUCKETS):
        tabs = jnp.where(buckets[None] == b, shifted[b][:, None, None, None], tabs)
    return tabs, jnp.maximum(jnp.max(shifted, axis=0), 0.0)


def kernel(x, g_pre_mix, w_in, sgu_ln_g, sgu_ln_b, sgu_w, sgu_b, rel_bias, w_o, g_post_mix,
           g_pre_ffn, w_gate, w_up, w_down, g_post_ffn):
    b, s, d = x.shape
    depth = w_in.shape[0]
    d_a = d // 2
    d_b = d - d_a
    d_main = 2 * d_a + 3 * d_b + IDX_HEADS * IDX_DIM
    top_k = min(MAX_TOPK, s // 4)
    tq = KEY_BITS * V7X_SUBLANES
    tm = 512
    x2 = x.reshape(b * s, d)
    for l in range(depth):
        w_l = w_in[l]
        w_main = w_l[:, :d_main].astype(bf16)
        w_ik = jnp.pad(w_l[:, d_main:], ((0, 0), (0, V7X_LANES - (IDX_DIM + IDX_HEADS)))).astype(bf16)
        out_a, q, k, vt, iq, ik, iwt, ksq = _proj_sgu(
            x2, g_pre_mix[l][None], w_main, w_ik, sgu_ln_g[l][None], sgu_ln_b[l][None],
            sgu_w[l], jnp.transpose(sgu_b[l]), d_a=d_a, d_b=d_b, tm=tm)
        bias_tabs, bias_max = _near_bias(rel_bias, tq)
        out_b = _dsa(bias_max, q, iq, iwt, k, vt, ik, ksq, bias_tabs, batch=b, tq=tq, top_k=top_k)
        w_ol = w_o[l].astype(bf16)
        x2 = _out_ffn(x2, out_a, out_b, w_ol[:d_a], w_ol[d_a:],
                      g_post_mix[l][None], g_pre_ffn[l][None], w_gate[l].astype(bf16),
                      w_up[l].astype(bf16), w_down[l].astype(bf16), g_post_ffn[l][None],
                      tm=tm, ff_chunk=256)
    return x2.reshape(b, s, d)
```

```python
import functools

import numpy as np
import jax
import jax.numpy as jnp
from jax import lax
from jax.experimental import pallas as pl
from jax.experimental.pallas import tpu as pltpu

CHUNK = 64
SGU_BLOCK = 128
SGU_GROUPS = 4
N_HEADS_B = 4
IDX_HEADS = 8
IDX_DIM = 64
MAX_TOPK = 256
NUM_BUCKETS = 32
MAX_DISTANCE = 128
EPS = 1e-6

V7X_LANES = 128
V7X_SUBLANES = 8
V7X_VMEM_LIMIT_BYTES = 56 * 1024 * 1024

KEY_BITS = 32
GAP_BITS = 30
_INT_MIN = -(2 ** 31)
_MASKED = -(2.0 ** 100)
LOG2_E = float(np.log2(np.e))
SUM_FLOOR = 1e-30

_NT = (((1,), (1,)), ((), ()))

f32 = jnp.float32
bf16 = jnp.bfloat16
i32 = jnp.int32


def _rms(x, g):
    return x * lax.rsqrt(jnp.mean(x * x, axis=-1, keepdims=True) + EPS) * g


def _proj_sgu_body(x_ref, g_ref, w_ref, wik_ref, lng_ref, lnb_ref, sw_ref, sbt_ref,
                   outa_ref, q_ref, k_ref, vt_ref, iq_ref, ik_ref, iwt_ref, ksq_ref, *, d_a, d_b):
    tm = x_ref.shape[0]
    h = _rms(x_ref[...], g_ref[...]).astype(bf16)

    def proj(c0, width):
        return jnp.dot(h, w_ref[:, c0:c0 + width], preferred_element_type=f32)

    u = proj(0, d_a)
    a_v = proj(d_a, d_a)
    c0 = 2 * d_a
    q_ref[...] = (proj(c0, d_b) * np.float32((d_b // N_HEADS_B) ** -0.5 * LOG2_E)).astype(bf16)
    k = proj(c0 + d_b, d_b).astype(bf16)
    k_ref[...] = k
    kt = k.astype(f32).T
    hd = d_b // N_HEADS_B
    ksq = [jnp.sum(kt[a * hd:(a + 1) * hd] ** 2, axis=0, keepdims=True) for a in range(N_HEADS_B)]
    ksq_ref[...] = jnp.concatenate(ksq + [jnp.zeros((V7X_SUBLANES - N_HEADS_B, tm), f32)], axis=0)
    vt_ref[...] = proj(c0 + 2 * d_b, d_b).astype(bf16).T
    iq = (proj(c0 + 3 * d_b, IDX_HEADS * IDX_DIM) * np.float32(IDX_DIM ** -0.5)).astype(bf16)
    for hh in range(IDX_HEADS):
        iq_ref[hh] = iq[:, hh * IDX_DIM:(hh + 1) * IDX_DIM]
    ikw = jnp.dot(h, wik_ref[...], preferred_element_type=f32)
    ik_ref[...] = ikw[:, :IDX_DIM].astype(bf16)
    iwt_ref[...] = ikw.T[IDX_DIM:IDX_DIM + IDX_HEADS, :] * np.float32(IDX_HEADS ** -0.5)

    mu = jnp.mean(a_v, axis=-1, keepdims=True)
    cen = a_v - mu
    var = jnp.mean(cen * cen, axis=-1, keepdims=True)
    vn = (cen * lax.rsqrt(var + EPS) * lng_ref[...] + lnb_ref[...]).astype(bf16)

    row_chunk = lax.broadcasted_iota(i32, (SGU_BLOCK, SGU_BLOCK), 0) // CHUNK
    col_chunk = lax.broadcasted_iota(i32, (SGU_BLOCK, SGU_BLOCK), 1) // CHUNK
    causal = col_chunk <= row_chunk
    gdim = d_a // SGU_GROUPS
    for g in range(SGU_GROUPS):
        w_g = jnp.where(causal, sw_ref[g], 0.0).astype(bf16)
        b_g = sbt_ref[:, g:g + 1]
        for c in range(tm // SGU_BLOCK):
            rows = slice(c * SGU_BLOCK, (c + 1) * SGU_BLOCK)
            cols = slice(g * gdim, (g + 1) * gdim)
            mixed = jnp.dot(w_g, vn[rows, cols], preferred_element_type=f32) + b_g
            outa_ref[rows, cols] = (u[rows, cols] * mixed).astype(bf16)


def _proj_sgu(x2, g_pre, w_main, w_ik, ln_g, ln_b, sgu_w, sgu_bt, *, d_a, d_b, tm):
    m, d = x2.shape
    const = lambda i: (0, 0)
    row = lambda i: (i, 0)
    col = lambda i: (0, i)
    return pl.pallas_call(
        functools.partial(_proj_sgu_body, d_a=d_a, d_b=d_b),
        grid=(m // tm,),
        in_specs=[
            pl.BlockSpec((tm, d), row),
            pl.BlockSpec((1, d), const),
            pl.BlockSpec(w_main.shape, const),
            pl.BlockSpec(w_ik.shape, const),
            pl.BlockSpec((1, d_a), const),
            pl.BlockSpec((1, d_a), const),
            pl.BlockSpec(sgu_w.shape, lambda i: (0, 0, 0)),
            pl.BlockSpec(sgu_bt.shape, const),
        ],
        out_specs=[
            pl.BlockSpec((tm, d_a), row),
            pl.BlockSpec((tm, d_b), row),
            pl.BlockSpec((tm, d_b), row),
            pl.BlockSpec((d_b, tm), col),
            pl.BlockSpec((IDX_HEADS, tm, IDX_DIM), lambda i: (0, i, 0)),
            pl.BlockSpec((tm, IDX_DIM), row),
            pl.BlockSpec((IDX_HEADS, tm), col),
            pl.BlockSpec((V7X_SUBLANES, tm), col),
        ],
        out_shape=[
            jax.ShapeDtypeStruct((m, d_a), bf16),
            jax.ShapeDtypeStruct((m, d_b), bf16),
            jax.ShapeDtypeStruct((m, d_b), bf16),
            jax.ShapeDtypeStruct((d_b, m), bf16),
            jax.ShapeDtypeStruct((IDX_HEADS, m, IDX_DIM), bf16),
            jax.ShapeDtypeStruct((m, IDX_DIM), bf16),
            jax.ShapeDtypeStruct((IDX_HEADS, m), f32),
            jax.ShapeDtypeStruct((V7X_SUBLANES, m), f32),
        ],
        compiler_params=pltpu.CompilerParams(
            dimension_semantics=("arbitrary",), vmem_limit_bytes=V7X_VMEM_LIMIT_BYTES),
        name="proj_sgu",
    )(x2, g_pre, w_main, w_ik, ln_g, ln_b, sgu_w, sgu_bt)


def _bit_planes(words):
    a = list(words)
    j, m = 16, 0x0000FFFF
    while j:
        k = 0
        while k < KEY_BITS:
            t = (a[k] ^ lax.shift_right_logical(a[k + j], i32(j))) & i32(m)
            a[k] = a[k] ^ t
            a[k + j] = a[k + j] ^ lax.shift_left(t, i32(j))
            k = (k + j + 1) & ~j
        j >>= 1
        m = (m ^ (m << j)) & 0xFFFFFFFF
    return a


def _dsa_body(bmax_ref, q_ref, iq_ref, iwt_ref, k_ref, vt_ref, ik_ref, ksq_ref, bias_ref, out_ref,
              sc_ref, planes_ref, aux_ref, madd_ref, sa_ref, sb_ref, m_ref, l_ref, acc_ref,
              *, tq, top_k, seq_bits):
    tk = tq
    group = KEY_BITS * V7X_SUBLANES
    assert tk == group
    hd = q_ref.shape[1] // N_HEADS_B
    qi = pl.program_id(1)
    n_tiles = qi + 1
    int_min = i32(_INT_MIN)

    def rows(j):
        return pl.ds(pl.multiple_of(j * tk, tk), tk)

    @pl.when((pl.program_id(0) == 0) & (qi == 0))
    def _():
        planes_ref[...] = jnp.zeros(planes_ref.shape, i32)

    def to_key(x):
        bits = lax.bitcast_convert_type(x, i32)
        return bits ^ ((bits >> 31) & i32(0x7FFFFFFF))

    def from_key(key):
        return lax.bitcast_convert_type(key ^ ((key >> 31) & i32(0x7FFFFFFF)), f32)

    def score_tile(j, diagonal):
        ik_j = ik_ref[rows(j), :]
        sc = jnp.zeros((tk, tq), f32)
        for h in range(IDX_HEADS):
            d = lax.dot_general(ik_j, iq_ref[h], _NT, preferred_element_type=f32)
            sc = sc + iwt_ref[h:h + 1, :] * jnp.maximum(d, 0.0)
        key = to_key(sc)
        if diagonal:
            key_chunk = lax.broadcasted_iota(i32, (tk, tq), 0) // CHUNK
            qry_chunk = lax.broadcasted_iota(i32, (tk, tq), 1) // CHUNK
            visible = key_chunk <= qry_chunk
            sc = jnp.where(visible, sc, -jnp.inf)
            key = jnp.where(visible, key, int_min)
        sc_ref[rows(j), :] = sc
        return key

    def plane_tile(j, key):
        prow = pl.ds(pl.multiple_of(j * V7X_SUBLANES, V7X_SUBLANES), V7X_SUBLANES)
        for lt in range(tq // V7X_LANES):
            lanes = slice(lt * V7X_LANES, (lt + 1) * V7X_LANES)
            planes = _bit_planes([key[V7X_SUBLANES * i:V7X_SUBLANES * (i + 1), lanes]
                                  for i in range(KEY_BITS)])
            planes_ref[0, prow, lanes] = ~planes[0]
            for p in range(1, KEY_BITS):
                planes_ref[p, prow, lanes] = planes[p]

    def far_scores(j, prev):
        key = score_tile(j, False)
        plane_tile(j - 1, prev)
        return key

    prev = lax.fori_loop(1, qi, far_scores, score_tile(0, False))
    last = score_tile(qi, True)
    plane_tile(jnp.maximum(qi - 1, 0), prev)
    plane_tile(qi, last)

    n_prow = planes_ref.shape[1]
    zero_row = jnp.zeros((1, tq), i32)
    live = lax.broadcasted_iota(i32, (n_prow, tq), 0) < n_tiles * V7X_SUBLANES
    alive, n_above, thr_bits = jnp.where(live, i32(-1), 0), zero_row, zero_row
    for p in range(KEY_BITS):
        ones = alive & planes_ref[p]
        n_ones = jnp.sum(lax.population_count(ones), axis=0, keepdims=True)
        take = (n_above + n_ones) >= top_k
        alive = jnp.where(take, ones, alive ^ ones)
        n_above = jnp.where(take, n_above, n_above + n_ones)
        thr_bits = thr_bits | jnp.where(take, i32(1 << (31 - p) if p else _INT_MIN), 0)
    thr_key = thr_bits ^ int_min
    n_equal = jnp.sum(lax.population_count(alive), axis=0, keepdims=True)
    found = thr_key > int_min
    lowest = jnp.float32(np.finfo(np.float32).min)
    thr = jnp.where(found, from_key(thr_key), lowest)
    tie = found & (n_above + n_equal > top_k)

    def fold(x):
        return x.reshape(tk // V7X_SUBLANES, V7X_SUBLANES, tq).sum(axis=0)

    def make_mask(thr_f):
        def body(j, acc):
            m = jnp.where(sc_ref[rows(j), :] >= thr_f, 0.0, _MASKED).astype(f32)
            madd_ref[rows(j), :] = m
            return acc + fold(m)
        acc = lax.fori_loop(0, n_tiles, body, jnp.zeros((V7X_SUBLANES, tq), f32))
        n_masked = jnp.sum(acc, axis=0, keepdims=True) * np.float32(1.0 / _MASKED)
        return n_tiles * tk - n_masked.astype(i32)

    def key_idx(j):
        return j * tk + lax.broadcasted_iota(i32, (tk, tq), 0)

    def count_where(pred):
        def body(j, acc):
            return acc + fold(pred(sc_ref[rows(j), :], j).astype(i32))
        acc = lax.fori_loop(0, n_tiles, body, jnp.zeros((V7X_SUBLANES, tq), i32))
        return jnp.sum(acc, axis=0, keepdims=True)

    def keep_lowest(base, delta, tied, n_keep):
        def index_pass(p, last):
            cand = last + lax.shift_left(i32(1), seq_bits - 1 - p)
            cnt = count_where(lambda st, j: ((st - base) == delta) & (key_idx(j) < cand))
            return jnp.where(cnt < n_keep, cand, last)

        last = lax.fori_loop(0, seq_bits, index_pass, zero_row)

        def remask(j, carry):
            above = sc_ref[rows(j), :] - base
            keep = (above > delta) | ((above == delta) & (jnp.logical_not(tied) | (key_idx(j) <= last)))
            madd_ref[rows(j), :] = jnp.where(keep, 0.0, _MASKED).astype(f32)
            return carry

        lax.fori_loop(0, n_tiles, remask, 0)

    qry = qi * tq + lax.broadcasted_iota(i32, (1, tq), 1)
    n_visible = (qry // CHUNK + 1) * CHUNK
    n_kept = make_mask(thr)

    aux_ref[...] = jnp.broadcast_to(n_equal, aux_ref.shape)

    @pl.when(jnp.max(tie.astype(i32)) > 0)
    def _():
        aux_ref[...] = jnp.broadcast_to(count_where(lambda st, j: st == thr), aux_ref.shape)

    differ = (n_kept != jnp.where(found, n_above + n_equal, n_visible)) | (tie & (aux_ref[0:1, :] != n_equal))
    agree = jnp.max(differ.astype(i32)) == 0

    @pl.when(agree & (jnp.max(tie.astype(i32)) > 0))
    def _():
        keep_lowest(thr, jnp.float32(0.0), tie, top_k - n_above)

    @pl.when(jnp.logical_not(agree))
    def _():
        def value_pass(p, prefix):
            cand = prefix + lax.shift_left(i32(1), 31 - p)
            cand_f = from_key(cand)
            cnt = count_where(lambda st, j: st >= cand_f)
            return jnp.where(cnt >= top_k, cand, prefix)

        base_key = lax.fori_loop(0, KEY_BITS, value_pass, jnp.full((1, tq), _INT_MIN, i32))
        short = (n_visible < top_k) | (base_key == int_min)
        base = jnp.where(short, lowest, from_key(base_key))
        gap = from_key(base_key + 1) - base

        def gap_pass(p, delta):
            cand = delta + gap / lax.shift_left(i32(1), p + 1).astype(f32)
            cnt = count_where(lambda st, j: (st - base) >= cand)
            return jnp.where(cnt >= top_k, cand, delta)

        delta = lax.fori_loop(0, GAP_BITS, gap_pass, jnp.zeros((1, tq), f32))
        delta = jnp.where(short, 0.0, delta)
        n_keep = top_k - count_where(lambda st, j: (st - base) > delta)
        tied = (count_where(lambda st, j: (st - base) >= delta) > top_k) & jnp.logical_not(short)
        keep_lowest(base, delta, tied, n_keep)

    heads = [slice(h * hd, (h + 1) * hd) for h in range(N_HEADS_B)]

    bounds = []
    for h, hs in enumerate(heads):
        qt = q_ref[:, hs].astype(f32).T
        longest_key = jnp.sqrt(jnp.max(ksq_ref[h:h + 1, :]))
        bounds.append(jnp.sqrt(jnp.sum(qt * qt, axis=0, keepdims=True)) * longest_key + bmax_ref[h])

    def qk(j, h):
        return lax.dot_general(k_ref[rows(j), heads[h]], q_ref[:, heads[h]], _NT,
                               preferred_element_type=f32)

    def logits(j, h, near):
        s = qk(j, h) + madd_ref[rows(j), :]
        return s if near is None else s + bias_ref[h, near]

    acc_ref[...] = jnp.zeros(acc_ref.shape, f32)
    l_ref[...] = jnp.zeros(l_ref.shape, f32)

    def issue(j, buf):
        for h in range(N_HEADS_B):
            buf[h] = qk(j, h)

    def consume(j, buf, near):
        madd = madd_ref[rows(j), :]
        for h in range(N_HEADS_B):
            x = buf[h] + madd
            if near is not None:
                x = x + bias_ref[h, near]
            e = jnp.exp2(x - bounds[h])
            l_ref[h] += fold(e)
            acc_ref[h] += jnp.dot(vt_ref[heads[h], rows(j)], e.astype(bf16),
                                  preferred_element_type=f32)

    n_far = jnp.maximum(qi - 1, 0)
    issue(0, sa_ref)

    def far_pair(i, carry):
        j = 2 * i
        issue(j + 1, sb_ref)
        consume(j, sa_ref, None)
        issue(j + 2, sa_ref)
        consume(j + 1, sb_ref, None)
        return carry

    lax.fori_loop(0, n_far // 2, far_pair, 0)
    j0 = 2 * (n_far // 2)

    @pl.when(n_far % 2 == 1)
    def _():
        issue(j0 + 1, sb_ref)
        consume(j0, sa_ref, None)
        issue(j0 + 2, sa_ref)
        consume(j0 + 1, sb_ref, 1)
        consume(j0 + 2, sa_ref, 0)

    @pl.when((n_far % 2 == 0) & (qi >= 1))
    def _():
        issue(j0 + 1, sb_ref)
        consume(j0, sa_ref, 1)
        consume(j0 + 1, sb_ref, 0)

    @pl.when(qi == 0)
    def _():
        consume(0, sa_ref, 0)

    smallest = jnp.min(jnp.sum(l_ref[...], axis=1))

    @pl.when(jnp.logical_not(smallest > SUM_FLOOR))
    def _():
        acc_ref[...] = jnp.zeros(acc_ref.shape, f32)
        first_row = lax.broadcasted_iota(i32, (V7X_SUBLANES, tq), 0) == 0

        def attend_online(j, near):
            for h in range(N_HEADS_B):
                s = logits(j, h, near)
                m_old = m_ref[h]
                m_new = jnp.maximum(m_old, jnp.max(s, axis=0, keepdims=True))
                alpha = jnp.exp2(m_old - m_new)
                p = jnp.exp2(s - m_new)
                l_new = (alpha * jnp.sum(l_ref[h], axis=0, keepdims=True)
                         + jnp.sum(p, axis=0, keepdims=True))
                l_ref[h] = jnp.where(first_row, l_new, 0.0)
                pv = jnp.dot(vt_ref[heads[h], rows(j)], p.astype(bf16), preferred_element_type=f32)
                acc_ref[h] = alpha * acc_ref[h] + pv
                m_ref[h] = m_new

        m_ref[...] = jnp.full(m_ref.shape, _MASKED, f32)
        l_ref[...] = jnp.zeros(l_ref.shape, f32)

        def far_online(j, carry):
            attend_online(j, None)
            return carry

        lax.fori_loop(0, qi - 1, far_online, 0)

        @pl.when(qi >= 1)
        def _():
            attend_online(qi - 1, 1)

        attend_online(qi, 0)

    for h in range(N_HEADS_B):
        total = jnp.sum(l_ref[h], axis=0, keepdims=True)
        out_ref[:, heads[h]] = (acc_ref[h] / total).T.astype(out_ref.dtype)


def _dsa(bias_max, q, iq, iwt, k, vt, ik, ksq, bias_tabs, *, batch, tq, top_k):
    m, d_b = q.shape
    s = m // batch
    nq = s // tq
    hd = d_b // N_HEADS_B
    tile_rows = lambda bi, qi: (bi * nq + qi, 0)
    batch_rows = lambda bi, qi: (bi, 0)
    return pl.pallas_call(
        functools.partial(_dsa_body, tq=tq, top_k=top_k, seq_bits=int(np.log2(s))),
        grid=(batch, nq),
        in_specs=[
            pl.BlockSpec(memory_space=pltpu.SMEM),
            pl.BlockSpec((tq, d_b), tile_rows),
            pl.BlockSpec((IDX_HEADS, tq, IDX_DIM), lambda bi, qi: (0, bi * nq + qi, 0)),
            pl.BlockSpec((IDX_HEADS, tq), lambda bi, qi: (0, bi * nq + qi)),
            pl.BlockSpec((s, d_b), batch_rows),
            pl.BlockSpec((d_b, s), lambda bi, qi: (0, bi)),
            pl.BlockSpec((s, IDX_DIM), batch_rows),
            pl.BlockSpec((V7X_SUBLANES, s), lambda bi, qi: (0, bi)),
            pl.BlockSpec(bias_tabs.shape, lambda bi, qi: (0, 0, 0, 0)),
        ],
        out_specs=pl.BlockSpec((tq, d_b), tile_rows),
        out_shape=jax.ShapeDtypeStruct((m, d_b), bf16),
        scratch_shapes=[
            pltpu.VMEM((s, tq), f32),
            pltpu.VMEM((KEY_BITS, s // KEY_BITS, tq), i32),
            pltpu.VMEM((V7X_SUBLANES, tq), i32),
            pltpu.VMEM((s, tq), f32),
            pltpu.VMEM((N_HEADS_B, tq, tq), f32),
            pltpu.VMEM((N_HEADS_B, tq, tq), f32),
            pltpu.VMEM((N_HEADS_B, 1, tq), f32),
            pltpu.VMEM((N_HEADS_B, V7X_SUBLANES, tq), f32),
            pltpu.VMEM((N_HEADS_B, hd, tq), f32),
        ],
        compiler_params=pltpu.CompilerParams(
            dimension_semantics=("arbitrary", "arbitrary"), vmem_limit_bytes=V7X_VMEM_LIMIT_BYTES),
        name="dsa",
    )(bias_max, q, iq, iwt, k, vt, ik, ksq, bias_tabs)


def _out_ffn_body(x_ref, a_ref, b_ref, woa_ref, wob_ref, gpm_ref, gpf_ref, wg_ref, wu_ref,
                  wd_ref, gpo_ref, o_ref, *, ff_chunk):
    mix = (jnp.dot(a_ref[...], woa_ref[...], preferred_element_type=f32)
           + jnp.dot(b_ref[...], wob_ref[...], preferred_element_type=f32))
    x1 = x_ref[...] + _rms(mix, gpm_ref[...])
    h = _rms(x1, gpf_ref[...]).astype(bf16)
    d_ff = wg_ref.shape[1]
    f = jnp.zeros(x1.shape, f32)
    for c in range(d_ff // ff_chunk):
        cs = slice(c * ff_chunk, (c + 1) * ff_chunk)
        gate = jnp.dot(h, wg_ref[:, cs], preferred_element_type=f32)
        up = jnp.dot(h, wu_ref[:, cs], preferred_element_type=f32)
        act = (gate * jax.nn.sigmoid(gate) * up).astype(bf16)
        f = f + jnp.dot(act, wd_ref[cs, :], preferred_element_type=f32)
    o_ref[...] = x1 + _rms(f, gpo_ref[...])


def _out_ffn(x2, out_a, out_b, w_oa, w_ob, g_pm, g_pf, w_gate, w_up, w_down, g_po, *, tm, ff_chunk):
    m, d = x2.shape
    d_a, d_b, d_ff = out_a.shape[1], out_b.shape[1], w_gate.shape[1]
    row = lambda i: (i, 0)
    const = lambda i: (0, 0)
    resident = functools.partial(pl.BlockSpec, index_map=const, pipeline_mode=pl.Buffered(1))
    return pl.pallas_call(
        functools.partial(_out_ffn_body, ff_chunk=ff_chunk),
        grid=(m // tm,),
        in_specs=[
            pl.BlockSpec((tm, d), row),
            pl.BlockSpec((tm, d_a), row),
            pl.BlockSpec((tm, d_b), row),
            resident((d_a, d)),
            resident((d_b, d)),
            resident((1, d)),
            resident((1, d)),
            resident((d, d_ff)),
            resident((d, d_ff)),
            resident((d_ff, d)),
            resident((1, d)),
        ],
        out_specs=pl.BlockSpec((tm, d), row),
        out_shape=jax.ShapeDtypeStruct((m, d), f32),
        compiler_params=pltpu.CompilerParams(
            dimension_semantics=("arbitrary",), vmem_limit_bytes=V7X_VMEM_LIMIT_BYTES),
        name="out_ffn",
    )(x2, out_a, out_b, w_oa, w_ob, g_pm, g_pf, w_gate, w_up, w_down, g_po)


def _t5_bucket(rel):
    nb = NUM_BUCKETS // 2
    ret = (rel > 0).astype(jnp.int32) * nb
    n = jnp.abs(rel)
    max_exact = nb // 2
    nf = jnp.maximum(n, 1).astype(jnp.float32)
    large = max_exact + (jnp.log(nf / max_exact) / np.float32(np.log(MAX_DISTANCE / max_exact))
                         * (nb - max_exact)).astype(jnp.int32)
    large = jnp.minimum(large, nb - 1)
    return ret + jnp.where(n < max_exact, n, large)


def _near_bias(rel_bias, tq):
    assert tq + 1 >= MAX_DISTANCE
    rel0 = jnp.arange(tq, dtype=jnp.int32)[:, None] - jnp.arange(tq, dtype=jnp.int32)[None, :]
    buckets = _t5_bucket(jnp.stack([rel0, rel0 - tq])) & (NUM_BUCKETS - 1)
    shifted = (rel_bias - rel_bias[NUM_BUCKETS // 2 - 1][None, :]) * np.float32(LOG2_E)
    tabs = jnp.zeros((rel_bias.shape[1],) + buckets.shape, f32)
    for b in range(NUM_BUCKETS):
        tabs = jnp.where(buckets[None] == b, shifted[b][:, None, None, None], tabs)
    return tabs, jnp.maximum(jnp.max(shifted, axis=0), 0.0)


def kernel(x, g_pre_mix, w_in, sgu_ln_g, sgu_ln_b, sgu_w, sgu_b, rel_bias, w_o, g_post_mix,
           g_pre_ffn, w_gate, w_up, w_down, g_post_ffn):
    b, s, d = x.shape
    depth = w_in.shape[0]
    d_a = d // 2
    d_b = d - d_a
    d_main = 2 * d_a + 3 * d_b + IDX_HEADS * IDX_DIM
    top_k = min(MAX_TOPK, s // 4)
    tq = KEY_BITS * V7X_SUBLANES
    tm = 512
    x2 = x.reshape(b * s, d)
    for l in range(depth):
        w_l = w_in[l]
        w_main = w_l[:, :d_main].astype(bf16)
        w_ik = jnp.pad(w_l[:, d_main:], ((0, 0), (0, V7X_LANES - (IDX_DIM + IDX_HEADS)))).astype(bf16)
        out_a, q, k, vt, iq, ik, iwt, ksq = _proj_sgu(
            x2, g_pre_mix[l][None], w_main, w_ik, sgu_ln_g[l][None], sgu_ln_b[l][None],
            sgu_w[l], jnp.transpose(sgu_b[l]), d_a=d_a, d_b=d_b, tm=tm)
        bias_tabs, bias_max = _near_bias(rel_bias, tq)
        out_b = _dsa(bias_max, q, iq, iwt, k, vt, ik, ksq, bias_tabs, batch=b, tq=tq, top_k=top_k)
        w_ol = w_o[l].astype(bf16)
        x2 = _out_ffn(x2, out_a, out_b, w_ol[:d_a], w_ol[d_a:],
                      g_post_mix[l][None], g_pre_ffn[l][None], w_gate[l].astype(bf16),
                      w_up[l].astype(bf16), w_down[l].astype(bf16), g_post_ffn[l][None],
                      tm=tm, ff_chunk=256)
    return x2.reshape(b, s, d)
```

```python
import functools

import numpy as np
import jax
import jax.numpy as jnp
from jax import lax
from jax.experimental import pallas as pl
from jax.experimental.pallas import tpu as pltpu

CHUNK = 64
SGU_BLOCK = 128
SGU_GROUPS = 4
N_HEADS_B = 4
IDX_HEADS = 8
IDX_DIM = 64
MAX_TOPK = 256
NUM_BUCKETS = 32
MAX_DISTANCE = 128
EPS = 1e-6

V7X_LANES = 128
V7X_SUBLANES = 8
V7X_VMEM_LIMIT_BYTES = 56 * 1024 * 1024

KEY_BITS = 32
GAP_BITS = 30
_INT_MIN = -(2 ** 31)
_MASKED = -(2.0 ** 100)
LOG2_E = float(np.log2(np.e))
SUM_FLOOR = 1e-30

_NT = (((1,), (1,)), ((), ()))

f32 = jnp.float32
bf16 = jnp.bfloat16
i32 = jnp.int32


def _rms(x, g):
    return x * lax.rsqrt(jnp.mean(x * x, axis=-1, keepdims=True) + EPS) * g


def _proj_sgu_body(x_ref, g_ref, w_ref, wik_ref, lng_ref, lnb_ref, sw_ref, sbt_ref,
                   outa_ref, q_ref, k_ref, vt_ref, iq_ref, ik_ref, iwt_ref, ksq_ref, *, d_a, d_b):
    tm = x_ref.shape[0]
    h = _rms(x_ref[...], g_ref[...]).astype(bf16)

    def proj(c0, width):
        return jnp.dot(h, w_ref[:, c0:c0 + width], preferred_element_type=f32)

    u = proj(0, d_a)
    a_v = proj(d_a, d_a)
    c0 = 2 * d_a
    q_ref[...] = (proj(c0, d_b) * np.float32((d_b // N_HEADS_B) ** -0.5 * LOG2_E)).astype(bf16)
    k = proj(c0 + d_b, d_b).astype(bf16)
    k_ref[...] = k
    kt = k.astype(f32).T
    hd = d_b // N_HEADS_B
    ksq = [jnp.sum(kt[a * hd:(a + 1) * hd] ** 2, axis=0, keepdims=True) for a in range(N_HEADS_B)]
    ksq_ref[...] = jnp.concatenate(ksq + [jnp.zeros((V7X_SUBLANES - N_HEADS_B, tm), f32)], axis=0)
    vt_ref[...] = proj(c0 + 2 * d_b, d_b).astype(bf16).T
    iq = (proj(c0 + 3 * d_b, IDX_HEADS * IDX_DIM) * np.float32(IDX_DIM ** -0.5)).astype(bf16)
    for hh in range(IDX_HEADS):
        iq_ref[hh] = iq[:, hh * IDX_DIM:(hh + 1) * IDX_DIM]
    ikw = jnp.dot(h, wik_ref[...], preferred_element_type=f32)
    ik_ref[...] = ikw[:, :IDX_DIM].astype(bf16)
    iwt_ref[...] = ikw.T[IDX_DIM:IDX_DIM + IDX_HEADS, :] * np.float32(IDX_HEADS ** -0.5)

    mu = jnp.mean(a_v, axis=-1, keepdims=True)
    cen = a_v - mu
    var = jnp.mean(cen * cen, axis=-1, keepdims=True)
    vn = (cen * lax.rsqrt(var + EPS) * lng_ref[...] + lnb_ref[...]).astype(bf16)

    row_chunk = lax.broadcasted_iota(i32, (SGU_BLOCK, SGU_BLOCK), 0) // CHUNK
    col_chunk = lax.broadcasted_iota(i32, (SGU_BLOCK, SGU_BLOCK), 1) // CHUNK
    causal = col_chunk <= row_chunk
    gdim = d_a // SGU_GROUPS
    for g in range(SGU_GROUPS):
        w_g = jnp.where(causal, sw_ref[g], 0.0).astype(bf16)
        b_g = sbt_ref[:, g:g + 1]
        for c in range(tm // SGU_BLOCK):
            rows = slice(c * SGU_BLOCK, (c + 1) * SGU_BLOCK)
            cols = slice(g * gdim, (g + 1) * gdim)
            mixed = jnp.dot(w_g, vn[rows, cols], preferred_element_type=f32) + b_g
            outa_ref[rows, cols] = (u[rows, cols] * mixed).astype(bf16)


def _proj_sgu(x2, g_pre, w_main, w_ik, ln_g, ln_b, sgu_w, sgu_bt, *, d_a, d_b, tm):
    m, d = x2.shape
    const = lambda i: (0, 0)
    row = lambda i: (i, 0)
    col = lambda i: (0, i)
    return pl.pallas_call(
        functools.partial(_proj_sgu_body, d_a=d_a, d_b=d_b),
        grid=(m // tm,),
        in_specs=[
            pl.BlockSpec((tm, d), row),
            pl.BlockSpec((1, d), const),
            pl.BlockSpec(w_main.shape, const),
            pl.BlockSpec(w_ik.shape, const),
            pl.BlockSpec((1, d_a), const),
            pl.BlockSpec((1, d_a), const),
            pl.BlockSpec(sgu_w.shape, lambda i: (0, 0, 0)),
            pl.BlockSpec(sgu_bt.shape, const),
        ],
        out_specs=[
            pl.BlockSpec((tm, d_a), row),
            pl.BlockSpec((tm, d_b), row),
            pl.BlockSpec((tm, d_b), row),
            pl.BlockSpec((d_b, tm), col),
            pl.BlockSpec((IDX_HEADS, tm, IDX_DIM), lambda i: (0, i, 0)),
            pl.BlockSpec((tm, IDX_DIM), row),
            pl.BlockSpec((IDX_HEADS, tm), col),
            pl.BlockSpec((V7X_SUBLANES, tm), col),
        ],
        out_shape=[
            jax.ShapeDtypeStruct((m, d_a), bf16),
            jax.ShapeDtypeStruct((m, d_b), bf16),
            jax.ShapeDtypeStruct((m, d_b), bf16),
            jax.ShapeDtypeStruct((d_b, m), bf16),
            jax.ShapeDtypeStruct((IDX_HEADS, m, IDX_DIM), bf16),
            jax.ShapeDtypeStruct((m, IDX_DIM), bf16),
            jax.ShapeDtypeStruct((IDX_HEADS, m), f32),
            jax.ShapeDtypeStruct((V7X_SUBLANES, m), f32),
        ],
        compiler_params=pltpu.CompilerParams(
            dimension_semantics=("arbitrary",), vmem_limit_bytes=V7X_VMEM_LIMIT_BYTES),
        name="proj_sgu",
    )(x2, g_pre, w_main, w_ik, ln_g, ln_b, sgu_w, sgu_bt)


def _bit_planes(words):
    a = list(words)
    j, m = 16, 0x0000FFFF
    while j:
        k = 0
        while k < KEY_BITS:
            t = (a[k] ^ lax.shift_right_logical(a[k + j], i32(j))) & i32(m)
            a[k] = a[k] ^ t
            a[k + j] = a[k + j] ^ lax.shift_left(t, i32(j))
            k = (k + j + 1) & ~j
        j >>= 1
        m = (m ^ (m << j)) & 0xFFFFFFFF
    return a


def _dsa_body(bmax_ref, q_ref, iq_ref, iwt_ref, k_ref, vt_ref, ik_ref, ksq_ref, bias_ref, out_ref,
              sc_ref, planes_ref, aux_ref, madd_ref, sa_ref, sb_ref, m_ref, l_ref, acc_ref,
              *, tq, top_k, seq_bits):
    tk = tq
    group = KEY_BITS * V7X_SUBLANES
    assert tk == group
    hd = q_ref.shape[1] // N_HEADS_B
    qi = pl.program_id(1)
    n_tiles = qi + 1
    int_min = i32(_INT_MIN)

    def rows(j):
        return pl.ds(pl.multiple_of(j * tk, tk), tk)

    @pl.when((pl.program_id(0) == 0) & (qi == 0))
    def _():
        planes_ref[...] = jnp.zeros(planes_ref.shape, i32)

    def to_key(x):
        bits = lax.bitcast_convert_type(x, i32)
        return bits ^ ((bits >> 31) & i32(0x7FFFFFFF))

    def from_key(key):
        return lax.bitcast_convert_type(key ^ ((key >> 31) & i32(0x7FFFFFFF)), f32)

    def score_tile(j, diagonal):
        ik_j = ik_ref[rows(j), :]
        sc = jnp.zeros((tk, tq), f32)
        for h in range(IDX_HEADS):
            d = lax.dot_general(ik_j, iq_ref[h], _NT, preferred_element_type=f32)
            sc = sc + iwt_ref[h:h + 1, :] * jnp.maximum(d, 0.0)
        key = to_key(sc)
        if diagonal:
            key_chunk = lax.broadcasted_iota(i32, (tk, tq), 0) // CHUNK
            qry_chunk = lax.broadcasted_iota(i32, (tk, tq), 1) // CHUNK
            visible = key_chunk <= qry_chunk
            sc = jnp.where(visible, sc, -jnp.inf)
            key = jnp.where(visible, key, int_min)
        sc_ref[rows(j), :] = sc
        return key

    def plane_tile(j, key):
        prow = pl.ds(pl.multiple_of(j * V7X_SUBLANES, V7X_SUBLANES), V7X_SUBLANES)
        for lt in range(tq // V7X_LANES):
            lanes = slice(lt * V7X_LANES, (lt + 1) * V7X_LANES)
            planes = _bit_planes([key[V7X_SUBLANES * i:V7X_SUBLANES * (i + 1), lanes]
                                  for i in range(KEY_BITS)])
            planes_ref[0, prow, lanes] = ~planes[0]
            for p in range(1, KEY_BITS):
                planes_ref[p, prow, lanes] = planes[p]

    def far_scores(j, prev):
        key = score_tile(j, False)
        plane_tile(j - 1, prev)
        return key

    prev = lax.fori_loop(1, qi, far_scores, score_tile(0, False))
    last = score_tile(qi, True)
    plane_tile(jnp.maximum(qi - 1, 0), prev)
    plane_tile(qi, last)

    n_prow = planes_ref.shape[1]
    zero_row = jnp.zeros((1, tq), i32)
    live = lax.broadcasted_iota(i32, (n_prow, tq), 0) < n_tiles * V7X_SUBLANES
    alive, n_above, thr_bits = jnp.where(live, i32(-1), 0), zero_row, zero_row
    for p in range(KEY_BITS):
        ones = alive & planes_ref[p]
        n_ones = jnp.sum(lax.population_count(ones), axis=0, keepdims=True)
        take = (n_above + n_ones) >= top_k
        alive = jnp.where(take, ones, alive ^ ones)
        n_above = jnp.where(take, n_above, n_above + n_ones)
        thr_bits = thr_bits | jnp.where(take, i32(1 << (31 - p) if p else _INT_MIN), 0)
    thr_key = thr_bits ^ int_min
    n_equal = jnp.sum(lax.population_count(alive), axis=0, keepdims=True)
    found = thr_key > int_min
    lowest = jnp.float32(np.finfo(np.float32).min)
    thr = jnp.where(found, from_key(thr_key), lowest)
    tie = found & (n_above + n_equal > top_k)

    def fold(x):
        return x.reshape(tk // V7X_SUBLANES, V7X_SUBLANES, tq).sum(axis=0)

    def make_mask(thr_f):
        def body(j, acc):
            m = jnp.where(sc_ref[rows(j), :] >= thr_f, 0.0, _MASKED).astype(f32)
            madd_ref[rows(j), :] = m
            return acc + fold(m)
        acc = lax.fori_loop(0, n_tiles, body, jnp.zeros((V7X_SUBLANES, tq), f32))
        n_masked = jnp.sum(acc, axis=0, keepdims=True) * np.float32(1.0 / _MASKED)
        return n_tiles * tk - n_masked.astype(i32)

    def key_idx(j):
        return j * tk + lax.broadcasted_iota(i32, (tk, tq), 0)

    def count_where(pred):
        def body(j, acc):
            return acc + fold(pred(sc_ref[rows(j), :], j).astype(i32))
        acc = lax.fori_loop(0, n_tiles, body, jnp.zeros((V7X_SUBLANES, tq), i32))
        return jnp.sum(acc, axis=0, keepdims=True)

    def last_kept(n_keep, count_before):
        def index_pass(p, last):
            cand = last + lax.shift_left(i32(1), seq_bits - 1 - p)
            return jnp.where(count_before(cand) < n_keep, cand, last)
        return lax.fori_loop(0, seq_bits, index_pass, zero_row)

    def remask(base, delta, tied, last):
        def body(j, acc):
            above = sc_ref[rows(j), :] - base
            keep = (above > delta) | ((above == delta) & (jnp.logical_not(tied) | (key_idx(j) <= last)))
            m = jnp.where(keep, 0.0, _MASKED).astype(f32)
            madd_ref[rows(j), :] = m
            return acc + fold(m)
        acc = lax.fori_loop(0, n_tiles, body, jnp.zeros((V7X_SUBLANES, tq), f32))
        n_masked = jnp.sum(acc, axis=0, keepdims=True) * np.float32(1.0 / _MASKED)
        return n_tiles * tk - n_masked.astype(i32)

    qry = qi * tq + lax.broadcasted_iota(i32, (1, tq), 1)
    n_visible = (qry // CHUNK + 1) * CHUNK
    n_kept = make_mask(thr)

    agree = jnp.max((n_kept != jnp.where(found, n_above + n_equal, n_visible)).astype(i32)) == 0
    aux_ref[...] = jnp.broadcast_to(n_kept, aux_ref.shape)

    @pl.when(agree & (jnp.max(tie.astype(i32)) > 0))
    def _():
        word = lax.broadcasted_iota(i32, (n_prow, tq), 0)
        first_key = (word // V7X_SUBLANES) * tk + word % V7X_SUBLANES

        def tied_before(cand):
            n_bits = jnp.clip((cand - first_key + (V7X_SUBLANES - 1)) // V7X_SUBLANES, 0, KEY_BITS)
            low_bits = jnp.where(n_bits >= KEY_BITS, i32(-1),
                                 lax.shift_left(i32(1), jnp.minimum(n_bits, KEY_BITS - 1)) - 1)
            return jnp.sum(lax.population_count(alive & low_bits), axis=0, keepdims=True)

        last = last_kept(top_k - n_above, tied_before)
        aux_ref[...] = jnp.broadcast_to(remask(thr, jnp.float32(0.0), tie, last), aux_ref.shape)

    settled = agree & (jnp.max((aux_ref[0:1, :] != jnp.where(found, top_k, n_visible)).astype(i32)) == 0)

    @pl.when(jnp.logical_not(settled))
    def _():
        def value_pass(p, prefix):
            cand = prefix + lax.shift_left(i32(1), 31 - p)
            cand_f = from_key(cand)
            cnt = count_where(lambda st, j: st >= cand_f)
            return jnp.where(cnt >= top_k, cand, prefix)

        base_key = lax.fori_loop(0, KEY_BITS, value_pass, jnp.full((1, tq), _INT_MIN, i32))
        short = (n_visible < top_k) | (base_key == int_min)
        base = jnp.where(short, lowest, from_key(base_key))
        gap = from_key(base_key + 1) - base

        def gap_pass(p, delta):
            cand = delta + gap / lax.shift_left(i32(1), p + 1).astype(f32)
            cnt = count_where(lambda st, j: (st - base) >= cand)
            return jnp.where(cnt >= top_k, cand, delta)

        delta = lax.fori_loop(0, GAP_BITS, gap_pass, jnp.zeros((1, tq), f32))
        delta = jnp.where(short, 0.0, delta)
        n_keep = top_k - count_where(lambda st, j: (st - base) > delta)
        tied = (count_where(lambda st, j: (st - base) >= delta) > top_k) & jnp.logical_not(short)
        last = last_kept(n_keep, lambda cand: count_where(
            lambda st, j: ((st - base) == delta) & (key_idx(j) < cand)))
        remask(base, delta, tied, last)

    heads = [slice(h * hd, (h + 1) * hd) for h in range(N_HEADS_B)]

    bounds = []
    for h, hs in enumerate(heads):
        qt = q_ref[:, hs].astype(f32).T
        longest_key = jnp.sqrt(jnp.max(ksq_ref[h:h + 1, :]))
        bounds.append(jnp.sqrt(jnp.sum(qt * qt, axis=0, keepdims=True)) * longest_key + bmax_ref[h])

    def qk(j, h):
        return lax.dot_general(k_ref[rows(j), heads[h]], q_ref[:, heads[h]], _NT,
                               preferred_element_type=f32)

    def logits(j, h, near):
        s = qk(j, h) + madd_ref[rows(j), :]
        return s if near is None else s + bias_ref[h, near]

    acc_ref[...] = jnp.zeros(acc_ref.shape, f32)
    l_ref[...] = jnp.zeros(l_ref.shape, f32)

    def issue(j, buf):
        for h in range(N_HEADS_B):
            buf[h] = qk(j, h)

    def consume(j, buf, near):
        madd = madd_ref[rows(j), :]
        for h in range(N_HEADS_B):
            x = buf[h] + madd
            if near is not None:
                x = x + bias_ref[h, near]
            e = jnp.exp2(x - bounds[h])
            l_ref[h] += fold(e)
            acc_ref[h] += jnp.dot(vt_ref[heads[h], rows(j)], e.astype(bf16),
                                  preferred_element_type=f32)

    n_far = jnp.maximum(qi - 1, 0)
    issue(0, sa_ref)

    def far_pair(i, carry):
        j = 2 * i
        issue(j + 1, sb_ref)
        consume(j, sa_ref, None)
        issue(j + 2, sa_ref)
        consume(j + 1, sb_ref, None)
        return carry

    lax.fori_loop(0, n_far // 2, far_pair, 0)
    j0 = 2 * (n_far // 2)

    @pl.when(n_far % 2 == 1)
    def _():
        issue(j0 + 1, sb_ref)
        consume(j0, sa_ref, None)
        issue(j0 + 2, sa_ref)
        consume(j0 + 1, sb_ref, 1)
        consume(j0 + 2, sa_ref, 0)

    @pl.when((n_far % 2 == 0) & (qi >= 1))
    def _():
        issue(j0 + 1, sb_ref)
        consume(j0, sa_ref, 1)
        consume(j0 + 1, sb_ref, 0)

    @pl.when(qi == 0)
    def _():
        consume(0, sa_ref, 0)

    smallest = jnp.min(jnp.sum(l_ref[...], axis=1))

    @pl.when(jnp.logical_not(smallest > SUM_FLOOR))
    def _():
        acc_ref[...] = jnp.zeros(acc_ref.shape, f32)
        first_row = lax.broadcasted_iota(i32, (V7X_SUBLANES, tq), 0) == 0

        def attend_online(j, near):
            for h in range(N_HEADS_B):
                s = logits(j, h, near)
                m_old = m_ref[h]
                m_new = jnp.maximum(m_old, jnp.max(s, axis=0, keepdims=True))
                alpha = jnp.exp2(m_old - m_new)
                p = jnp.exp2(s - m_new)
                l_new = (alpha * jnp.sum(l_ref[h], axis=0, keepdims=True)
                         + jnp.sum(p, axis=0, keepdims=True))
                l_ref[h] = jnp.where(first_row, l_new, 0.0)
                pv = jnp.dot(vt_ref[heads[h], rows(j)], p.astype(bf16), preferred_element_type=f32)
                acc_ref[h] = alpha * acc_ref[h] + pv
                m_ref[h] = m_new

        m_ref[...] = jnp.full(m_ref.shape, _MASKED, f32)
        l_ref[...] = jnp.zeros(l_ref.shape, f32)

        def far_online(j, carry):
            attend_online(j, None)
            return carry

        lax.fori_loop(0, qi - 1, far_online, 0)

        @pl.when(qi >= 1)
        def _():
            attend_online(qi - 1, 1)

        attend_online(qi, 0)

    for h in range(N_HEADS_B):
        total = jnp.sum(l_ref[h], axis=0, keepdims=True)
        out_ref[:, heads[h]] = (acc_ref[h] / total).T.astype(out_ref.dtype)


def _dsa(bias_max, q, iq, iwt, k, vt, ik, ksq, bias_tabs, *, batch, tq, top_k):
    m, d_b = q.shape
    s = m // batch
    nq = s // tq
    hd = d_b // N_HEADS_B
    tile_rows = lambda bi, qi: (bi * nq + qi, 0)
    batch_rows = lambda bi, qi: (bi, 0)
    return pl.pallas_call(
        functools.partial(_dsa_body, tq=tq, top_k=top_k, seq_bits=int(np.log2(s))),
        grid=(batch, nq),
        in_specs=[
            pl.BlockSpec(memory_space=pltpu.SMEM),
            pl.BlockSpec((tq, d_b), tile_rows),
            pl.BlockSpec((IDX_HEADS, tq, IDX_DIM), lambda bi, qi: (0, bi * nq + qi, 0)),
            pl.BlockSpec((IDX_HEADS, tq), lambda bi, qi: (0, bi * nq + qi)),
            pl.BlockSpec((s, d_b), batch_rows),
            pl.BlockSpec((d_b, s), lambda bi, qi: (0, bi)),
            pl.BlockSpec((s, IDX_DIM), batch_rows),
            pl.BlockSpec((V7X_SUBLANES, s), lambda bi, qi: (0, bi)),
            pl.BlockSpec(bias_tabs.shape, lambda bi, qi: (0, 0, 0, 0)),
        ],
        out_specs=pl.BlockSpec((tq, d_b), tile_rows),
        out_shape=jax.ShapeDtypeStruct((m, d_b), bf16),
        scratch_shapes=[
            pltpu.VMEM((s, tq), f32),
            pltpu.VMEM((KEY_BITS, s // KEY_BITS, tq), i32),
            pltpu.VMEM((V7X_SUBLANES, tq), i32),
            pltpu.VMEM((s, tq), f32),
            pltpu.VMEM((N_HEADS_B, tq, tq), f32),
            pltpu.VMEM((N_HEADS_B, tq, tq), f32),
            pltpu.VMEM((N_HEADS_B, 1, tq), f32),
            pltpu.VMEM((N_HEADS_B, V7X_SUBLANES, tq), f32),
            pltpu.VMEM((N_HEADS_B, hd, tq), f32),
        ],
        compiler_params=pltpu.CompilerParams(
            dimension_semantics=("arbitrary", "arbitrary"), vmem_limit_bytes=V7X_VMEM_LIMIT_BYTES),
        name="dsa",
    )(bias_max, q, iq, iwt, k, vt, ik, ksq, bias_tabs)


def _out_ffn_body(x_ref, a_ref, b_ref, woa_ref, wob_ref, gpm_ref, gpf_ref, wg_ref, wu_ref,
                  wd_ref, gpo_ref, o_ref, *, ff_chunk):
    mix = (jnp.dot(a_ref[...], woa_ref[...], preferred_element_type=f32)
           + jnp.dot(b_ref[...], wob_ref[...], preferred_element_type=f32))
    x1 = x_ref[...] + _rms(mix, gpm_ref[...])
    h = _rms(x1, gpf_ref[...]).astype(bf16)
    d_ff = wg_ref.shape[1]
    f = jnp.zeros(x1.shape, f32)
    for c in range(d_ff // ff_chunk):
        cs = slice(c * ff_chunk, (c + 1) * ff_chunk)
        gate = jnp.dot(h, wg_ref[:, cs], preferred_element_type=f32)
        up = jnp.dot(h, wu_ref[:, cs], preferred_element_type=f32)
        act = (gate * jax.nn.sigmoid(gate) * up).astype(bf16)
        f = f + jnp.dot(act, wd_ref[cs, :], preferred_element_type=f32)
    o_ref[...] = x1 + _rms(f, gpo_ref[...])


def _out_ffn(x2, out_a, out_b, w_oa, w_ob, g_pm, g_pf, w_gate, w_up, w_down, g_po, *, tm, ff_chunk):
    m, d = x2.shape
    d_a, d_b, d_ff = out_a.shape[1], out_b.shape[1], w_gate.shape[1]
    row = lambda i: (i, 0)
    const = lambda i: (0, 0)
    resident = functools.partial(pl.BlockSpec, index_map=const, pipeline_mode=pl.Buffered(1))
    return pl.pallas_call(
        functools.partial(_out_ffn_body, ff_chunk=ff_chunk),
        grid=(m // tm,),
        in_specs=[
            pl.BlockSpec((tm, d), row),
            pl.BlockSpec((tm, d_a), row),
            pl.BlockSpec((tm, d_b), row),
            resident((d_a, d)),
            resident((d_b, d)),
            resident((1, d)),
            resident((1, d)),
            resident((d, d_ff)),
            resident((d, d_ff)),
            resident((d_ff, d)),
            resident((1, d)),
        ],
        out_specs=pl.BlockSpec((tm, d), row),
        out_shape=jax.ShapeDtypeStruct((m, d), f32),
        compiler_params=pltpu.CompilerParams(
            dimension_semantics=("arbitrary",), vmem_limit_bytes=V7X_VMEM_LIMIT_BYTES),
        name="out_ffn",
    )(x2, out_a, out_b, w_oa, w_ob, g_pm, g_pf, w_gate, w_up, w_down, g_po)


def _t5_bucket(rel):
    nb = NUM_BUCKETS // 2
    ret = (rel > 0).astype(jnp.int32) * nb
    n = jnp.abs(rel)
    max_exact = nb // 2
    nf = jnp.maximum(n, 1).astype(jnp.float32)
    large = max_exact + (jnp.log(nf / max_exact) / np.float32(np.log(MAX_DISTANCE / max_exact))
                         * (nb - max_exact)).astype(jnp.int32)
    large = jnp.minimum(large, nb - 1)
    return ret + jnp.where(n < max_exact, n, large)


def _near_bias(rel_bias, tq):
    assert tq + 1 >= MAX_DISTANCE
    rel0 = jnp.arange(tq, dtype=jnp.int32)[:, None] - jnp.arange(tq, dtype=jnp.int32)[None, :]
    buckets = _t5_bucket(jnp.stack([rel0, rel0 - tq])) & (NUM_BUCKETS - 1)
    shifted = (rel_bias - rel_bias[NUM_BUCKETS // 2 - 1][None, :]) * np.float32(LOG2_E)
    tabs = jnp.zeros((rel_bias.shape[1],) + buckets.shape, f32)
    for b in range(NUM_BUCKETS):
        tabs = jnp.where(buckets[None] == b, shifted[b][:, None, None, None], tabs)
    return tabs, jnp.maximum(jnp.max(shifted, axis=0), 0.0)


def kernel(x, g_pre_mix, w_in, sgu_ln_g, sgu_ln_b, sgu_w, sgu_b, rel_bias, w_o, g_post_mix,
           g_pre_ffn, w_gate, w_up, w_down, g_post_ffn):
    b, s, d = x.shape
    depth = w_in.shape[0]
    d_a = d // 2
    d_b = d - d_a
    d_main = 2 * d_a + 3 * d_b + IDX_HEADS * IDX_DIM
    top_k = min(MAX_TOPK, s // 4)
    tq = KEY_BITS * V7X_SUBLANES
    tm = 512
    x2 = x.reshape(b * s, d)
    for l in range(depth):
        w_l = w_in[l]
        w_main = w_l[:, :d_main].astype(bf16)
        w_ik = jnp.pad(w_l[:, d_main:], ((0, 0), (0, V7X_LANES - (IDX_DIM + IDX_HEADS)))).astype(bf16)
        out_a, q, k, vt, iq, ik, iwt, ksq = _proj_sgu(
            x2, g_pre_mix[l][None], w_main, w_ik, sgu_ln_g[l][None], sgu_ln_b[l][None],
            sgu_w[l], jnp.transpose(sgu_b[l]), d_a=d_a, d_b=d_b, tm=tm)
        bias_tabs, bias_max = _near_bias(rel_bias, tq)
        out_b = _dsa(bias_max, q, iq, iwt, k, vt, ik, ksq, bias_tabs, batch=b, tq=tq, top_k=top_k)
        w_ol = w_o[l].astype(bf16)
        x2 = _out_ffn(x2, out_a, out_b, w_ol[:d_a], w_ol[d_a:],
                      g_post_mix[l][None], g_pre_ffn[l][None], w_gate[l].astype(bf16),
                      w_up[l].astype(bf16), w_down[l].astype(bf16), g_post_ffn[l][None],
                      tm=tm, ff_chunk=256)
    return x2.reshape(b, s, d)
```

```python
import functools

import numpy as np
import jax
import jax.numpy as jnp
from jax import lax
from jax.experimental import pallas as pl
from jax.experimental.pallas import tpu as pltpu

CHUNK = 64
SGU_BLOCK = 128
SGU_GROUPS = 4
N_HEADS_B = 4
IDX_HEADS = 8
IDX_DIM = 64
MAX_TOPK = 256
NUM_BUCKETS = 32
MAX_DISTANCE = 128
EPS = 1e-6

V7X_LANES = 128
V7X_SUBLANES = 8
V7X_VMEM_LIMIT_BYTES = 56 * 1024 * 1024

KEY_BITS = 32
GAP_BITS = 30
_INT_MIN = -(2 ** 31)
_MASKED = -(2.0 ** 100)
LOG2_E = float(np.log2(np.e))
SUM_FLOOR = 1e-30

_NT = (((1,), (1,)), ((), ()))

f32 = jnp.float32
bf16 = jnp.bfloat16
i32 = jnp.int32


def _rms(x, g):
    return x * lax.rsqrt(jnp.mean(x * x, axis=-1, keepdims=True) + EPS) * g


def _proj_sgu_body(x_ref, g_ref, w_ref, wik_ref, lng_ref, lnb_ref, sw_ref, sbt_ref,
                   outa_ref, q_ref, k_ref, vt_ref, iq_ref, ik_ref, iwt_ref, ksq_ref, *, d_a, d_b):
    tm = x_ref.shape[0]
    h = _rms(x_ref[...], g_ref[...]).astype(bf16)

    def proj(c0, width):
        return jnp.dot(h, w_ref[:, c0:c0 + width], preferred_element_type=f32)

    u = proj(0, d_a)
    a_v = proj(d_a, d_a)
    c0 = 2 * d_a
    q_ref[...] = (proj(c0, d_b) * np.float32((d_b // N_HEADS_B) ** -0.5 * LOG2_E)).astype(bf16)
    k = proj(c0 + d_b, d_b).astype(bf16)
    k_ref[...] = k
    kt = k.astype(f32).T
    hd = d_b // N_HEADS_B
    ksq = [jnp.sum(kt[a * hd:(a + 1) * hd] ** 2, axis=0, keepdims=True) for a in range(N_HEADS_B)]
    ksq_ref[...] = jnp.concatenate(ksq + [jnp.zeros((V7X_SUBLANES - N_HEADS_B, tm), f32)], axis=0)
    vt_ref[...] = proj(c0 + 2 * d_b, d_b).astype(bf16).T
    iq = (proj(c0 + 3 * d_b, IDX_HEADS * IDX_DIM) * np.float32(IDX_DIM ** -0.5)).astype(bf16)
    for hh in range(IDX_HEADS):
        iq_ref[hh] = iq[:, hh * IDX_DIM:(hh + 1) * IDX_DIM]
    ikw = jnp.dot(h, wik_ref[...], preferred_element_type=f32)
    ik_ref[...] = ikw[:, :IDX_DIM].astype(bf16)
    iwt_ref[...] = ikw.T[IDX_DIM:IDX_DIM + IDX_HEADS, :] * np.float32(IDX_HEADS ** -0.5)

    mu = jnp.mean(a_v, axis=-1, keepdims=True)
    cen = a_v - mu
    var = jnp.mean(cen * cen, axis=-1, keepdims=True)
    vn = (cen * lax.rsqrt(var + EPS) * lng_ref[...] + lnb_ref[...]).astype(bf16)

    row_chunk = lax.broadcasted_iota(i32, (SGU_BLOCK, SGU_BLOCK), 0) // CHUNK
    col_chunk = lax.broadcasted_iota(i32, (SGU_BLOCK, SGU_BLOCK), 1) // CHUNK
    causal = col_chunk <= row_chunk
    gdim = d_a // SGU_GROUPS
    for g in range(SGU_GROUPS):
        w_g = jnp.where(causal, sw_ref[g], 0.0).astype(bf16)
        b_g = sbt_ref[:, g:g + 1]
        for c in range(tm // SGU_BLOCK):
            rows = slice(c * SGU_BLOCK, (c + 1) * SGU_BLOCK)
            cols = slice(g * gdim, (g + 1) * gdim)
            mixed = jnp.dot(w_g, vn[rows, cols], preferred_element_type=f32) + b_g
            outa_ref[rows, cols] = (u[rows, cols] * mixed).astype(bf16)


def _proj_sgu(x2, g_pre, w_main, w_ik, ln_g, ln_b, sgu_w, sgu_bt, *, d_a, d_b, tm):
    m, d = x2.shape
    const = lambda i: (0, 0)
    row = lambda i: (i, 0)
    col = lambda i: (0, i)
    return pl.pallas_call(
        functools.partial(_proj_sgu_body, d_a=d_a, d_b=d_b),
        grid=(m // tm,),
        in_specs=[
            pl.BlockSpec((tm, d), row),
            pl.BlockSpec((1, d), const),
            pl.BlockSpec(w_main.shape, const),
            pl.BlockSpec(w_ik.shape, const),
            pl.BlockSpec((1, d_a), const),
            pl.BlockSpec((1, d_a), const),
            pl.BlockSpec(sgu_w.shape, lambda i: (0, 0, 0)),
            pl.BlockSpec(sgu_bt.shape, const),
        ],
        out_specs=[
            pl.BlockSpec((tm, d_a), row),
            pl.BlockSpec((tm, d_b), row),
            pl.BlockSpec((tm, d_b), row),
            pl.BlockSpec((d_b, tm), col),
            pl.BlockSpec((IDX_HEADS, tm, IDX_DIM), lambda i: (0, i, 0)),
            pl.BlockSpec((tm, IDX_DIM), row),
            pl.BlockSpec((IDX_HEADS, tm), col),
            pl.BlockSpec((V7X_SUBLANES, tm), col),
        ],
        out_shape=[
            jax.ShapeDtypeStruct((m, d_a), bf16),
            jax.ShapeDtypeStruct((m, d_b), bf16),
            jax.ShapeDtypeStruct((m, d_b), bf16),
            jax.ShapeDtypeStruct((d_b, m), bf16),
            jax.ShapeDtypeStruct((IDX_HEADS, m, IDX_DIM), bf16),
            jax.ShapeDtypeStruct((m, IDX_DIM), bf16),
            jax.ShapeDtypeStruct((IDX_HEADS, m), f32),
            jax.ShapeDtypeStruct((V7X_SUBLANES, m), f32),
        ],
        compiler_params=pltpu.CompilerParams(
            dimension_semantics=("arbitrary",), vmem_limit_bytes=V7X_VMEM_LIMIT_BYTES),
        name="proj_sgu",
    )(x2, g_pre, w_main, w_ik, ln_g, ln_b, sgu_w, sgu_bt)


def _bit_planes(words):
    a = list(words)
    j, m = 16, 0x0000FFFF
    while j:
        k = 0
        while k < KEY_BITS:
            t = (a[k] ^ lax.shift_right_logical(a[k + j], i32(j))) & i32(m)
            a[k] = a[k] ^ t
            a[k + j] = a[k + j] ^ lax.shift_left(t, i32(j))
            k = (k + j + 1) & ~j
        j >>= 1
        m = (m ^ (m << j)) & 0xFFFFFFFF
    return a


def _dsa_body(bmax_ref, q_ref, iq_ref, iwt_ref, k_ref, vt_ref, ik_ref, ksq_ref, bias_ref, out_ref,
              sc_ref, planes_ref, aux_ref, madd_ref, sa_ref, sb_ref, m_ref, l_ref, acc_ref,
              *, tq, top_k, seq_bits):
    tk = tq
    group = KEY_BITS * V7X_SUBLANES
    assert tk == group
    hd = q_ref.shape[1] // N_HEADS_B
    qi = pl.program_id(1)
    n_tiles = qi + 1
    int_min = i32(_INT_MIN)

    def rows(j):
        return pl.ds(pl.multiple_of(j * tk, tk), tk)

    @pl.when((pl.program_id(0) == 0) & (qi == 0))
    def _():
        planes_ref[...] = jnp.zeros(planes_ref.shape, i32)

    def to_key(x):
        bits = lax.bitcast_convert_type(x, i32)
        return bits ^ ((bits >> 31) & i32(0x7FFFFFFF))

    def from_key(key):
        return lax.bitcast_convert_type(key ^ ((key >> 31) & i32(0x7FFFFFFF)), f32)

    def score_tile(j, diagonal):
        ik_j = ik_ref[rows(j), :]
        sc = jnp.zeros((tk, tq), f32)
        for h in range(IDX_HEADS):
            d = lax.dot_general(ik_j, iq_ref[h], _NT, preferred_element_type=f32)
            sc = sc + iwt_ref[h:h + 1, :] * jnp.maximum(d, 0.0)
        key = to_key(sc)
        if diagonal:
            key_chunk = lax.broadcasted_iota(i32, (tk, tq), 0) // CHUNK
            qry_chunk = lax.broadcasted_iota(i32, (tk, tq), 1) // CHUNK
            visible = key_chunk <= qry_chunk
            sc = jnp.where(visible, sc, -jnp.inf)
            key = jnp.where(visible, key, int_min)
        sc_ref[rows(j), :] = sc
        return key

    def plane_tile(j, key):
        prow = pl.ds(pl.multiple_of(j * V7X_SUBLANES, V7X_SUBLANES), V7X_SUBLANES)
        for lt in range(tq // V7X_LANES):
            lanes = slice(lt * V7X_LANES, (lt + 1) * V7X_LANES)
            planes = _bit_planes([key[V7X_SUBLANES * i:V7X_SUBLANES * (i + 1), lanes]
                                  for i in range(KEY_BITS)])
            planes_ref[0, prow, lanes] = ~planes[0]
            for p in range(1, KEY_BITS):
                planes_ref[p, prow, lanes] = planes[p]

    def far_scores(j, prev):
        key = score_tile(j, False)
        plane_tile(j - 1, prev)
        return key

    prev = lax.fori_loop(1, qi, far_scores, score_tile(0, False))
    last = score_tile(qi, True)
    plane_tile(jnp.maximum(qi - 1, 0), prev)
    plane_tile(qi, last)

    n_prow = planes_ref.shape[1]
    zero_row = jnp.zeros((1, tq), i32)
    live = lax.broadcasted_iota(i32, (n_prow, tq), 0) < n_tiles * V7X_SUBLANES
    alive, n_above, thr_bits = jnp.where(live, i32(-1), 0), zero_row, zero_row
    for p in range(KEY_BITS):
        ones = alive & planes_ref[p]
        n_ones = jnp.sum(lax.population_count(ones), axis=0, keepdims=True)
        take = (n_above + n_ones) >= top_k
        alive = jnp.where(take, ones, alive ^ ones)
        n_above = jnp.where(take, n_above, n_above + n_ones)
        thr_bits = thr_bits | jnp.where(take, i32(1 << (31 - p) if p else _INT_MIN), 0)
    thr_key = thr_bits ^ int_min
    n_equal = jnp.sum(lax.population_count(alive), axis=0, keepdims=True)
    found = thr_key > int_min
    lowest = jnp.float32(np.finfo(np.float32).min)
    thr = jnp.where(found, from_key(thr_key), lowest)
    tie = found & (n_above + n_equal > top_k)

    def fold(x):
        return x.reshape(tk // V7X_SUBLANES, V7X_SUBLANES, tq).sum(axis=0)

    def make_mask(thr_f):
        def body(j, acc):
            m = jnp.where(sc_ref[rows(j), :] >= thr_f, 0.0, _MASKED).astype(f32)
            madd_ref[rows(j), :] = m
            return acc + fold(m)
        acc = lax.fori_loop(0, n_tiles, body, jnp.zeros((V7X_SUBLANES, tq), f32))
        n_masked = jnp.sum(acc, axis=0, keepdims=True) * np.float32(1.0 / _MASKED)
        return n_tiles * tk - n_masked.astype(i32)

    def key_idx(j):
        return j * tk + lax.broadcasted_iota(i32, (tk, tq), 0)

    def count_where(pred):
        def body(j, acc):
            return acc + fold(pred(sc_ref[rows(j), :], j).astype(i32))
        acc = lax.fori_loop(0, n_tiles, body, jnp.zeros((V7X_SUBLANES, tq), i32))
        return jnp.sum(acc, axis=0, keepdims=True)

    def last_kept(n_keep, count_before):
        def index_pass(p, last):
            cand = last + lax.shift_left(i32(1), seq_bits - 1 - p)
            return jnp.where(count_before(cand) < n_keep, cand, last)
        return lax.fori_loop(0, seq_bits, index_pass, zero_row)

    def remask(base, delta, tied, last):
        def body(j, acc):
            above = sc_ref[rows(j), :] - base
            keep = (above > delta) | ((above == delta) & (jnp.logical_not(tied) | (key_idx(j) <= last)))
            m = jnp.where(keep, 0.0, _MASKED).astype(f32)
            madd_ref[rows(j), :] = m
            return acc + fold(m)
        acc = lax.fori_loop(0, n_tiles, body, jnp.zeros((V7X_SUBLANES, tq), f32))
        n_masked = jnp.sum(acc, axis=0, keepdims=True) * np.float32(1.0 / _MASKED)
        return n_tiles * tk - n_masked.astype(i32)

    qry = qi * tq + lax.broadcasted_iota(i32, (1, tq), 1)
    n_visible = (qry // CHUNK + 1) * CHUNK
    n_kept = make_mask(thr)

    agree = jnp.max((n_kept != jnp.where(found, n_above + n_equal, n_visible)).astype(i32)) == 0
    aux_ref[...] = jnp.broadcast_to(n_kept, aux_ref.shape)

    @pl.when(agree & (jnp.max(tie.astype(i32)) > 0))
    def _():
        word = lax.broadcasted_iota(i32, (n_prow, tq), 0)
        log2_sublanes = V7X_SUBLANES.bit_length() - 1
        first_key = (word >> log2_sublanes) * tk + (word & (V7X_SUBLANES - 1))

        def tied_before(cand):
            n_bits = jnp.clip((cand - first_key + (V7X_SUBLANES - 1)) >> log2_sublanes, 0, KEY_BITS)
            low_bits = jnp.where(n_bits >= KEY_BITS, i32(-1),
                                 lax.shift_left(i32(1), jnp.minimum(n_bits, KEY_BITS - 1)) - 1)
            return jnp.sum(lax.population_count(alive & low_bits), axis=0, keepdims=True)

        last = last_kept(top_k - n_above, tied_before)
        aux_ref[...] = jnp.broadcast_to(remask(thr, jnp.float32(0.0), tie, last), aux_ref.shape)

    settled = agree & (jnp.max((aux_ref[0:1, :] != jnp.where(found, top_k, n_visible)).astype(i32)) == 0)

    @pl.when(jnp.logical_not(settled))
    def _():
        def value_pass(p, prefix):
            cand = prefix + lax.shift_left(i32(1), 31 - p)
            cand_f = from_key(cand)
            cnt = count_where(lambda st, j: st >= cand_f)
            return jnp.where(cnt >= top_k, cand, prefix)

        base_key = lax.fori_loop(0, KEY_BITS, value_pass, jnp.full((1, tq), _INT_MIN, i32))
        short = (n_visible < top_k) | (base_key == int_min)
        base = jnp.where(short, lowest, from_key(base_key))
        gap = from_key(base_key + 1) - base

        def gap_pass(p, delta):
            cand = delta + gap / lax.shift_left(i32(1), p + 1).astype(f32)
            cnt = count_where(lambda st, j: (st - base) >= cand)
            return jnp.where(cnt >= top_k, cand, delta)

        delta = lax.fori_loop(0, GAP_BITS, gap_pass, jnp.zeros((1, tq), f32))
        delta = jnp.where(short, 0.0, delta)
        n_keep = top_k - count_where(lambda st, j: (st - base) > delta)
        tied = (count_where(lambda st, j: (st - base) >= delta) > top_k) & jnp.logical_not(short)
        last = last_kept(n_keep, lambda cand: count_where(
            lambda st, j: ((st - base) == delta) & (key_idx(j) < cand)))
        remask(base, delta, tied, last)

    heads = [slice(h * hd, (h + 1) * hd) for h in range(N_HEADS_B)]

    bounds = []
    for h, hs in enumerate(heads):
        qt = q_ref[:, hs].astype(f32).T
        longest_key = jnp.sqrt(jnp.max(ksq_ref[h:h + 1, :]))
        bounds.append(jnp.sqrt(jnp.sum(qt * qt, axis=0, keepdims=True)) * longest_key + bmax_ref[h])

    def qk(j, h):
        return lax.dot_general(k_ref[rows(j), heads[h]], q_ref[:, heads[h]], _NT,
                               preferred_element_type=f32)

    def logits(j, h, near):
        s = qk(j, h) + madd_ref[rows(j), :]
        return s if near is None else s + bias_ref[h, near]

    acc_ref[...] = jnp.zeros(acc_ref.shape, f32)
    l_ref[...] = jnp.zeros(l_ref.shape, f32)

    def issue(j, buf):
        for h in range(N_HEADS_B):
            buf[h] = qk(j, h)

    def consume(j, buf, near):
        madd = madd_ref[rows(j), :]
        for h in range(N_HEADS_B):
            x = buf[h] + madd
            if near is not None:
                x = x + bias_ref[h, near]
            e = jnp.exp2(x - bounds[h])
            l_ref[h] += fold(e)
            acc_ref[h] += jnp.dot(vt_ref[heads[h], rows(j)], e.astype(bf16),
                                  preferred_element_type=f32)

    n_far = jnp.maximum(qi - 1, 0)
    issue(0, sa_ref)

    def far_pair(i, carry):
        j = 2 * i
        issue(j + 1, sb_ref)
        consume(j, sa_ref, None)
        issue(j + 2, sa_ref)
        consume(j + 1, sb_ref, None)
        return carry

    lax.fori_loop(0, n_far // 2, far_pair, 0)
    j0 = 2 * (n_far // 2)

    @pl.when(n_far % 2 == 1)
    def _():
        issue(j0 + 1, sb_ref)
        consume(j0, sa_ref, None)
        issue(j0 + 2, sa_ref)
        consume(j0 + 1, sb_ref, 1)
        consume(j0 + 2, sa_ref, 0)

    @pl.when((n_far % 2 == 0) & (qi >= 1))
    def _():
        issue(j0 + 1, sb_ref)
        consume(j0, sa_ref, 1)
        consume(j0 + 1, sb_ref, 0)

    @pl.when(qi == 0)
    def _():
        consume(0, sa_ref, 0)

    smallest = jnp.min(jnp.sum(l_ref[...], axis=1))

    @pl.when(jnp.logical_not(smallest > SUM_FLOOR))
    def _():
        acc_ref[...] = jnp.zeros(acc_ref.shape, f32)
        first_row = lax.broadcasted_iota(i32, (V7X_SUBLANES, tq), 0) == 0

        def attend_online(j, near):
            for h in range(N_HEADS_B):
                s = logits(j, h, near)
                m_old = m_ref[h]
                m_new = jnp.maximum(m_old, jnp.max(s, axis=0, keepdims=True))
                alpha = jnp.exp2(m_old - m_new)
                p = jnp.exp2(s - m_new)
                l_new = (alpha * jnp.sum(l_ref[h], axis=0, keepdims=True)
                         + jnp.sum(p, axis=0, keepdims=True))
                l_ref[h] = jnp.where(first_row, l_new, 0.0)
                pv = jnp.dot(vt_ref[heads[h], rows(j)], p.astype(bf16), preferred_element_type=f32)
                acc_ref[h] = alpha * acc_ref[h] + pv
                m_ref[h] = m_new

        m_ref[...] = jnp.full(m_ref.shape, _MASKED, f32)
        l_ref[...] = jnp.zeros(l_ref.shape, f32)

        def far_online(j, carry):
            attend_online(j, None)
            return carry

        lax.fori_loop(0, qi - 1, far_online, 0)

        @pl.when(qi >= 1)
        def _():
            attend_online(qi - 1, 1)

        attend_online(qi, 0)

    for h in range(N_HEADS_B):
        total = jnp.sum(l_ref[h], axis=0, keepdims=True)
        out_ref[:, heads[h]] = (acc_ref[h] / total).T.astype(out_ref.dtype)


def _dsa(bias_max, q, iq, iwt, k, vt, ik, ksq, bias_tabs, *, batch, tq, top_k):
    m, d_b = q.shape
    s = m // batch
    nq = s // tq
    hd = d_b // N_HEADS_B
    tile_rows = lambda bi, qi: (bi * nq + qi, 0)
    batch_rows = lambda bi, qi: (bi, 0)
    return pl.pallas_call(
        functools.partial(_dsa_body, tq=tq, top_k=top_k, seq_bits=int(np.log2(s))),
        grid=(batch, nq),
        in_specs=[
            pl.BlockSpec(memory_space=pltpu.SMEM),
            pl.BlockSpec((tq, d_b), tile_rows),
            pl.BlockSpec((IDX_HEADS, tq, IDX_DIM), lambda bi, qi: (0, bi * nq + qi, 0)),
            pl.BlockSpec((IDX_HEADS, tq), lambda bi, qi: (0, bi * nq + qi)),
            pl.BlockSpec((s, d_b), batch_rows),
            pl.BlockSpec((d_b, s), lambda bi, qi: (0, bi)),
            pl.BlockSpec((s, IDX_DIM), batch_rows),
            pl.BlockSpec((V7X_SUBLANES, s), lambda bi, qi: (0, bi)),
            pl.BlockSpec(bias_tabs.shape, lambda bi, qi: (0, 0, 0, 0)),
        ],
        out_specs=pl.BlockSpec((tq, d_b), tile_rows),
        out_shape=jax.ShapeDtypeStruct((m, d_b), bf16),
        scratch_shapes=[
            pltpu.VMEM((s, tq), f32),
            pltpu.VMEM((KEY_BITS, s // KEY_BITS, tq), i32),
            pltpu.VMEM((V7X_SUBLANES, tq), i32),
            pltpu.VMEM((s, tq), f32),
            pltpu.VMEM((N_HEADS_B, tq, tq), f32),
            pltpu.VMEM((N_HEADS_B, tq, tq), f32),
            pltpu.VMEM((N_HEADS_B, 1, tq), f32),
            pltpu.VMEM((N_HEADS_B, V7X_SUBLANES, tq), f32),
            pltpu.VMEM((N_HEADS_B, hd, tq), f32),
        ],
        compiler_params=pltpu.CompilerParams(
            dimension_semantics=("arbitrary", "arbitrary"), vmem_limit_bytes=V7X_VMEM_LIMIT_BYTES),
        name="dsa",
    )(bias_max, q, iq, iwt, k, vt, ik, ksq, bias_tabs)


def _out_ffn_body(x_ref, a_ref, b_ref, woa_ref, wob_ref, gpm_ref, gpf_ref, wg_ref, wu_ref,
                  wd_ref, gpo_ref, o_ref, *, ff_chunk):
    mix = (jnp.dot(a_ref[...], woa_ref[...], preferred_element_type=f32)
           + jnp.dot(b_ref[...], wob_ref[...], preferred_element_type=f32))
    x1 = x_ref[...] + _rms(mix, gpm_ref[...])
    h = _rms(x1, gpf_ref[...]).astype(bf16)
    d_ff = wg_ref.shape[1]
    f = jnp.zeros(x1.shape, f32)
    for c in range(d_ff // ff_chunk):
        cs = slice(c * ff_chunk, (c + 1) * ff_chunk)
        gate = jnp.dot(h, wg_ref[:, cs], preferred_element_type=f32)
        up = jnp.dot(h, wu_ref[:, cs], preferred_element_type=f32)
        act = (gate * jax.nn.sigmoid(gate) * up).astype(bf16)
        f = f + jnp.dot(act, wd_ref[cs, :], preferred_element_type=f32)
    o_ref[...] = x1 + _rms(f, gpo_ref[...])


def _out_ffn(x2, out_a, out_b, w_oa, w_ob, g_pm, g_pf, w_gate, w_up, w_down, g_po, *, tm, ff_chunk):
    m, d = x2.shape
    d_a, d_b, d_ff = out_a.shape[1], out_b.shape[1], w_gate.shape[1]
    row = lambda i: (i, 0)
    const = lambda i: (0, 0)
    resident = functools.partial(pl.BlockSpec, index_map=const, pipeline_mode=pl.Buffered(1))
    return pl.pallas_call(
        functools.partial(_out_ffn_body, ff_chunk=ff_chunk),
        grid=(m // tm,),
        in_specs=[
            pl.BlockSpec((tm, d), row),
            pl.BlockSpec((tm, d_a), row),
            pl.BlockSpec((tm, d_b), row),
            resident((d_a, d)),
            resident((d_b, d)),
            resident((1, d)),
            resident((1, d)),
            resident((d, d_ff)),
            resident((d, d_ff)),
            resident((d_ff, d)),
            resident((1, d)),
        ],
        out_specs=pl.BlockSpec((tm, d), row),
        out_shape=jax.ShapeDtypeStruct((m, d), f32),
        compiler_params=pltpu.CompilerParams(
            dimension_semantics=("arbitrary",), vmem_limit_bytes=V7X_VMEM_LIMIT_BYTES),
        name="out_ffn",
    )(x2, out_a, out_b, w_oa, w_ob, g_pm, g_pf, w_gate, w_up, w_down, g_po)


def _t5_bucket(rel):
    nb = NUM_BUCKETS // 2
    ret = (rel > 0).astype(jnp.int32) * nb
    n = jnp.abs(rel)
    max_exact = nb // 2
    nf = jnp.maximum(n, 1).astype(jnp.float32)
    large = max_exact + (jnp.log(nf / max_exact) / np.float32(np.log(MAX_DISTANCE / max_exact))
                         * (nb - max_exact)).astype(jnp.int32)
    large = jnp.minimum(large, nb - 1)
    return ret + jnp.where(n < max_exact, n, large)


def _near_bias(rel_bias, tq):
    assert tq + 1 >= MAX_DISTANCE
    rel0 = jnp.arange(tq, dtype=jnp.int32)[:, None] - jnp.arange(tq, dtype=jnp.int32)[None, :]
    buckets = _t5_bucket(jnp.stack([rel0, rel0 - tq])) & (NUM_BUCKETS - 1)
    shifted = (rel_bias - rel_bias[NUM_BUCKETS // 2 - 1][None, :]) * np.float32(LOG2_E)
    tabs = jnp.zeros((rel_bias.shape[1],) + buckets.shape, f32)
    for b in range(NUM_BUCKETS):
        tabs = jnp.where(buckets[None] == b, shifted[b][:, None, None, None], tabs)
    return tabs, jnp.maximum(jnp.max(shifted, axis=0), 0.0)


def kernel(x, g_pre_mix, w_in, sgu_ln_g, sgu_ln_b, sgu_w, sgu_b, rel_bias, w_o, g_post_mix,
           g_pre_ffn, w_gate, w_up, w_down, g_post_ffn):
    b, s, d = x.shape
    depth = w_in.shape[0]
    d_a = d // 2
    d_b = d - d_a
    d_main = 2 * d_a + 3 * d_b + IDX_HEADS * IDX_DIM
    top_k = min(MAX_TOPK, s // 4)
    tq = KEY_BITS * V7X_SUBLANES
    tm = 512
    x2 = x.reshape(b * s, d)
    for l in range(depth):
        w_l = w_in[l]
        w_main = w_l[:, :d_main].astype(bf16)
        w_ik = jnp.pad(w_l[:, d_main:], ((0, 0), (0, V7X_LANES - (IDX_DIM + IDX_HEADS)))).astype(bf16)
        out_a, q, k, vt, iq, ik, iwt, ksq = _proj_sgu(
            x2, g_pre_mix[l][None], w_main, w_ik, sgu_ln_g[l][None], sgu_ln_b[l][None],
            sgu_w[l], jnp.transpose(sgu_b[l]), d_a=d_a, d_b=d_b, tm=tm)
        bias_tabs, bias_max = _near_bias(rel_bias, tq)
        out_b = _dsa(bias_max, q, iq, iwt, k, vt, ik, ksq, bias_tabs, batch=b, tq=tq, top_k=top_k)
        w_ol = w_o[l].astype(bf16)
        x2 = _out_ffn(x2, out_a, out_b, w_ol[:d_a], w_ol[d_a:],
                      g_post_mix[l][None], g_pre_ffn[l][None], w_gate[l].astype(bf16),
                      w_up[l].astype(bf16), w_down[l].astype(bf16), g_post_ffn[l][None],
                      tm=tm, ff_chunk=256)
    return x2.reshape(b, s, d)
```

```python
import functools

import numpy as np
import jax
import jax.numpy as jnp
from jax import lax
from jax.experimental import pallas as pl
from jax.experimental.pallas import tpu as pltpu

CHUNK = 64
SGU_BLOCK = 128
SGU_GROUPS = 4
N_HEADS_B = 4
IDX_HEADS = 8
IDX_DIM = 64
MAX_TOPK = 256
NUM_BUCKETS = 32
MAX_DISTANCE = 128
EPS = 1e-6

V7X_LANES = 128
V7X_SUBLANES = 8
V7X_VMEM_LIMIT_BYTES = 56 * 1024 * 1024

KEY_BITS = 32
GAP_BITS = 30
_INT_MIN = -(2 ** 31)
_MASKED = -(2.0 ** 100)
LOG2_E = float(np.log2(np.e))
SUM_FLOOR = 1e-30

_NT = (((1,), (1,)), ((), ()))

f32 = jnp.float32
bf16 = jnp.bfloat16
i32 = jnp.int32


def _rms(x, g):
    return x * lax.rsqrt(jnp.mean(x * x, axis=-1, keepdims=True) + EPS) * g


def _proj_sgu_body(x_ref, g_ref, w_ref, wik_ref, lng_ref, lnb_ref, sw_ref, sbt_ref,
                   outa_ref, q_ref, k_ref, vt_ref, iq_ref, ik_ref, iwt_ref, ksq_ref, *, d_a, d_b):
    tm = x_ref.shape[0]
    h = _rms(x_ref[...], g_ref[...]).astype(bf16)

    def proj(c0, width):
        return jnp.dot(h, w_ref[:, c0:c0 + width], preferred_element_type=f32)

    u = proj(0, d_a)
    a_v = proj(d_a, d_a)
    c0 = 2 * d_a
    q_ref[...] = (proj(c0, d_b) * np.float32((d_b // N_HEADS_B) ** -0.5 * LOG2_E)).astype(bf16)
    k = proj(c0 + d_b, d_b).astype(bf16)
    k_ref[...] = k
    kt = k.astype(f32).T
    hd = d_b // N_HEADS_B
    ksq = [jnp.sum(kt[a * hd:(a + 1) * hd] ** 2, axis=0, keepdims=True) for a in range(N_HEADS_B)]
    ksq_ref[...] = jnp.concatenate(ksq + [jnp.zeros((V7X_SUBLANES - N_HEADS_B, tm), f32)], axis=0)
    vt_ref[...] = proj(c0 + 2 * d_b, d_b).astype(bf16).T
    iq = (proj(c0 + 3 * d_b, IDX_HEADS * IDX_DIM) * np.float32(IDX_DIM ** -0.5)).astype(bf16)
    for hh in range(IDX_HEADS):
        iq_ref[hh] = iq[:, hh * IDX_DIM:(hh + 1) * IDX_DIM]
    ikw = jnp.dot(h, wik_ref[...], preferred_element_type=f32)
    ik_ref[...] = ikw[:, :IDX_DIM].astype(bf16)
    iwt_ref[...] = ikw.T[IDX_DIM:IDX_DIM + IDX_HEADS, :] * np.float32(IDX_HEADS ** -0.5)

    mu = jnp.mean(a_v, axis=-1, keepdims=True)
    cen = a_v - mu
    var = jnp.mean(cen * cen, axis=-1, keepdims=True)
    vn = (cen * lax.rsqrt(var + EPS) * lng_ref[...] + lnb_ref[...]).astype(bf16)

    row_chunk = lax.broadcasted_iota(i32, (SGU_BLOCK, SGU_BLOCK), 0) // CHUNK
    col_chunk = lax.broadcasted_iota(i32, (SGU_BLOCK, SGU_BLOCK), 1) // CHUNK
    causal = col_chunk <= row_chunk
    gdim = d_a // SGU_GROUPS
    for g in range(SGU_GROUPS):
        w_g = jnp.where(causal, sw_ref[g], 0.0).astype(bf16)
        b_g = sbt_ref[:, g:g + 1]
        for c in range(tm // SGU_BLOCK):
            rows = slice(c * SGU_BLOCK, (c + 1) * SGU_BLOCK)
            cols = slice(g * gdim, (g + 1) * gdim)
            mixed = jnp.dot(w_g, vn[rows, cols], preferred_element_type=f32) + b_g
            outa_ref[rows, cols] = (u[rows, cols] * mixed).astype(bf16)


def _proj_sgu(x2, g_pre, w_main, w_ik, ln_g, ln_b, sgu_w, sgu_bt, *, d_a, d_b, tm):
    m, d = x2.shape
    const = lambda i: (0, 0)
    row = lambda i: (i, 0)
    col = lambda i: (0, i)
    return pl.pallas_call(
        functools.partial(_proj_sgu_body, d_a=d_a, d_b=d_b),
        grid=(m // tm,),
        in_specs=[
            pl.BlockSpec((tm, d), row),
            pl.BlockSpec((1, d), const),
            pl.BlockSpec(w_main.shape, const),
            pl.BlockSpec(w_ik.shape, const),
            pl.BlockSpec((1, d_a), const),
            pl.BlockSpec((1, d_a), const),
            pl.BlockSpec(sgu_w.shape, lambda i: (0, 0, 0)),
            pl.BlockSpec(sgu_bt.shape, const),
        ],
        out_specs=[
            pl.BlockSpec((tm, d_a), row),
            pl.BlockSpec((tm, d_b), row),
            pl.BlockSpec((tm, d_b), row),
            pl.BlockSpec((d_b, tm), col),
            pl.BlockSpec((IDX_HEADS, tm, IDX_DIM), lambda i: (0, i, 0)),
            pl.BlockSpec((tm, IDX_DIM), row),
            pl.BlockSpec((IDX_HEADS, tm), col),
            pl.BlockSpec((V7X_SUBLANES, tm), col),
        ],
        out_shape=[
            jax.ShapeDtypeStruct((m, d_a), bf16),
            jax.ShapeDtypeStruct((m, d_b), bf16),
            jax.ShapeDtypeStruct((m, d_b), bf16),
            jax.ShapeDtypeStruct((d_b, m), bf16),
            jax.ShapeDtypeStruct((IDX_HEADS, m, IDX_DIM), bf16),
            jax.ShapeDtypeStruct((m, IDX_DIM), bf16),
            jax.ShapeDtypeStruct((IDX_HEADS, m), f32),
            jax.ShapeDtypeStruct((V7X_SUBLANES, m), f32),
        ],
        compiler_params=pltpu.CompilerParams(
            dimension_semantics=("arbitrary",), vmem_limit_bytes=V7X_VMEM_LIMIT_BYTES),
        name="proj_sgu",
    )(x2, g_pre, w_main, w_ik, ln_g, ln_b, sgu_w, sgu_bt)


def _bit_planes(words):
    a = list(words)
    j, m = 16, 0x0000FFFF
    while j:
        k = 0
        while k < KEY_BITS:
            t = (a[k] ^ lax.shift_right_logical(a[k + j], i32(j))) & i32(m)
            a[k] = a[k] ^ t
            a[k + j] = a[k + j] ^ lax.shift_left(t, i32(j))
            k = (k + j + 1) & ~j
        j >>= 1
        m = (m ^ (m << j)) & 0xFFFFFFFF
    return a


def _dsa_body(bmax_ref, q_ref, iq_ref, iwt_ref, k_ref, vt_ref, ik_ref, ksq_ref, bias_ref, out_ref,
              sc_ref, planes_ref, aux_ref, madd_ref, sa_ref, sb_ref, m_ref, l_ref, acc_ref,
              *, tq, top_k, seq_bits):
    tk = tq
    group = KEY_BITS * V7X_SUBLANES
    assert tk == group
    hd = q_ref.shape[1] // N_HEADS_B
    qi = pl.program_id(1)
    n_tiles = qi + 1
    int_min = i32(_INT_MIN)

    def rows(j):
        return pl.ds(pl.multiple_of(j * tk, tk), tk)

    @pl.when((pl.program_id(0) == 0) & (qi == 0))
    def _():
        planes_ref[...] = jnp.zeros(planes_ref.shape, i32)

    def to_key(x):
        bits = lax.bitcast_convert_type(x, i32)
        return bits ^ ((bits >> 31) & i32(0x7FFFFFFF))

    def from_key(key):
        return lax.bitcast_convert_type(key ^ ((key >> 31) & i32(0x7FFFFFFF)), f32)

    def score_tile(j, diagonal):
        ik_j = ik_ref[rows(j), :]
        sc = jnp.zeros((tk, tq), f32)
        for h in range(IDX_HEADS):
            d = lax.dot_general(ik_j, iq_ref[h], _NT, preferred_element_type=f32)
            sc = sc + iwt_ref[h:h + 1, :] * jnp.maximum(d, 0.0)
        key = to_key(sc)
        if diagonal:
            key_chunk = lax.broadcasted_iota(i32, (tk, tq), 0) // CHUNK
            qry_chunk = lax.broadcasted_iota(i32, (tk, tq), 1) // CHUNK
            visible = key_chunk <= qry_chunk
            sc = jnp.where(visible, sc, -jnp.inf)
            key = jnp.where(visible, key, int_min)
        sc_ref[rows(j), :] = sc
        return key

    def plane_tile(j, key):
        prow = pl.ds(pl.multiple_of(j * V7X_SUBLANES, V7X_SUBLANES), V7X_SUBLANES)
        for lt in range(tq // V7X_LANES):
            lanes = slice(lt * V7X_LANES, (lt + 1) * V7X_LANES)
            planes = _bit_planes([key[V7X_SUBLANES * i:V7X_SUBLANES * (i + 1), lanes]
                                  for i in range(KEY_BITS)])
            planes_ref[0, prow, lanes] = ~planes[0]
            for p in range(1, KEY_BITS):
                planes_ref[p, prow, lanes] = planes[p]

    def far_scores(j, prev):
        key = score_tile(j, False)
        plane_tile(j - 1, prev)
        return key

    prev = lax.fori_loop(1, qi, far_scores, score_tile(0, False))
    last = score_tile(qi, True)
    plane_tile(jnp.maximum(qi - 1, 0), prev)
    plane_tile(qi, last)

    n_prow = planes_ref.shape[1]
    zero_row = jnp.zeros((1, tq), i32)
    live = lax.broadcasted_iota(i32, (n_prow, tq), 0) < n_tiles * V7X_SUBLANES
    alive, n_above, thr_bits = jnp.where(live, i32(-1), 0), zero_row, zero_row
    for p in range(KEY_BITS):
        ones = alive & planes_ref[p]
        n_ones = jnp.sum(lax.population_count(ones), axis=0, keepdims=True)
        take = (n_above + n_ones) >= top_k
        alive = jnp.where(take, ones, alive ^ ones)
        n_above = jnp.where(take, n_above, n_above + n_ones)
        thr_bits = thr_bits | jnp.where(take, i32(1 << (31 - p) if p else _INT_MIN), 0)
    thr_key = thr_bits ^ int_min
    n_equal = jnp.sum(lax.population_count(alive), axis=0, keepdims=True)
    found = thr_key > int_min
    lowest = jnp.float32(np.finfo(np.float32).min)
    thr = jnp.where(found, from_key(thr_key), lowest)
    tie = found & (n_above + n_equal > top_k)

    def fold(x):
        return x.reshape(tk // V7X_SUBLANES, V7X_SUBLANES, tq).sum(axis=0)

    def make_mask(thr_f):
        def body(j, acc):
            m = jnp.where(sc_ref[rows(j), :] >= thr_f, 0.0, _MASKED).astype(f32)
            madd_ref[rows(j), :] = m
            return acc + fold(m)
        acc = lax.fori_loop(0, n_tiles, body, jnp.zeros((V7X_SUBLANES, tq), f32))
        n_masked = jnp.sum(acc, axis=0, keepdims=True) * np.float32(1.0 / _MASKED)
        return n_tiles * tk - n_masked.astype(i32)

    def key_idx(j):
        return j * tk + lax.broadcasted_iota(i32, (tk, tq), 0)

    def count_where(pred):
        def body(j, acc):
            return acc + fold(pred(sc_ref[rows(j), :], j).astype(i32))
        acc = lax.fori_loop(0, n_tiles, body, jnp.zeros((V7X_SUBLANES, tq), i32))
        return jnp.sum(acc, axis=0, keepdims=True)

    def last_kept(n_keep, count_before):
        def index_pass(p, last):
            cand = last + lax.shift_left(i32(1), seq_bits - 1 - p)
            return jnp.where(count_before(cand) < n_keep, cand, last)
        return lax.fori_loop(0, seq_bits, index_pass, zero_row)

    def remask(base, delta, tied, last):
        def body(j, acc):
            above = sc_ref[rows(j), :] - base
            keep = (above > delta) | ((above == delta) & (jnp.logical_not(tied) | (key_idx(j) <= last)))
            m = jnp.where(keep, 0.0, _MASKED).astype(f32)
            madd_ref[rows(j), :] = m
            return acc + fold(m)
        acc = lax.fori_loop(0, n_tiles, body, jnp.zeros((V7X_SUBLANES, tq), f32))
        n_masked = jnp.sum(acc, axis=0, keepdims=True) * np.float32(1.0 / _MASKED)
        return n_tiles * tk - n_masked.astype(i32)

    qry = qi * tq + lax.broadcasted_iota(i32, (1, tq), 1)
    n_visible = (qry // CHUNK + 1) * CHUNK
    n_kept = make_mask(thr)

    agree = jnp.max((n_kept != jnp.where(found, n_above + n_equal, n_visible)).astype(i32)) == 0
    aux_ref[...] = jnp.broadcast_to(n_kept, aux_ref.shape)

    @pl.when(agree & (jnp.max(tie.astype(i32)) > 0))
    def _():
        word = lax.broadcasted_iota(i32, (n_prow, tq), 0)
        log2_sublanes = V7X_SUBLANES.bit_length() - 1
        first_key = (word >> log2_sublanes) * tk + (word & (V7X_SUBLANES - 1))

        def tied_before(cand):
            n_bits = jnp.clip((cand - first_key + (V7X_SUBLANES - 1)) >> log2_sublanes, 0, KEY_BITS)
            top_bits = jnp.where(n_bits <= 0, i32(0),
                                 lax.shift_left(i32(-1), jnp.minimum(KEY_BITS - n_bits, KEY_BITS - 1)))
            return jnp.sum(lax.population_count(alive & top_bits), axis=0, keepdims=True)

        last = last_kept(top_k - n_above, tied_before)
        aux_ref[...] = jnp.broadcast_to(remask(thr, jnp.float32(0.0), tie, last), aux_ref.shape)

    settled = agree & (jnp.max((aux_ref[0:1, :] != jnp.where(found, top_k, n_visible)).astype(i32)) == 0)

    @pl.when(jnp.logical_not(settled))
    def _():
        def value_pass(p, prefix):
            cand = prefix + lax.shift_left(i32(1), 31 - p)
            cand_f = from_key(cand)
            cnt = count_where(lambda st, j: st >= cand_f)
            return jnp.where(cnt >= top_k, cand, prefix)

        base_key = lax.fori_loop(0, KEY_BITS, value_pass, jnp.full((1, tq), _INT_MIN, i32))
        short = (n_visible < top_k) | (base_key == int_min)
        base = jnp.where(short, lowest, from_key(base_key))
        gap = from_key(base_key + 1) - base

        def gap_pass(p, delta):
            cand = delta + gap / lax.shift_left(i32(1), p + 1).astype(f32)
            cnt = count_where(lambda st, j: (st - base) >= cand)
            return jnp.where(cnt >= top_k, cand, delta)

        delta = lax.fori_loop(0, GAP_BITS, gap_pass, jnp.zeros((1, tq), f32))
        delta = jnp.where(short, 0.0, delta)
        n_keep = top_k - count_where(lambda st, j: (st - base) > delta)
        tied = (count_where(lambda st, j: (st - base) >= delta) > top_k) & jnp.logical_not(short)
        last = last_kept(n_keep, lambda cand: count_where(
            lambda st, j: ((st - base) == delta) & (key_idx(j) < cand)))
        remask(base, delta, tied, last)

    heads = [slice(h * hd, (h + 1) * hd) for h in range(N_HEADS_B)]

    bounds = []
    for h, hs in enumerate(heads):
        qt = q_ref[:, hs].astype(f32).T
        longest_key = jnp.sqrt(jnp.max(ksq_ref[h:h + 1, :]))
        bounds.append(jnp.sqrt(jnp.sum(qt * qt, axis=0, keepdims=True)) * longest_key + bmax_ref[h])

    def qk(j, h):
        return lax.dot_general(k_ref[rows(j), heads[h]], q_ref[:, heads[h]], _NT,
                               preferred_element_type=f32)

    def logits(j, h, near):
        s = qk(j, h) + madd_ref[rows(j), :]
        return s if near is None else s + bias_ref[h, near]

    acc_ref[...] = jnp.zeros(acc_ref.shape, f32)
    l_ref[...] = jnp.zeros(l_ref.shape, f32)

    def issue(j, buf):
        for h in range(N_HEADS_B):
            buf[h] = qk(j, h)

    def consume(j, buf, near):
        madd = madd_ref[rows(j), :]
        for h in range(N_HEADS_B):
            x = buf[h] + madd
            if near is not None:
                x = x + bias_ref[h, near]
            e = jnp.exp2(x - bounds[h])
            l_ref[h] += fold(e)
            acc_ref[h] += jnp.dot(vt_ref[heads[h], rows(j)], e.astype(bf16),
                                  preferred_element_type=f32)

    n_far = jnp.maximum(qi - 1, 0)
    issue(0, sa_ref)

    def far_pair(i, carry):
        j = 2 * i
        issue(j + 1, sb_ref)
        consume(j, sa_ref, None)
        issue(j + 2, sa_ref)
        consume(j + 1, sb_ref, None)
        return carry

    lax.fori_loop(0, n_far // 2, far_pair, 0)
    j0 = 2 * (n_far // 2)

    @pl.when(n_far % 2 == 1)
    def _():
        issue(j0 + 1, sb_ref)
        consume(j0, sa_ref, None)
        issue(j0 + 2, sa_ref)
        consume(j0 + 1, sb_ref, 1)
        consume(j0 + 2, sa_ref, 0)

    @pl.when((n_far % 2 == 0) & (qi >= 1))
    def _():
        issue(j0 + 1, sb_ref)
        consume(j0, sa_ref, 1)
        consume(j0 + 1, sb_ref, 0)

    @pl.when(qi == 0)
    def _():
        consume(0, sa_ref, 0)

    smallest = jnp.min(jnp.sum(l_ref[...], axis=1))

    @pl.when(jnp.logical_not(smallest > SUM_FLOOR))
    def _():
        acc_ref[...] = jnp.zeros(acc_ref.shape, f32)
        first_row = lax.broadcasted_iota(i32, (V7X_SUBLANES, tq), 0) == 0

        def attend_online(j, near):
            for h in range(N_HEADS_B):
                s = logits(j, h, near)
                m_old = m_ref[h]
                m_new = jnp.maximum(m_old, jnp.max(s, axis=0, keepdims=True))
                alpha = jnp.exp2(m_old - m_new)
                p = jnp.exp2(s - m_new)
                l_new = (alpha * jnp.sum(l_ref[h], axis=0, keepdims=True)
                         + jnp.sum(p, axis=0, keepdims=True))
                l_ref[h] = jnp.where(first_row, l_new, 0.0)
                pv = jnp.dot(vt_ref[heads[h], rows(j)], p.astype(bf16), preferred_element_type=f32)
                acc_ref[h] = alpha * acc_ref[h] + pv
                m_ref[h] = m_new

        m_ref[...] = jnp.full(m_ref.shape, _MASKED, f32)
        l_ref[...] = jnp.zeros(l_ref.shape, f32)

        def far_online(j, carry):
            attend_online(j, None)
            return carry

        lax.fori_loop(0, qi - 1, far_online, 0)

        @pl.when(qi >= 1)
        def _():
            attend_online(qi - 1, 1)

        attend_online(qi, 0)

    for h in range(N_HEADS_B):
        total = jnp.sum(l_ref[h], axis=0, keepdims=True)
        out_ref[:, heads[h]] = (acc_ref[h] / total).T.astype(out_ref.dtype)


def _dsa(bias_max, q, iq, iwt, k, vt, ik, ksq, bias_tabs, *, batch, tq, top_k):
    m, d_b = q.shape
    s = m // batch
    nq = s // tq
    hd = d_b // N_HEADS_B
    tile_rows = lambda bi, qi: (bi * nq + qi, 0)
    batch_rows = lambda bi, qi: (bi, 0)
    return pl.pallas_call(
        functools.partial(_dsa_body, tq=tq, top_k=top_k, seq_bits=int(np.log2(s))),
        grid=(batch, nq),
        in_specs=[
            pl.BlockSpec(memory_space=pltpu.SMEM),
            pl.BlockSpec((tq, d_b), tile_rows),
            pl.BlockSpec((IDX_HEADS, tq, IDX_DIM), lambda bi, qi: (0, bi * nq + qi, 0)),
            pl.BlockSpec((IDX_HEADS, tq), lambda bi, qi: (0, bi * nq + qi)),
            pl.BlockSpec((s, d_b), batch_rows),
            pl.BlockSpec((d_b, s), lambda bi, qi: (0, bi)),
            pl.BlockSpec((s, IDX_DIM), batch_rows),
            pl.BlockSpec((V7X_SUBLANES, s), lambda bi, qi: (0, bi)),
            pl.BlockSpec(bias_tabs.shape, lambda bi, qi: (0, 0, 0, 0)),
        ],
        out_specs=pl.BlockSpec((tq, d_b), tile_rows),
        out_shape=jax.ShapeDtypeStruct((m, d_b), bf16),
        scratch_shapes=[
            pltpu.VMEM((s, tq), f32),
            pltpu.VMEM((KEY_BITS, s // KEY_BITS, tq), i32),
            pltpu.VMEM((V7X_SUBLANES, tq), i32),
            pltpu.VMEM((s, tq), f32),
            pltpu.VMEM((N_HEADS_B, tq, tq), f32),
            pltpu.VMEM((N_HEADS_B, tq, tq), f32),
            pltpu.VMEM((N_HEADS_B, 1, tq), f32),
            pltpu.VMEM((N_HEADS_B, V7X_SUBLANES, tq), f32),
            pltpu.VMEM((N_HEADS_B, hd, tq), f32),
        ],
        compiler_params=pltpu.CompilerParams(
            dimension_semantics=("arbitrary", "arbitrary"), vmem_limit_bytes=V7X_VMEM_LIMIT_BYTES),
        name="dsa",
    )(bias_max, q, iq, iwt, k, vt, ik, ksq, bias_tabs)


def _out_ffn_body(x_ref, a_ref, b_ref, woa_ref, wob_ref, gpm_ref, gpf_ref, wg_ref, wu_ref,
                  wd_ref, gpo_ref, o_ref, *, ff_chunk):
    mix = (jnp.dot(a_ref[...], woa_ref[...], preferred_element_type=f32)
           + jnp.dot(b_ref[...], wob_ref[...], preferred_element_type=f32))
    x1 = x_ref[...] + _rms(mix, gpm_ref[...])
    h = _rms(x1, gpf_ref[...]).astype(bf16)
    d_ff = wg_ref.shape[1]
    f = jnp.zeros(x1.shape, f32)
    for c in range(d_ff // ff_chunk):
        cs = slice(c * ff_chunk, (c + 1) * ff_chunk)
        gate = jnp.dot(h, wg_ref[:, cs], preferred_element_type=f32)
        up = jnp.dot(h, wu_ref[:, cs], preferred_element_type=f32)
        act = (gate * jax.nn.sigmoid(gate) * up).astype(bf16)
        f = f + jnp.dot(act, wd_ref[cs, :], preferred_element_type=f32)
    o_ref[...] = x1 + _rms(f, gpo_ref[...])


def _out_ffn(x2, out_a, out_b, w_oa, w_ob, g_pm, g_pf, w_gate, w_up, w_down, g_po, *, tm, ff_chunk):
    m, d = x2.shape
    d_a, d_b, d_ff = out_a.shape[1], out_b.shape[1], w_gate.shape[1]
    row = lambda i: (i, 0)
    const = lambda i: (0, 0)
    resident = functools.partial(pl.BlockSpec, index_map=const, pipeline_mode=pl.Buffered(1))
    return pl.pallas_call(
        functools.partial(_out_ffn_body, ff_chunk=ff_chunk),
        grid=(m // tm,),
        in_specs=[
            pl.BlockSpec((tm, d), row),
            pl.BlockSpec((tm, d_a), row),
            pl.BlockSpec((tm, d_b), row),
            resident((d_a, d)),
            resident((d_b, d)),
            resident((1, d)),
            resident((1, d)),
            resident((d, d_ff)),
            resident((d, d_ff)),
            resident((d_ff, d)),
            resident((1, d)),
        ],
        out_specs=pl.BlockSpec((tm, d), row),
        out_shape=jax.ShapeDtypeStruct((m, d), f32),
        compiler_params=pltpu.CompilerParams(
            dimension_semantics=("arbitrary",), vmem_limit_bytes=V7X_VMEM_LIMIT_BYTES),
        name="out_ffn",
    )(x2, out_a, out_b, w_oa, w_ob, g_pm, g_pf, w_gate, w_up, w_down, g_po)


def _t5_bucket(rel):
    nb = NUM_BUCKETS // 2
    ret = (rel > 0).astype(jnp.int32) * nb
    n = jnp.abs(rel)
    max_exact = nb // 2
    nf = jnp.maximum(n, 1).astype(jnp.float32)
    large = max_exact + (jnp.log(nf / max_exact) / np.float32(np.log(MAX_DISTANCE / max_exact))
                         * (nb - max_exact)).astype(jnp.int32)
    large = jnp.minimum(large, nb - 1)
    return ret + jnp.where(n < max_exact, n, large)


def _near_bias(rel_bias, tq):
    assert tq + 1 >= MAX_DISTANCE
    rel0 = jnp.arange(tq, dtype=jnp.int32)[:, None] - jnp.arange(tq, dtype=jnp.int32)[None, :]
    buckets = _t5_bucket(jnp.stack([rel0, rel0 - tq])) & (NUM_BUCKETS - 1)
    shifted = (rel_bias - rel_bias[NUM_BUCKETS // 2 - 1][None, :]) * np.float32(LOG2_E)
    tabs = jnp.zeros((rel_bias.shape[1],) + buckets.shape, f32)
    for b in range(NUM_BUCKETS):
        tabs = jnp.where(buckets[None] == b, shifted[b][:, None, None, None], tabs)
    return tabs, jnp.maximum(jnp.max(shifted, axis=0), 0.0)


def kernel(x, g_pre_mix, w_in, sgu_ln_g, sgu_ln_b, sgu_w, sgu_b, rel_bias, w_o, g_post_mix,
           g_pre_ffn, w_gate, w_up, w_down, g_post_ffn):
    b, s, d = x.shape
    depth = w_in.shape[0]
    d_a = d // 2
    d_b = d - d_a
    d_main = 2 * d_a + 3 * d_b + IDX_HEADS * IDX_DIM
    top_k = min(MAX_TOPK, s // 4)
    tq = KEY_BITS * V7X_SUBLANES
    tm = 512
    x2 = x.reshape(b * s, d)
    for l in range(depth):
        w_l = w_in[l]
        w_main = w_l[:, :d_main].astype(bf16)
        w_ik = jnp.pad(w_l[:, d_main:], ((0, 0), (0, V7X_LANES - (IDX_DIM + IDX_HEADS)))).astype(bf16)
        out_a, q, k, vt, iq, ik, iwt, ksq = _proj_sgu(
            x2, g_pre_mix[l][None], w_main, w_ik, sgu_ln_g[l][None], sgu_ln_b[l][None],
            sgu_w[l], jnp.transpose(sgu_b[l]), d_a=d_a, d_b=d_b, tm=tm)
        bias_tabs, bias_max = _near_bias(rel_bias, tq)
        out_b = _dsa(bias_max, q, iq, iwt, k, vt, ik, ksq, bias_tabs, batch=b, tq=tq, top_k=top_k)
        w_ol = w_o[l].astype(bf16)
        x2 = _out_ffn(x2, out_a, out_b, w_ol[:d_a], w_ol[d_a:],
                      g_post_mix[l][None], g_pre_ffn[l][None], w_gate[l].astype(bf16),
                      w_up[l].astype(bf16), w_down[l].astype(bf16), g_post_ffn[l][None],
                      tm=tm, ff_chunk=256)
    return x2.reshape(b, s, d)
```

```python
import functools

import numpy as np
import jax
import jax.numpy as jnp
from jax import lax
from jax.experimental import pallas as pl
from jax.experimental.pallas import tpu as pltpu

CHUNK = 64
SGU_BLOCK = 128
SGU_GROUPS = 4
N_HEADS_B = 4
IDX_HEADS = 8
IDX_DIM = 64
MAX_TOPK = 256
NUM_BUCKETS = 32
MAX_DISTANCE = 128
EPS = 1e-6

V7X_LANES = 128
V7X_SUBLANES = 8
V7X_VMEM_LIMIT_BYTES = 56 * 1024 * 1024

TOKEN_TILE = 512
FF_CHUNK = 256
KEY_BITS = 32
GAP_BITS = 30
_INT_MIN = -(2 ** 31)
_INT_MAX = 2 ** 31 - 1
_MASKED = -(2.0 ** 100)
LOG2_E = float(np.log2(np.e))
SUM_FLOOR = 1e-30

_NT = (((1,), (1,)), ((), ()))

f32 = jnp.float32
bf16 = jnp.bfloat16
i32 = jnp.int32


def _rms(x, g):
    return x * lax.rsqrt(jnp.mean(x * x, axis=-1, keepdims=True) + EPS) * g


def _proj_sgu_body(x_ref, g_ref, w_ref, wik_ref, lng_ref, lnb_ref, sw_ref, sbt_ref,
                   outa_ref, q_ref, k_ref, vt_ref, iq_ref, ik_ref, iwt_ref, ksq_ref, *, d_a, d_b):
    tm = x_ref.shape[0]
    h = _rms(x_ref[...], g_ref[...]).astype(bf16)

    def proj(c0, width):
        return jnp.dot(h, w_ref[:, c0:c0 + width], preferred_element_type=f32)

    u = proj(0, d_a)
    a_v = proj(d_a, d_a)
    c0 = 2 * d_a
    q_ref[...] = (proj(c0, d_b) * np.float32((d_b // N_HEADS_B) ** -0.5 * LOG2_E)).astype(bf16)
    k = proj(c0 + d_b, d_b).astype(bf16)
    k_ref[...] = k
    kt = k.astype(f32).T
    hd = d_b // N_HEADS_B
    ksq = [jnp.sum(kt[a * hd:(a + 1) * hd] ** 2, axis=0, keepdims=True) for a in range(N_HEADS_B)]
    ksq_ref[...] = jnp.concatenate(ksq + [jnp.zeros((V7X_SUBLANES - N_HEADS_B, tm), f32)], axis=0)
    vt_ref[...] = proj(c0 + 2 * d_b, d_b).astype(bf16).T
    iq = (proj(c0 + 3 * d_b, IDX_HEADS * IDX_DIM) * np.float32(IDX_DIM ** -0.5)).astype(bf16)
    for hh in range(IDX_HEADS):
        iq_ref[hh] = iq[:, hh * IDX_DIM:(hh + 1) * IDX_DIM]
    ikw = jnp.dot(h, wik_ref[...], preferred_element_type=f32)
    ik_ref[...] = ikw[:, :IDX_DIM].astype(bf16)
    iwt_ref[...] = ikw.T[IDX_DIM:IDX_DIM + IDX_HEADS, :] * np.float32(IDX_HEADS ** -0.5)

    mu = jnp.mean(a_v, axis=-1, keepdims=True)
    cen = a_v - mu
    var = jnp.mean(cen * cen, axis=-1, keepdims=True)
    vn = (cen * lax.rsqrt(var + EPS) * lng_ref[...] + lnb_ref[...]).astype(bf16)

    row_chunk = lax.broadcasted_iota(i32, (SGU_BLOCK, SGU_BLOCK), 0) // CHUNK
    col_chunk = lax.broadcasted_iota(i32, (SGU_BLOCK, SGU_BLOCK), 1) // CHUNK
    causal = col_chunk <= row_chunk
    gdim = d_a // SGU_GROUPS
    for g in range(SGU_GROUPS):
        w_g = jnp.where(causal, sw_ref[g], 0.0).astype(bf16)
        b_g = sbt_ref[:, g:g + 1]
        for c in range(tm // SGU_BLOCK):
            rows = slice(c * SGU_BLOCK, (c + 1) * SGU_BLOCK)
            cols = slice(g * gdim, (g + 1) * gdim)
            mixed = jnp.dot(w_g, vn[rows, cols], preferred_element_type=f32) + b_g
            outa_ref[rows, cols] = (u[rows, cols] * mixed).astype(bf16)


def _proj_sgu(x2, g_pre, w_main, w_ik, ln_g, ln_b, sgu_w, sgu_bt, *, d_a, d_b, tm):
    m, d = x2.shape
    const = lambda i: (0, 0)
    row = lambda i: (i, 0)
    col = lambda i: (0, i)
    return pl.pallas_call(
        functools.partial(_proj_sgu_body, d_a=d_a, d_b=d_b),
        grid=(m // tm,),
        in_specs=[
            pl.BlockSpec((tm, d), row),
            pl.BlockSpec((1, d), const),
            pl.BlockSpec(w_main.shape, const),
            pl.BlockSpec(w_ik.shape, const),
            pl.BlockSpec((1, d_a), const),
            pl.BlockSpec((1, d_a), const),
            pl.BlockSpec(sgu_w.shape, lambda i: (0, 0, 0)),
            pl.BlockSpec(sgu_bt.shape, const),
        ],
        out_specs=[
            pl.BlockSpec((tm, d_a), row),
            pl.BlockSpec((tm, d_b), row),
            pl.BlockSpec((tm, d_b), row),
            pl.BlockSpec((d_b, tm), col),
            pl.BlockSpec((IDX_HEADS, tm, IDX_DIM), lambda i: (0, i, 0)),
            pl.BlockSpec((tm, IDX_DIM), row),
            pl.BlockSpec((IDX_HEADS, tm), col),
            pl.BlockSpec((V7X_SUBLANES, tm), col),
        ],
        out_shape=[
            jax.ShapeDtypeStruct((m, d_a), bf16),
            jax.ShapeDtypeStruct((m, d_b), bf16),
            jax.ShapeDtypeStruct((m, d_b), bf16),
            jax.ShapeDtypeStruct((d_b, m), bf16),
            jax.ShapeDtypeStruct((IDX_HEADS, m, IDX_DIM), bf16),
            jax.ShapeDtypeStruct((m, IDX_DIM), bf16),
            jax.ShapeDtypeStruct((IDX_HEADS, m), f32),
            jax.ShapeDtypeStruct((V7X_SUBLANES, m), f32),
        ],
        compiler_params=pltpu.CompilerParams(
            dimension_semantics=("arbitrary",), vmem_limit_bytes=V7X_VMEM_LIMIT_BYTES),
        name="proj_sgu",
    )(x2, g_pre, w_main, w_ik, ln_g, ln_b, sgu_w, sgu_bt)


def _bit_planes(words):
    a = list(words)
    j, m = 16, 0x0000FFFF
    while j:
        k = 0
        while k < KEY_BITS:
            t = (a[k] ^ lax.shift_right_logical(a[k + j], i32(j))) & i32(m)
            a[k] = a[k] ^ t
            a[k + j] = a[k + j] ^ lax.shift_left(t, i32(j))
            k = (k + j + 1) & ~j
        j >>= 1
        m = (m ^ (m << j)) & 0xFFFFFFFF
    return a


def _count_bits(words):
    rows = V7X_SUBLANES
    level = [words[r:r + rows] for r in range(0, words.shape[0], rows)]
    total, weight = None, 0
    while level:
        carries = []
        while len(level) >= 3:
            a, b, c = level.pop(), level.pop(), level.pop()
            u = a ^ b
            level.append(u ^ c)
            carries.append((a & b) | (u & c))
        if len(level) == 2:
            a, b = level.pop(), level.pop()
            level.append(a ^ b)
            carries.append(a & b)
        count = lax.population_count(level[0]) << weight
        total = count if total is None else total + count
        level, weight = carries, weight + 1
    return jnp.sum(total, axis=0, keepdims=True)


def _dsa_body(bmax_ref, q_ref, iq_ref, iwt_ref, k_ref, vt_ref, ik_ref, ksq_ref, bias_ref, out_ref,
              sc_ref, planes_ref, aux_ref, madd_ref, sa_ref, sb_ref, m_ref, l_ref, acc_ref,
              *, tq, top_k, seq_bits):
    tk = tq
    assert tk == KEY_BITS * V7X_SUBLANES
    hd = q_ref.shape[1] // N_HEADS_B
    qi = pl.program_id(1)
    n_tiles = qi + 1
    int_min = i32(_INT_MIN)

    def rows(j):
        return pl.ds(pl.multiple_of(j * tk, tk), tk)

    @pl.when((pl.program_id(0) == 0) & (qi == 0))
    def _():
        planes_ref[...] = jnp.zeros(planes_ref.shape, i32)

    def to_key(x):
        bits = lax.bitcast_convert_type(x, i32)
        return bits ^ ((bits >> (KEY_BITS - 1)) & i32(_INT_MAX))

    def from_key(key):
        return lax.bitcast_convert_type(key ^ ((key >> (KEY_BITS - 1)) & i32(_INT_MAX)), f32)

    def score_tile(j, diagonal):
        ik_j = ik_ref[rows(j), :]
        sc = jnp.zeros((tk, tq), f32)
        for h in range(IDX_HEADS):
            d = lax.dot_general(ik_j, iq_ref[h], _NT, preferred_element_type=f32)
            sc = sc + iwt_ref[h:h + 1, :] * jnp.maximum(d, 0.0)
        key = to_key(sc)
        if diagonal:
            key_chunk = lax.broadcasted_iota(i32, (tk, tq), 0) // CHUNK
            qry_chunk = lax.broadcasted_iota(i32, (tk, tq), 1) // CHUNK
            visible = key_chunk <= qry_chunk
            sc = jnp.where(visible, sc, -jnp.inf)
            key = jnp.where(visible, key, int_min)
        sc_ref[rows(j), :] = sc
        return key

    def plane_tile(j, key):
        prow = pl.ds(pl.multiple_of(j * V7X_SUBLANES, V7X_SUBLANES), V7X_SUBLANES)
        for lt in range(tq // V7X_LANES):
            lanes = slice(lt * V7X_LANES, (lt + 1) * V7X_LANES)
            planes = _bit_planes([key[V7X_SUBLANES * i:V7X_SUBLANES * (i + 1), lanes]
                                  for i in range(KEY_BITS)])
            planes_ref[0, prow, lanes] = ~planes[0]
            for p in range(1, KEY_BITS):
                planes_ref[p, prow, lanes] = planes[p]

    def far_scores(j, prev):
        key = score_tile(j, False)
        plane_tile(j - 1, prev)
        return key

    prev = lax.fori_loop(1, qi, far_scores, score_tile(0, False))
    last = score_tile(qi, True)
    plane_tile(jnp.maximum(qi - 1, 0), prev)
    plane_tile(qi, last)

    n_prow = planes_ref.shape[1]
    zero_row = jnp.zeros((1, tq), i32)
    live = lax.broadcasted_iota(i32, (n_prow, tq), 0) < n_tiles * V7X_SUBLANES
    alive, n_above, thr_bits = jnp.where(live, i32(-1), 0), zero_row, zero_row
    for p in range(KEY_BITS):
        ones = alive & planes_ref[p]
        n_ones = _count_bits(ones)
        take = (n_above + n_ones) >= top_k
        alive = jnp.where(take, ones, alive ^ ones)
        n_above = jnp.where(take, n_above, n_above + n_ones)
        thr_bits = thr_bits | jnp.where(take, i32(1 << (KEY_BITS - 1 - p) if p else _INT_MIN), 0)
    thr_key = thr_bits ^ int_min
    n_equal = _count_bits(alive)
    found = thr_key > int_min
    lowest = jnp.float32(np.finfo(np.float32).min)
    thr = jnp.where(found, from_key(thr_key), lowest)
    tie = found & (n_above + n_equal > top_k)

    def fold(x):
        return x.reshape(tk // V7X_SUBLANES, V7X_SUBLANES, tq).sum(axis=0)

    def kept(mask_sums):
        n_masked = jnp.sum(mask_sums, axis=0, keepdims=True) * np.float32(1.0 / _MASKED)
        return n_tiles * tk - n_masked.astype(i32)

    def make_mask(thr_f):
        def body(j, acc):
            m = jnp.where(sc_ref[rows(j), :] >= thr_f, 0.0, _MASKED).astype(f32)
            madd_ref[rows(j), :] = m
            return acc + fold(m)
        return kept(lax.fori_loop(0, n_tiles, body, jnp.zeros((V7X_SUBLANES, tq), f32)))

    def key_idx(j):
        return j * tk + lax.broadcasted_iota(i32, (tk, tq), 0)

    def count_where(pred):
        def body(j, acc):
            return acc + fold(pred(sc_ref[rows(j), :], j).astype(i32))
        acc = lax.fori_loop(0, n_tiles, body, jnp.zeros((V7X_SUBLANES, tq), i32))
        return jnp.sum(acc, axis=0, keepdims=True)

    def last_kept(n_keep, count_before):
        def index_pass(p, last):
            cand = last + lax.shift_left(i32(1), seq_bits - 1 - p)
            return jnp.where(count_before(cand) < n_keep, cand, last)
        return lax.fori_loop(0, seq_bits, index_pass, zero_row)

    def remask(base, delta, tied, last):
        def body(j, acc):
            above = sc_ref[rows(j), :] - base
            keep = (above > delta) | ((above == delta) & (jnp.logical_not(tied) | (key_idx(j) <= last)))
            m = jnp.where(keep, 0.0, _MASKED).astype(f32)
            madd_ref[rows(j), :] = m
            return acc + fold(m)
        return kept(lax.fori_loop(0, n_tiles, body, jnp.zeros((V7X_SUBLANES, tq), f32)))

    qry = qi * tq + lax.broadcasted_iota(i32, (1, tq), 1)
    n_visible = (qry // CHUNK + 1) * CHUNK
    n_kept = make_mask(thr)

    agree = jnp.max((n_kept != jnp.where(found, n_above + n_equal, n_visible)).astype(i32)) == 0
    aux_ref[...] = jnp.broadcast_to(n_kept, aux_ref.shape)

    @pl.when(agree & (jnp.max(tie.astype(i32)) > 0))
    def _():
        word = lax.broadcasted_iota(i32, (n_prow, tq), 0)
        log2_sublanes = V7X_SUBLANES.bit_length() - 1
        first_key = (word >> log2_sublanes) * tk + (word & (V7X_SUBLANES - 1))

        def tied_before(cand):
            n_bits = jnp.clip((cand - first_key + (V7X_SUBLANES - 1)) >> log2_sublanes, 0, KEY_BITS)
            top_bits = jnp.where(n_bits <= 0, i32(0),
                                 lax.shift_left(i32(-1), jnp.minimum(KEY_BITS - n_bits, KEY_BITS - 1)))
            return jnp.sum(lax.population_count(alive & top_bits), axis=0, keepdims=True)

        last = last_kept(top_k - n_above, tied_before)
        aux_ref[...] = jnp.broadcast_to(remask(thr, jnp.float32(0.0), tie, last), aux_ref.shape)

    settled = agree & (jnp.max((aux_ref[0:1, :] != jnp.where(found, top_k, n_visible)).astype(i32)) == 0)

    @pl.when(jnp.logical_not(settled))
    def _():
        def value_pass(p, prefix):
            cand = prefix + lax.shift_left(i32(1), KEY_BITS - 1 - p)
            cand_f = from_key(cand)
            cnt = count_where(lambda st, j: st >= cand_f)
            return jnp.where(cnt >= top_k, cand, prefix)

        base_key = lax.fori_loop(0, KEY_BITS, value_pass, jnp.full((1, tq), _INT_MIN, i32))
        short = (n_visible < top_k) | (base_key == int_min)
        base = jnp.where(short, lowest, from_key(base_key))
        gap = from_key(base_key + 1) - base

        def gap_pass(p, delta):
            cand = delta + gap / lax.shift_left(i32(1), p + 1).astype(f32)
            cnt = count_where(lambda st, j: (st - base) >= cand)
            return jnp.where(cnt >= top_k, cand, delta)

        delta = lax.fori_loop(0, GAP_BITS, gap_pass, jnp.zeros((1, tq), f32))
        delta = jnp.where(short, 0.0, delta)
        n_keep = top_k - count_where(lambda st, j: (st - base) > delta)
        tied = (count_where(lambda st, j: (st - base) >= delta) > top_k) & jnp.logical_not(short)
        last = last_kept(n_keep, lambda cand: count_where(
            lambda st, j: ((st - base) == delta) & (key_idx(j) < cand)))
        remask(base, delta, tied, last)

    heads = [slice(h * hd, (h + 1) * hd) for h in range(N_HEADS_B)]

    bounds = []
    for h, hs in enumerate(heads):
        qt = q_ref[:, hs].astype(f32).T
        longest_key = jnp.sqrt(jnp.max(ksq_ref[h:h + 1, :]))
        bounds.append(jnp.sqrt(jnp.sum(qt * qt, axis=0, keepdims=True)) * longest_key + bmax_ref[h])

    def qk(j, h):
        return lax.dot_general(k_ref[rows(j), heads[h]], q_ref[:, heads[h]], _NT,
                               preferred_element_type=f32)

    def logits(j, h, near):
        s = qk(j, h) + madd_ref[rows(j), :]
        return s if near is None else s + bias_ref[h, near]

    acc_ref[...] = jnp.zeros(acc_ref.shape, f32)
    l_ref[...] = jnp.zeros(l_ref.shape, f32)

    def issue(j, buf):
        for h in range(N_HEADS_B):
            buf[h] = qk(j, h)

    def consume(j, buf, near):
        madd = madd_ref[rows(j), :]
        for h in range(N_HEADS_B):
            x = buf[h] + madd
            if near is not None:
                x = x + bias_ref[h, near]
            e = jnp.exp2(x - bounds[h])
            l_ref[h] += fold(e)
            acc_ref[h] += jnp.dot(vt_ref[heads[h], rows(j)], e.astype(bf16),
                                  preferred_element_type=f32)

    n_far = jnp.maximum(qi - 1, 0)
    issue(0, sa_ref)

    def far_pair(i, carry):
        j = 2 * i
        issue(j + 1, sb_ref)
        consume(j, sa_ref, None)
        issue(j + 2, sa_ref)
        consume(j + 1, sb_ref, None)
        return carry

    lax.fori_loop(0, n_far // 2, far_pair, 0)
    j0 = 2 * (n_far // 2)

    @pl.when(n_far % 2 == 1)
    def _():
        issue(j0 + 1, sb_ref)
        consume(j0, sa_ref, None)
        issue(j0 + 2, sa_ref)
        consume(j0 + 1, sb_ref, 1)
        consume(j0 + 2, sa_ref, 0)

    @pl.when((n_far % 2 == 0) & (qi >= 1))
    def _():
        issue(j0 + 1, sb_ref)
        consume(j0, sa_ref, 1)
        consume(j0 + 1, sb_ref, 0)

    @pl.when(qi == 0)
    def _():
        consume(0, sa_ref, 0)

    smallest = jnp.min(jnp.sum(l_ref[...], axis=1))

    @pl.when(jnp.logical_not(smallest > SUM_FLOOR))
    def _():
        acc_ref[...] = jnp.zeros(acc_ref.shape, f32)
        first_row = lax.broadcasted_iota(i32, (V7X_SUBLANES, tq), 0) == 0

        def attend_online(j, near):
            for h in range(N_HEADS_B):
                s = logits(j, h, near)
                m_old = m_ref[h]
                m_new = jnp.maximum(m_old, jnp.max(s, axis=0, keepdims=True))
                alpha = jnp.exp2(m_old - m_new)
                p = jnp.exp2(s - m_new)
                l_new = (alpha * jnp.sum(l_ref[h], axis=0, keepdims=True)
                         + jnp.sum(p, axis=0, keepdims=True))
                l_ref[h] = jnp.where(first_row, l_new, 0.0)
                pv = jnp.dot(vt_ref[heads[h], rows(j)], p.astype(bf16), preferred_element_type=f32)
                acc_ref[h] = alpha * acc_ref[h] + pv
                m_ref[h] = m_new

        m_ref[...] = jnp.full(m_ref.shape, _MASKED, f32)
        l_ref[...] = jnp.zeros(l_ref.shape, f32)

        def far_online(j, carry):
            attend_online(j, None)
            return carry

        lax.fori_loop(0, qi - 1, far_online, 0)

        @pl.when(qi >= 1)
        def _():
            attend_online(qi - 1, 1)

        attend_online(qi, 0)

    for h in range(N_HEADS_B):
        total = jnp.sum(l_ref[h], axis=0, keepdims=True)
        out_ref[:, heads[h]] = (acc_ref[h] / total).T.astype(out_ref.dtype)


def _dsa(bias_max, q, iq, iwt, k, vt, ik, ksq, bias_tabs, *, batch, tq, top_k):
    m, d_b = q.shape
    s = m // batch
    nq = s // tq
    hd = d_b // N_HEADS_B
    tile_rows = lambda bi, qi: (bi * nq + qi, 0)
    batch_rows = lambda bi, qi: (bi, 0)
    return pl.pallas_call(
        functools.partial(_dsa_body, tq=tq, top_k=top_k, seq_bits=int(np.log2(s))),
        grid=(batch, nq),
        in_specs=[
            pl.BlockSpec(memory_space=pltpu.SMEM),
            pl.BlockSpec((tq, d_b), tile_rows),
            pl.BlockSpec((IDX_HEADS, tq, IDX_DIM), lambda bi, qi: (0, bi * nq + qi, 0)),
            pl.BlockSpec((IDX_HEADS, tq), lambda bi, qi: (0, bi * nq + qi)),
            pl.BlockSpec((s, d_b), batch_rows),
            pl.BlockSpec((d_b, s), lambda bi, qi: (0, bi)),
            pl.BlockSpec((s, IDX_DIM), batch_rows),
            pl.BlockSpec((V7X_SUBLANES, s), lambda bi, qi: (0, bi)),
            pl.BlockSpec(bias_tabs.shape, lambda bi, qi: (0, 0, 0, 0)),
        ],
        out_specs=pl.BlockSpec((tq, d_b), tile_rows),
        out_shape=jax.ShapeDtypeStruct((m, d_b), bf16),
        scratch_shapes=[
            pltpu.VMEM((s, tq), f32),
            pltpu.VMEM((KEY_BITS, s // KEY_BITS, tq), i32),
            pltpu.VMEM((V7X_SUBLANES, tq), i32),
            pltpu.VMEM((s, tq), f32),
            pltpu.VMEM((N_HEADS_B, tq, tq), f32),
            pltpu.VMEM((N_HEADS_B, tq, tq), f32),
            pltpu.VMEM((N_HEADS_B, 1, tq), f32),
            pltpu.VMEM((N_HEADS_B, V7X_SUBLANES, tq), f32),
            pltpu.VMEM((N_HEADS_B, hd, tq), f32),
        ],
        compiler_params=pltpu.CompilerParams(
            dimension_semantics=("arbitrary", "arbitrary"), vmem_limit_bytes=V7X_VMEM_LIMIT_BYTES),
        name="dsa",
    )(bias_max, q, iq, iwt, k, vt, ik, ksq, bias_tabs)


def _out_ffn_body(x_ref, a_ref, b_ref, woa_ref, wob_ref, gpm_ref, gpf_ref, wg_ref, wu_ref,
                  wd_ref, gpo_ref, o_ref, *, ff_chunk):
    mix = (jnp.dot(a_ref[...], woa_ref[...], preferred_element_type=f32)
           + jnp.dot(b_ref[...], wob_ref[...], preferred_element_type=f32))
    x1 = x_ref[...] + _rms(mix, gpm_ref[...])
    h = _rms(x1, gpf_ref[...]).astype(bf16)
    d_ff = wg_ref.shape[1]
    f = jnp.zeros(x1.shape, f32)
    for c in range(d_ff // ff_chunk):
        cs = slice(c * ff_chunk, (c + 1) * ff_chunk)
        gate = jnp.dot(h, wg_ref[:, cs], preferred_element_type=f32)
        up = jnp.dot(h, wu_ref[:, cs], preferred_element_type=f32)
        act = (gate * jax.nn.sigmoid(gate) * up).astype(bf16)
        f = f + jnp.dot(act, wd_ref[cs, :], preferred_element_type=f32)
    o_ref[...] = x1 + _rms(f, gpo_ref[...])


def _out_ffn(x2, out_a, out_b, w_oa, w_ob, g_pm, g_pf, w_gate, w_up, w_down, g_po, *, tm, ff_chunk):
    m, d = x2.shape
    d_a, d_b, d_ff = out_a.shape[1], out_b.shape[1], w_gate.shape[1]
    row = lambda i: (i, 0)
    const = lambda i: (0, 0)
    resident = functools.partial(pl.BlockSpec, index_map=const, pipeline_mode=pl.Buffered(1))
    return pl.pallas_call(
        functools.partial(_out_ffn_body, ff_chunk=ff_chunk),
        grid=(m // tm,),
        in_specs=[
            pl.BlockSpec((tm, d), row),
            pl.BlockSpec((tm, d_a), row),
            pl.BlockSpec((tm, d_b), row),
            resident((d_a, d)),
            resident((d_b, d)),
            resident((1, d)),
            resident((1, d)),
            resident((d, d_ff)),
            resident((d, d_ff)),
            resident((d_ff, d)),
            resident((1, d)),
        ],
        out_specs=pl.BlockSpec((tm, d), row),
        out_shape=jax.ShapeDtypeStruct((m, d), f32),
        compiler_params=pltpu.CompilerParams(
            dimension_semantics=("arbitrary",), vmem_limit_bytes=V7X_VMEM_LIMIT_BYTES),
        name="out_ffn",
    )(x2, out_a, out_b, w_oa, w_ob, g_pm, g_pf, w_gate, w_up, w_down, g_po)


def _t5_bucket(rel):
    nb = NUM_BUCKETS // 2
    ret = (rel > 0).astype(jnp.int32) * nb
    n = jnp.abs(rel)
    max_exact = nb // 2
    nf = jnp.maximum(n, 1).astype(jnp.float32)
    large = max_exact + (jnp.log(nf / max_exact) / np.float32(np.log(MAX_DISTANCE / max_exact))
                         * (nb - max_exact)).astype(jnp.int32)
    large = jnp.minimum(large, nb - 1)
    return ret + jnp.where(n < max_exact, n, large)


def _near_bias(rel_bias, tq):
    assert tq + 1 >= MAX_DISTANCE
    rel0 = jnp.arange(tq, dtype=jnp.int32)[:, None] - jnp.arange(tq, dtype=jnp.int32)[None, :]
    buckets = _t5_bucket(jnp.stack([rel0, rel0 - tq])) & (NUM_BUCKETS - 1)
    shifted = (rel_bias - rel_bias[NUM_BUCKETS // 2 - 1][None, :]) * np.float32(LOG2_E)
    tabs = jnp.zeros((rel_bias.shape[1],) + buckets.shape, f32)
    for b in range(NUM_BUCKETS):
        tabs = jnp.where(buckets[None] == b, shifted[b][:, None, None, None], tabs)
    return tabs, jnp.maximum(jnp.max(shifted, axis=0), 0.0)


def kernel(x, g_pre_mix, w_in, sgu_ln_g, sgu_ln_b, sgu_w, sgu_b, rel_bias, w_o, g_post_mix,
           g_pre_ffn, w_gate, w_up, w_down, g_post_ffn):
    b, s, d = x.shape
    depth = w_in.shape[0]
    d_a = d // 2
    d_b = d - d_a
    d_main = 2 * d_a + 3 * d_b + IDX_HEADS * IDX_DIM
    top_k = min(MAX_TOPK, s // 4)
    tq = KEY_BITS * V7X_SUBLANES
    tm = TOKEN_TILE
    x2 = x.reshape(b * s, d)
    for l in range(depth):
        w_l = w_in[l]
        w_main = w_l[:, :d_main].astype(bf16)
        w_ik = jnp.pad(w_l[:, d_main:], ((0, 0), (0, V7X_LANES - (IDX_DIM + IDX_HEADS)))).astype(bf16)
        out_a, q, k, vt, iq, ik, iwt, ksq = _proj_sgu(
            x2, g_pre_mix[l][None], w_main, w_ik, sgu_ln_g[l][None], sgu_ln_b[l][None],
            sgu_w[l], jnp.transpose(sgu_b[l]), d_a=d_a, d_b=d_b, tm=tm)
        bias_tabs, bias_max = _near_bias(rel_bias, tq)
        out_b = _dsa(bias_max, q, iq, iwt, k, vt, ik, ksq, bias_tabs, batch=b, tq=tq, top_k=top_k)
        w_ol = w_o[l].astype(bf16)
        x2 = _out_ffn(x2, out_a, out_b, w_ol[:d_a], w_ol[d_a:],
                      g_post_mix[l][None], g_pre_ffn[l][None], w_gate[l].astype(bf16),
                      w_up[l].astype(bf16), w_down[l].astype(bf16), g_post_ffn[l][None],
                      tm=tm, ff_chunk=FF_CHUNK)
    return x2.reshape(b, s, d)
```

```python
import functools

import numpy as np
import jax
import jax.numpy as jnp
from jax import lax
from jax.experimental import pallas as pl
from jax.experimental.pallas import tpu as pltpu

CHUNK = 64
SGU_BLOCK = 128
SGU_GROUPS = 4
N_HEADS_B = 4
IDX_HEADS = 8
IDX_DIM = 64
MAX_TOPK = 256
NUM_BUCKETS = 32
MAX_DISTANCE = 128
EPS = 1e-6

V7X_LANES = 128
V7X_SUBLANES = 8
V7X_VMEM_LIMIT_BYTES = 56 * 1024 * 1024

TOKEN_TILE = 512
FF_CHUNK = 256
KEY_BITS = 32
GAP_BITS = 30
_INT_MIN = -(2 ** 31)
_INT_MAX = 2 ** 31 - 1
_MASKED = -(2.0 ** 100)
LOG2_E = float(np.log2(np.e))
SUM_FLOOR = 1e-30

_NT = (((1,), (1,)), ((), ()))

f32 = jnp.float32
bf16 = jnp.bfloat16
i32 = jnp.int32


def _rms(x, g):
    return x * lax.rsqrt(jnp.mean(x * x, axis=-1, keepdims=True) + EPS) * g


def _proj_sgu_body(x_ref, g_ref, w_ref, wik_ref, lng_ref, lnb_ref, sw_ref, sbt_ref,
                   outa_ref, q_ref, k_ref, vt_ref, iq_ref, ik_ref, iwt_ref, ksq_ref, *, d_a, d_b):
    tm = x_ref.shape[0]
    h = _rms(x_ref[...], g_ref[...]).astype(bf16)

    def proj(c0, width):
        return jnp.dot(h, w_ref[:, c0:c0 + width], preferred_element_type=f32)

    u = proj(0, d_a)
    a_v = proj(d_a, d_a)
    c0 = 2 * d_a
    q_ref[...] = (proj(c0, d_b) * np.float32((d_b // N_HEADS_B) ** -0.5 * LOG2_E)).astype(bf16)
    k = proj(c0 + d_b, d_b).astype(bf16)
    k_ref[...] = k
    kt = k.astype(f32).T
    hd = d_b // N_HEADS_B
    ksq = [jnp.sum(kt[a * hd:(a + 1) * hd] ** 2, axis=0, keepdims=True) for a in range(N_HEADS_B)]
    ksq_ref[...] = jnp.concatenate(ksq + [jnp.zeros((V7X_SUBLANES - N_HEADS_B, tm), f32)], axis=0)
    vt_ref[...] = proj(c0 + 2 * d_b, d_b).astype(bf16).T
    iq = (proj(c0 + 3 * d_b, IDX_HEADS * IDX_DIM) * np.float32(IDX_DIM ** -0.5)).astype(bf16)
    for hh in range(IDX_HEADS):
        iq_ref[hh] = iq[:, hh * IDX_DIM:(hh + 1) * IDX_DIM]
    ikw = jnp.dot(h, wik_ref[...], preferred_element_type=f32)
    ik_ref[...] = ikw[:, :IDX_DIM].astype(bf16)
    iwt_ref[...] = ikw.T[IDX_DIM:IDX_DIM + IDX_HEADS, :] * np.float32(IDX_HEADS ** -0.5)

    mu = jnp.mean(a_v, axis=-1, keepdims=True)
    cen = a_v - mu
    var = jnp.mean(cen * cen, axis=-1, keepdims=True)
    vn = (cen * lax.rsqrt(var + EPS) * lng_ref[...] + lnb_ref[...]).astype(bf16)

    row_chunk = lax.broadcasted_iota(i32, (SGU_BLOCK, SGU_BLOCK), 0) // CHUNK
    col_chunk = lax.broadcasted_iota(i32, (SGU_BLOCK, SGU_BLOCK), 1) // CHUNK
    causal = col_chunk <= row_chunk
    gdim = d_a // SGU_GROUPS
    for g in range(SGU_GROUPS):
        w_g = jnp.where(causal, sw_ref[g], 0.0).astype(bf16)
        b_g = sbt_ref[:, g:g + 1]
        for c in range(tm // SGU_BLOCK):
            rows = slice(c * SGU_BLOCK, (c + 1) * SGU_BLOCK)
            cols = slice(g * gdim, (g + 1) * gdim)
            mixed = jnp.dot(w_g, vn[rows, cols], preferred_element_type=f32) + b_g
            outa_ref[rows, cols] = (u[rows, cols] * mixed).astype(bf16)


def _proj_sgu(x2, g_pre, w_main, w_ik, ln_g, ln_b, sgu_w, sgu_bt, *, d_a, d_b, tm):
    m, d = x2.shape
    const = lambda i: (0, 0)
    row = lambda i: (i, 0)
    col = lambda i: (0, i)
    return pl.pallas_call(
        functools.partial(_proj_sgu_body, d_a=d_a, d_b=d_b),
        grid=(m // tm,),
        in_specs=[
            pl.BlockSpec((tm, d), row),
            pl.BlockSpec((1, d), const),
            pl.BlockSpec(w_main.shape, const),
            pl.BlockSpec(w_ik.shape, const),
            pl.BlockSpec((1, d_a), const),
            pl.BlockSpec((1, d_a), const),
            pl.BlockSpec(sgu_w.shape, lambda i: (0, 0, 0)),
            pl.BlockSpec(sgu_bt.shape, const),
        ],
        out_specs=[
            pl.BlockSpec((tm, d_a), row),
            pl.BlockSpec((tm, d_b), row),
            pl.BlockSpec((tm, d_b), row),
            pl.BlockSpec((d_b, tm), col),
            pl.BlockSpec((IDX_HEADS, tm, IDX_DIM), lambda i: (0, i, 0)),
            pl.BlockSpec((tm, IDX_DIM), row),
            pl.BlockSpec((IDX_HEADS, tm), col),
            pl.BlockSpec((V7X_SUBLANES, tm), col),
        ],
        out_shape=[
            jax.ShapeDtypeStruct((m, d_a), bf16),
            jax.ShapeDtypeStruct((m, d_b), bf16),
            jax.ShapeDtypeStruct((m, d_b), bf16),
            jax.ShapeDtypeStruct((d_b, m), bf16),
            jax.ShapeDtypeStruct((IDX_HEADS, m, IDX_DIM), bf16),
            jax.ShapeDtypeStruct((m, IDX_DIM), bf16),
            jax.ShapeDtypeStruct((IDX_HEADS, m), f32),
            jax.ShapeDtypeStruct((V7X_SUBLANES, m), f32),
        ],
        compiler_params=pltpu.CompilerParams(
            dimension_semantics=("arbitrary",), vmem_limit_bytes=V7X_VMEM_LIMIT_BYTES),
        name="proj_sgu",
    )(x2, g_pre, w_main, w_ik, ln_g, ln_b, sgu_w, sgu_bt)


def _bit_planes(words):
    a = list(words)
    j, m = 16, 0x0000FFFF
    while j:
        k = 0
        while k < KEY_BITS:
            t = (a[k] ^ lax.shift_right_logical(a[k + j], i32(j))) & i32(m)
            a[k] = a[k] ^ t
            a[k + j] = a[k + j] ^ lax.shift_left(t, i32(j))
            k = (k + j + 1) & ~j
        j >>= 1
        m = (m ^ (m << j)) & 0xFFFFFFFF
    return a


def _count_bits(words):
    return jnp.sum(lax.population_count(words), axis=0, keepdims=True)


def _dsa_body(bmax_ref, q_ref, iq_ref, iwt_ref, k_ref, vt_ref, ik_ref, ksq_ref, bias_ref, out_ref,
              sc_ref, planes_ref, aux_ref, madd_ref, sa_ref, sb_ref, m_ref, l_ref, acc_ref,
              *, tq, top_k, seq_bits):
    tk = tq
    assert tk == KEY_BITS * V7X_SUBLANES
    hd = q_ref.shape[1] // N_HEADS_B
    qi = pl.program_id(1)
    n_tiles = qi + 1
    int_min = i32(_INT_MIN)

    def rows(j):
        return pl.ds(pl.multiple_of(j * tk, tk), tk)

    @pl.when((pl.program_id(0) == 0) & (qi == 0))
    def _():
        planes_ref[...] = jnp.zeros(planes_ref.shape, i32)

    def to_key(x):
        bits = lax.bitcast_convert_type(x, i32)
        return bits ^ ((bits >> (KEY_BITS - 1)) & i32(_INT_MAX))

    def from_key(key):
        return lax.bitcast_convert_type(key ^ ((key >> (KEY_BITS - 1)) & i32(_INT_MAX)), f32)

    def score_tile(j, diagonal):
        ik_j = ik_ref[rows(j), :]
        sc = jnp.zeros((tk, tq), f32)
        for h in range(IDX_HEADS):
            d = lax.dot_general(ik_j, iq_ref[h], _NT, preferred_element_type=f32)
            sc = sc + iwt_ref[h:h + 1, :] * jnp.maximum(d, 0.0)
        key = to_key(sc)
        if diagonal:
            key_chunk = lax.broadcasted_iota(i32, (tk, tq), 0) // CHUNK
            qry_chunk = lax.broadcasted_iota(i32, (tk, tq), 1) // CHUNK
            visible = key_chunk <= qry_chunk
            sc = jnp.where(visible, sc, -jnp.inf)
            key = jnp.where(visible, key, int_min)
        sc_ref[rows(j), :] = sc
        return key

    def plane_tile(j, key):
        prow = pl.ds(pl.multiple_of(j * V7X_SUBLANES, V7X_SUBLANES), V7X_SUBLANES)
        for lt in range(tq // V7X_LANES):
            lanes = slice(lt * V7X_LANES, (lt + 1) * V7X_LANES)
            planes = _bit_planes([key[V7X_SUBLANES * i:V7X_SUBLANES * (i + 1), lanes]
                                  for i in range(KEY_BITS)])
            planes_ref[0, prow, lanes] = ~planes[0]
            for p in range(1, KEY_BITS):
                planes_ref[p, prow, lanes] = planes[p]

    def far_scores(j, prev):
        key = score_tile(j, False)
        plane_tile(j - 1, prev)
        return key

    prev = lax.fori_loop(1, qi, far_scores, score_tile(0, False))
    last = score_tile(qi, True)
    plane_tile(jnp.maximum(qi - 1, 0), prev)
    plane_tile(qi, last)

    n_prow = planes_ref.shape[1]
    zero_row = jnp.zeros((1, tq), i32)
    live = lax.broadcasted_iota(i32, (n_prow, tq), 0) < n_tiles * V7X_SUBLANES
    alive, n_above, thr_bits = jnp.where(live, i32(-1), 0), zero_row, zero_row
    for p in range(KEY_BITS):
        ones = alive & planes_ref[p]
        n_ones = _count_bits(ones)
        take = (n_above + n_ones) >= top_k
        alive = jnp.where(take, ones, alive ^ ones)
        n_above = jnp.where(take, n_above, n_above + n_ones)
        thr_bits = thr_bits | jnp.where(take, i32(1 << (KEY_BITS - 1 - p) if p else _INT_MIN), 0)
    thr_key = thr_bits ^ int_min
    n_equal = _count_bits(alive)
    found = thr_key > int_min
    lowest = jnp.float32(np.finfo(np.float32).min)
    thr = jnp.where(found, from_key(thr_key), lowest)
    tie = found & (n_above + n_equal > top_k)

    def fold(x):
        return x.reshape(tk // V7X_SUBLANES, V7X_SUBLANES, tq).sum(axis=0)

    def kept(mask_sums):
        n_masked = jnp.sum(mask_sums, axis=0, keepdims=True) * np.float32(1.0 / _MASKED)
        return n_tiles * tk - n_masked.astype(i32)

    def make_mask(thr_f):
        def body(j, acc):
            m = jnp.where(sc_ref[rows(j), :] >= thr_f, 0.0, _MASKED).astype(f32)
            madd_ref[rows(j), :] = m
            return acc + fold(m)
        return kept(lax.fori_loop(0, n_tiles, body, jnp.zeros((V7X_SUBLANES, tq), f32)))

    def key_idx(j):
        return j * tk + lax.broadcasted_iota(i32, (tk, tq), 0)

    def count_where(pred):
        def body(j, acc):
            return acc + fold(pred(sc_ref[rows(j), :], j).astype(i32))
        acc = lax.fori_loop(0, n_tiles, body, jnp.zeros((V7X_SUBLANES, tq), i32))
        return jnp.sum(acc, axis=0, keepdims=True)

    def last_kept(n_keep, count_before):
        def index_pass(p, last):
            cand = last + lax.shift_left(i32(1), seq_bits - 1 - p)
            return jnp.where(count_before(cand) < n_keep, cand, last)
        return lax.fori_loop(0, seq_bits, index_pass, zero_row)

    def remask(base, delta, tied, last):
        def body(j, acc):
            above = sc_ref[rows(j), :] - base
            keep = (above > delta) | ((above == delta) & (jnp.logical_not(tied) | (key_idx(j) <= last)))
            m = jnp.where(keep, 0.0, _MASKED).astype(f32)
            madd_ref[rows(j), :] = m
            return acc + fold(m)
        return kept(lax.fori_loop(0, n_tiles, body, jnp.zeros((V7X_SUBLANES, tq), f32)))

    qry = qi * tq + lax.broadcasted_iota(i32, (1, tq), 1)
    n_visible = (qry // CHUNK + 1) * CHUNK
    n_kept = make_mask(thr)

    agree = jnp.max((n_kept != jnp.where(found, n_above + n_equal, n_visible)).astype(i32)) == 0
    aux_ref[...] = jnp.broadcast_to(n_kept, aux_ref.shape)

    @pl.when(agree & (jnp.max(tie.astype(i32)) > 0))
    def _():
        word = lax.broadcasted_iota(i32, (n_prow, tq), 0)
        log2_sublanes = V7X_SUBLANES.bit_length() - 1
        first_key = (word >> log2_sublanes) * tk + (word & (V7X_SUBLANES - 1))

        def tied_before(cand):
            n_bits = jnp.clip((cand - first_key + (V7X_SUBLANES - 1)) >> log2_sublanes, 0, KEY_BITS)
            top_bits = jnp.where(n_bits <= 0, i32(0),
                                 lax.shift_left(i32(-1), jnp.minimum(KEY_BITS - n_bits, KEY_BITS - 1)))
            return _count_bits(alive & top_bits)

        last = last_kept(top_k - n_above, tied_before)
        aux_ref[...] = jnp.broadcast_to(remask(thr, jnp.float32(0.0), tie, last), aux_ref.shape)

    settled = agree & (jnp.max((aux_ref[0:1, :] != jnp.where(found, top_k, n_visible)).astype(i32)) == 0)

    @pl.when(jnp.logical_not(settled))
    def _():
        def value_pass(p, prefix):
            cand = prefix + lax.shift_left(i32(1), KEY_BITS - 1 - p)
            cand_f = from_key(cand)
            cnt = count_where(lambda st, j: st >= cand_f)
            return jnp.where(cnt >= top_k, cand, prefix)

        base_key = lax.fori_loop(0, KEY_BITS, value_pass, jnp.full((1, tq), _INT_MIN, i32))
        short = (n_visible < top_k) | (base_key == int_min)
        base = jnp.where(short, lowest, from_key(base_key))
        gap = from_key(base_key + 1) - base

        def gap_pass(p, delta):
            cand = delta + gap / lax.shift_left(i32(1), p + 1).astype(f32)
            cnt = count_where(lambda st, j: (st - base) >= cand)
            return jnp.where(cnt >= top_k, cand, delta)

        delta = lax.fori_loop(0, GAP_BITS, gap_pass, jnp.zeros((1, tq), f32))
        delta = jnp.where(short, 0.0, delta)
        n_keep = top_k - count_where(lambda st, j: (st - base) > delta)
        tied = (count_where(lambda st, j: (st - base) >= delta) > top_k) & jnp.logical_not(short)
        last = last_kept(n_keep, lambda cand: count_where(
            lambda st, j: ((st - base) == delta) & (key_idx(j) < cand)))
        remask(base, delta, tied, last)

    heads = [slice(h * hd, (h + 1) * hd) for h in range(N_HEADS_B)]

    bounds = []
    for h, hs in enumerate(heads):
        qt = q_ref[:, hs].astype(f32).T
        longest_key = jnp.sqrt(jnp.max(ksq_ref[h:h + 1, :]))
        bounds.append(jnp.sqrt(jnp.sum(qt * qt, axis=0, keepdims=True)) * longest_key + bmax_ref[h])

    def qk(j, h):
        return lax.dot_general(k_ref[rows(j), heads[h]], q_ref[:, heads[h]], _NT,
                               preferred_element_type=f32)

    def logits(j, h, near):
        s = qk(j, h) + madd_ref[rows(j), :]
        return s if near is None else s + bias_ref[h, near]

    acc_ref[...] = jnp.zeros(acc_ref.shape, f32)
    l_ref[...] = jnp.zeros(l_ref.shape, f32)

    def issue(j, buf):
        for h in range(N_HEADS_B):
            buf[h] = qk(j, h)

    def consume(j, buf, near):
        madd = madd_ref[rows(j), :]
        for h in range(N_HEADS_B):
            x = buf[h] + madd
            if near is not None:
                x = x + bias_ref[h, near]
            e = jnp.exp2(x - bounds[h])
            l_ref[h] += fold(e)
            acc_ref[h] += jnp.dot(vt_ref[heads[h], rows(j)], e.astype(bf16),
                                  preferred_element_type=f32)

    n_far = jnp.maximum(qi - 1, 0)
    issue(0, sa_ref)

    def far_pair(i, carry):
        j = 2 * i
        issue(j + 1, sb_ref)
        consume(j, sa_ref, None)
        issue(j + 2, sa_ref)
        consume(j + 1, sb_ref, None)
        return carry

    lax.fori_loop(0, n_far // 2, far_pair, 0)
    j0 = 2 * (n_far // 2)

    @pl.when(n_far % 2 == 1)
    def _():
        issue(j0 + 1, sb_ref)
        consume(j0, sa_ref, None)
        issue(j0 + 2, sa_ref)
        consume(j0 + 1, sb_ref, 1)
        consume(j0 + 2, sa_ref, 0)

    @pl.when((n_far % 2 == 0) & (qi >= 1))
    def _():
        issue(j0 + 1, sb_ref)
        consume(j0, sa_ref, 1)
        consume(j0 + 1, sb_ref, 0)

    @pl.when(qi == 0)
    def _():
        consume(0, sa_ref, 0)

    smallest = jnp.min(jnp.sum(l_ref[...], axis=1))

    @pl.when(jnp.logical_not(smallest > SUM_FLOOR))
    def _():
        acc_ref[...] = jnp.zeros(acc_ref.shape, f32)
        first_row = lax.broadcasted_iota(i32, (V7X_SUBLANES, tq), 0) == 0

        def attend_online(j, near):
            for h in range(N_HEADS_B):
                s = logits(j, h, near)
                m_old = m_ref[h]
                m_new = jnp.maximum(m_old, jnp.max(s, axis=0, keepdims=True))
                alpha = jnp.exp2(m_old - m_new)
                p = jnp.exp2(s - m_new)
                l_new = (alpha * jnp.sum(l_ref[h], axis=0, keepdims=True)
                         + jnp.sum(p, axis=0, keepdims=True))
                l_ref[h] = jnp.where(first_row, l_new, 0.0)
                pv = jnp.dot(vt_ref[heads[h], rows(j)], p.astype(bf16), preferred_element_type=f32)
                acc_ref[h] = alpha * acc_ref[h] + pv
                m_ref[h] = m_new

        m_ref[...] = jnp.full(m_ref.shape, _MASKED, f32)
        l_ref[...] = jnp.zeros(l_ref.shape, f32)

        def far_online(j, carry):
            attend_online(j, None)
            return carry

        lax.fori_loop(0, qi - 1, far_online, 0)

        @pl.when(qi >= 1)
        def _():
            attend_online(qi - 1, 1)

        attend_online(qi, 0)

    for h in range(N_HEADS_B):
        total = jnp.sum(l_ref[h], axis=0, keepdims=True)
        out_ref[:, heads[h]] = (acc_ref[h] / total).T.astype(out_ref.dtype)


def _dsa(bias_max, q, iq, iwt, k, vt, ik, ksq, bias_tabs, *, batch, tq, top_k):
    m, d_b = q.shape
    s = m // batch
    nq = s // tq
    hd = d_b // N_HEADS_B
    tile_rows = lambda bi, qi: (bi * nq + qi, 0)
    batch_rows = lambda bi, qi: (bi, 0)
    return pl.pallas_call(
        functools.partial(_dsa_body, tq=tq, top_k=top_k, seq_bits=int(np.log2(s))),
        grid=(batch, nq),
        in_specs=[
            pl.BlockSpec(memory_space=pltpu.SMEM),
            pl.BlockSpec((tq, d_b), tile_rows),
            pl.BlockSpec((IDX_HEADS, tq, IDX_DIM), lambda bi, qi: (0, bi * nq + qi, 0)),
            pl.BlockSpec((IDX_HEADS, tq), lambda bi, qi: (0, bi * nq + qi)),
            pl.BlockSpec((s, d_b), batch_rows),
            pl.BlockSpec((d_b, s), lambda bi, qi: (0, bi)),
            pl.BlockSpec((s, IDX_DIM), batch_rows),
            pl.BlockSpec((V7X_SUBLANES, s), lambda bi, qi: (0, bi)),
            pl.BlockSpec(bias_tabs.shape, lambda bi, qi: (0, 0, 0, 0)),
        ],
        out_specs=pl.BlockSpec((tq, d_b), tile_rows),
        out_shape=jax.ShapeDtypeStruct((m, d_b), bf16),
        scratch_shapes=[
            pltpu.VMEM((s, tq), f32),
            pltpu.VMEM((KEY_BITS, s // KEY_BITS, tq), i32),
            pltpu.VMEM((V7X_SUBLANES, tq), i32),
            pltpu.VMEM((s, tq), f32),
            pltpu.VMEM((N_HEADS_B, tq, tq), f32),
            pltpu.VMEM((N_HEADS_B, tq, tq), f32),
            pltpu.VMEM((N_HEADS_B, 1, tq), f32),
            pltpu.VMEM((N_HEADS_B, V7X_SUBLANES, tq), f32),
            pltpu.VMEM((N_HEADS_B, hd, tq), f32),
        ],
        compiler_params=pltpu.CompilerParams(
            dimension_semantics=("arbitrary", "arbitrary"), vmem_limit_bytes=V7X_VMEM_LIMIT_BYTES),
        name="dsa",
    )(bias_max, q, iq, iwt, k, vt, ik, ksq, bias_tabs)


def _out_ffn_body(x_ref, a_ref, b_ref, woa_ref, wob_ref, gpm_ref, gpf_ref, wg_ref, wu_ref,
                  wd_ref, gpo_ref, o_ref, *, ff_chunk):
    mix = (jnp.dot(a_ref[...], woa_ref[...], preferred_element_type=f32)
           + jnp.dot(b_ref[...], wob_ref[...], preferred_element_type=f32))
    x1 = x_ref[...] + _rms(mix, gpm_ref[...])
    h = _rms(x1, gpf_ref[...]).astype(bf16)
    d_ff = wg_ref.shape[1]
    f = jnp.zeros(x1.shape, f32)
    for c in range(d_ff // ff_chunk):
        cs = slice(c * ff_chunk, (c + 1) * ff_chunk)
        gate = jnp.dot(h, wg_ref[:, cs], preferred_element_type=f32)
        up = jnp.dot(h, wu_ref[:, cs], preferred_element_type=f32)
        act = (gate * jax.nn.sigmoid(gate) * up).astype(bf16)
        f = f + jnp.dot(act, wd_ref[cs, :], preferred_element_type=f32)
    o_ref[...] = x1 + _rms(f, gpo_ref[...])


def _out_ffn(x2, out_a, out_b, w_oa, w_ob, g_pm, g_pf, w_gate, w_up, w_down, g_po, *, tm, ff_chunk):
    m, d = x2.shape
    d_a, d_b, d_ff = out_a.shape[1], out_b.shape[1], w_gate.shape[1]
    row = lambda i: (i, 0)
    const = lambda i: (0, 0)
    resident = functools.partial(pl.BlockSpec, index_map=const, pipeline_mode=pl.Buffered(1))
    return pl.pallas_call(
        functools.partial(_out_ffn_body, ff_chunk=ff_chunk),
        grid=(m // tm,),
        in_specs=[
            pl.BlockSpec((tm, d), row),
            pl.BlockSpec((tm, d_a), row),
            pl.BlockSpec((tm, d_b), row),
            resident((d_a, d)),
            resident((d_b, d)),
            resident((1, d)),
            resident((1, d)),
            resident((d, d_ff)),
            resident((d, d_ff)),
            resident((d_ff, d)),
            resident((1, d)),
        ],
        out_specs=pl.BlockSpec((tm, d), row),
        out_shape=jax.ShapeDtypeStruct((m, d), f32),
        compiler_params=pltpu.CompilerParams(
            dimension_semantics=("arbitrary",), vmem_limit_bytes=V7X_VMEM_LIMIT_BYTES),
        name="out_ffn",
    )(x2, out_a, out_b, w_oa, w_ob, g_pm, g_pf, w_gate, w_up, w_down, g_po)


def _t5_bucket(rel):
    nb = NUM_BUCKETS // 2
    ret = (rel > 0).astype(jnp.int32) * nb
    n = jnp.abs(rel)
    max_exact = nb // 2
    nf = jnp.maximum(n, 1).astype(jnp.float32)
    large = max_exact + (jnp.log(nf / max_exact) / np.float32(np.log(MAX_DISTANCE / max_exact))
                         * (nb - max_exact)).astype(jnp.int32)
    large = jnp.minimum(large, nb - 1)
    return ret + jnp.where(n < max_exact, n, large)


def _near_bias(rel_bias, tq):
    assert tq + 1 >= MAX_DISTANCE
    rel0 = jnp.arange(tq, dtype=jnp.int32)[:, None] - jnp.arange(tq, dtype=jnp.int32)[None, :]
    buckets = _t5_bucket(jnp.stack([rel0, rel0 - tq])) & (NUM_BUCKETS - 1)
    shifted = (rel_bias - rel_bias[NUM_BUCKETS // 2 - 1][None, :]) * np.float32(LOG2_E)
    tabs = jnp.zeros((rel_bias.shape[1],) + buckets.shape, f32)
    for b in range(NUM_BUCKETS):
        tabs = jnp.where(buckets[None] == b, shifted[b][:, None, None, None], tabs)
    return tabs, jnp.maximum(jnp.max(shifted, axis=0), 0.0)


def kernel(x, g_pre_mix, w_in, sgu_ln_g, sgu_ln_b, sgu_w, sgu_b, rel_bias, w_o, g_post_mix,
           g_pre_ffn, w_gate, w_up, w_down, g_post_ffn):
    b, s, d = x.shape
    depth = w_in.shape[0]
    d_a = d // 2
    d_b = d - d_a
    d_main = 2 * d_a + 3 * d_b + IDX_HEADS * IDX_DIM
    top_k = min(MAX_TOPK, s // 4)
    tq = KEY_BITS * V7X_SUBLANES
    tm = TOKEN_TILE
    x2 = x.reshape(b * s, d)
    for l in range(depth):
        w_l = w_in[l]
        w_main = w_l[:, :d_main].astype(bf16)
        w_ik = jnp.pad(w_l[:, d_main:], ((0, 0), (0, V7X_LANES - (IDX_DIM + IDX_HEADS)))).astype(bf16)
        out_a, q, k, vt, iq, ik, iwt, ksq = _proj_sgu(
            x2, g_pre_mix[l][None], w_main, w_ik, sgu_ln_g[l][None], sgu_ln_b[l][None],
            sgu_w[l], jnp.transpose(sgu_b[l]), d_a=d_a, d_b=d_b, tm=tm)
        bias_tabs, bias_max = _near_bias(rel_bias, tq)
        out_b = _dsa(bias_max, q, iq, iwt, k, vt, ik, ksq, bias_tabs, batch=b, tq=tq, top_k=top_k)
        w_ol = w_o[l].astype(bf16)
        x2 = _out_ffn(x2, out_a, out_b, w_ol[:d_a], w_ol[d_a:],
                      g_post_mix[l][None], g_pre_ffn[l][None], w_gate[l].astype(bf16),
                      w_up[l].astype(bf16), w_down[l].astype(bf16), g_post_ffn[l][None],
                      tm=tm, ff_chunk=FF_CHUNK)
    return x2.reshape(b, s, d)
```

```python
import functools

import numpy as np
import jax
import jax.numpy as jnp
from jax import lax
from jax.experimental import pallas as pl
from jax.experimental.pallas import tpu as pltpu

CHUNK = 64
SGU_BLOCK = 128
SGU_GROUPS = 4
N_HEADS_B = 4
IDX_HEADS = 8
IDX_DIM = 64
MAX_TOPK = 256
NUM_BUCKETS = 32
MAX_DISTANCE = 128
EPS = 1e-6

V7X_LANES = 128
V7X_SUBLANES = 8
V7X_VMEM_LIMIT_BYTES = 56 * 1024 * 1024

TOKEN_TILE = 512
FF_CHUNK = 256
KEY_BITS = 32
GAP_BITS = 30
_INT_MIN = -(2 ** 31)
_INT_MAX = 2 ** 31 - 1
_MASKED = -(2.0 ** 100)
LOG2_E = float(np.log2(np.e))
SUM_FLOOR = 1e-30

_NT = (((1,), (1,)), ((), ()))

f32 = jnp.float32
bf16 = jnp.bfloat16
i32 = jnp.int32


def _rms(x, g):
    return x * lax.rsqrt(jnp.mean(x * x, axis=-1, keepdims=True) + EPS) * g


def _proj_sgu_body(x_ref, g_ref, w_ref, wik_ref, lng_ref, lnb_ref, sw_ref, sbt_ref,
                   outa_ref, q_ref, k_ref, vt_ref, iq_ref, ik_ref, iwt_ref, ksq_ref, *, d_a, d_b):
    tm = x_ref.shape[0]
    h = _rms(x_ref[...], g_ref[...]).astype(bf16)

    def proj(c0, width):
        return jnp.dot(h, w_ref[:, c0:c0 + width], preferred_element_type=f32)

    u = proj(0, d_a)
    a_v = proj(d_a, d_a)
    c0 = 2 * d_a
    q_ref[...] = (proj(c0, d_b) * np.float32((d_b // N_HEADS_B) ** -0.5 * LOG2_E)).astype(bf16)
    k = proj(c0 + d_b, d_b).astype(bf16)
    k_ref[...] = k
    kt = k.astype(f32).T
    hd = d_b // N_HEADS_B
    ksq = [jnp.sum(kt[a * hd:(a + 1) * hd] ** 2, axis=0, keepdims=True) for a in range(N_HEADS_B)]
    ksq_ref[...] = jnp.concatenate(ksq + [jnp.zeros((V7X_SUBLANES - N_HEADS_B, tm), f32)], axis=0)
    vt_ref[...] = proj(c0 + 2 * d_b, d_b).astype(bf16).T
    iq = (proj(c0 + 3 * d_b, IDX_HEADS * IDX_DIM) * np.float32(IDX_DIM ** -0.5)).astype(bf16)
    for hh in range(IDX_HEADS):
        iq_ref[hh] = iq[:, hh * IDX_DIM:(hh + 1) * IDX_DIM]
    ikw = jnp.dot(h, wik_ref[...], preferred_element_type=f32)
    ik_ref[...] = ikw[:, :IDX_DIM].astype(bf16)
    iwt_ref[...] = ikw.T[IDX_DIM:IDX_DIM + IDX_HEADS, :] * np.float32(IDX_HEADS ** -0.5)

    mu = jnp.mean(a_v, axis=-1, keepdims=True)
    cen = a_v - mu
    var = jnp.mean(cen * cen, axis=-1, keepdims=True)
    vn = (cen * lax.rsqrt(var + EPS) * lng_ref[...] + lnb_ref[...]).astype(bf16)

    row_chunk = lax.broadcasted_iota(i32, (SGU_BLOCK, SGU_BLOCK), 0) // CHUNK
    col_chunk = lax.broadcasted_iota(i32, (SGU_BLOCK, SGU_BLOCK), 1) // CHUNK
    causal = col_chunk <= row_chunk
    gdim = d_a // SGU_GROUPS
    for g in range(SGU_GROUPS):
        w_g = jnp.where(causal, sw_ref[g], 0.0).astype(bf16)
        b_g = sbt_ref[:, g:g + 1]
        for c in range(tm // SGU_BLOCK):
            rows = slice(c * SGU_BLOCK, (c + 1) * SGU_BLOCK)
            cols = slice(g * gdim, (g + 1) * gdim)
            mixed = jnp.dot(w_g, vn[rows, cols], preferred_element_type=f32) + b_g
            outa_ref[rows, cols] = (u[rows, cols] * mixed).astype(bf16)


def _proj_sgu(x2, g_pre, w_main, w_ik, ln_g, ln_b, sgu_w, sgu_bt, *, d_a, d_b, tm):
    m, d = x2.shape
    const = lambda i: (0, 0)
    row = lambda i: (i, 0)
    col = lambda i: (0, i)
    return pl.pallas_call(
        functools.partial(_proj_sgu_body, d_a=d_a, d_b=d_b),
        grid=(m // tm,),
        in_specs=[
            pl.BlockSpec((tm, d), row),
            pl.BlockSpec((1, d), const),
            pl.BlockSpec(w_main.shape, const),
            pl.BlockSpec(w_ik.shape, const),
            pl.BlockSpec((1, d_a), const),
            pl.BlockSpec((1, d_a), const),
            pl.BlockSpec(sgu_w.shape, lambda i: (0, 0, 0)),
            pl.BlockSpec(sgu_bt.shape, const),
        ],
        out_specs=[
            pl.BlockSpec((tm, d_a), row),
            pl.BlockSpec((tm, d_b), row),
            pl.BlockSpec((tm, d_b), row),
            pl.BlockSpec((d_b, tm), col),
            pl.BlockSpec((IDX_HEADS, tm, IDX_DIM), lambda i: (0, i, 0)),
            pl.BlockSpec((tm, IDX_DIM), row),
            pl.BlockSpec((IDX_HEADS, tm), col),
            pl.BlockSpec((V7X_SUBLANES, tm), col),
        ],
        out_shape=[
            jax.ShapeDtypeStruct((m, d_a), bf16),
            jax.ShapeDtypeStruct((m, d_b), bf16),
            jax.ShapeDtypeStruct((m, d_b), bf16),
            jax.ShapeDtypeStruct((d_b, m), bf16),
            jax.ShapeDtypeStruct((IDX_HEADS, m, IDX_DIM), bf16),
            jax.ShapeDtypeStruct((m, IDX_DIM), bf16),
            jax.ShapeDtypeStruct((IDX_HEADS, m), f32),
            jax.ShapeDtypeStruct((V7X_SUBLANES, m), f32),
        ],
        compiler_params=pltpu.CompilerParams(
            dimension_semantics=("arbitrary",), vmem_limit_bytes=V7X_VMEM_LIMIT_BYTES),
        name="proj_sgu",
    )(x2, g_pre, w_main, w_ik, ln_g, ln_b, sgu_w, sgu_bt)


def _bit_planes(words):
    a = list(words)
    j, m = 16, 0x0000FFFF
    while j:
        k = 0
        while k < KEY_BITS:
            t = (a[k] ^ lax.shift_right_logical(a[k + j], i32(j))) & i32(m)
            a[k] = a[k] ^ t
            a[k + j] = a[k + j] ^ lax.shift_left(t, i32(j))
            k = (k + j + 1) & ~j
        j >>= 1
        m = (m ^ (m << j)) & 0xFFFFFFFF
    return a


def _count_bits(words):
    return jnp.sum(lax.population_count(words), axis=0, keepdims=True)


def _dsa_body(bmax_ref, q_ref, iq_ref, iwt_ref, k_ref, vt_ref, ik_ref, ksq_ref, bias_ref, out_ref,
              sc_ref, planes_ref, aux_ref, madd_ref, sa_ref, sb_ref, m_ref, l_ref, acc_ref,
              *, tq, top_k, seq_bits):
    tk = tq
    assert tk == KEY_BITS * V7X_SUBLANES
    hd = q_ref.shape[1] // N_HEADS_B
    qi = pl.program_id(1)
    n_tiles = qi + 1
    int_min = i32(_INT_MIN)

    def rows(j):
        return pl.ds(pl.multiple_of(j * tk, tk), tk)

    @pl.when((pl.program_id(0) == 0) & (qi == 0))
    def _():
        planes_ref[...] = jnp.zeros(planes_ref.shape, i32)

    def to_key(x):
        bits = lax.bitcast_convert_type(x, i32)
        return bits ^ ((bits >> (KEY_BITS - 1)) & i32(_INT_MAX))

    def from_key(key):
        return lax.bitcast_convert_type(key ^ ((key >> (KEY_BITS - 1)) & i32(_INT_MAX)), f32)

    def score_tile(j, diagonal):
        ik_j = ik_ref[rows(j), :]
        sc = jnp.zeros((tk, tq), f32)
        for h in range(IDX_HEADS):
            d = lax.dot_general(ik_j, iq_ref[h], _NT, preferred_element_type=f32)
            sc = sc + iwt_ref[h:h + 1, :] * jnp.maximum(d, 0.0)
        key = to_key(sc)
        if diagonal:
            key_chunk = lax.broadcasted_iota(i32, (tk, tq), 0) // CHUNK
            qry_chunk = lax.broadcasted_iota(i32, (tk, tq), 1) // CHUNK
            visible = key_chunk <= qry_chunk
            sc = jnp.where(visible, sc, -jnp.inf)
            key = jnp.where(visible, key, int_min)
        sc_ref[rows(j), :] = sc
        return key

    def plane_tile(j, key):
        prow = pl.ds(pl.multiple_of(j * V7X_SUBLANES, V7X_SUBLANES), V7X_SUBLANES)
        for lt in range(tq // V7X_LANES):
            lanes = slice(lt * V7X_LANES, (lt + 1) * V7X_LANES)
            planes = _bit_planes([key[V7X_SUBLANES * i:V7X_SUBLANES * (i + 1), lanes]
                                  for i in range(KEY_BITS)])
            planes_ref[0, prow, lanes] = ~planes[0]
            for p in range(1, KEY_BITS):
                planes_ref[p, prow, lanes] = planes[p]

    def far_scores(j, prev):
        key = score_tile(j, False)
        plane_tile(j - 1, prev)
        return key

    def far_pair(i, prev):
        return far_scores(2 * i + 2, far_scores(2 * i + 1, prev))

    n_far = jnp.maximum(qi - 1, 0)
    prev = lax.fori_loop(0, n_far // 2, far_pair, score_tile(0, False))
    prev = lax.cond(n_far % 2 == 1, lambda p: far_scores(qi - 1, p), lambda p: p, prev)
    last = score_tile(qi, True)
    plane_tile(jnp.maximum(qi - 1, 0), prev)
    plane_tile(qi, last)

    n_prow = planes_ref.shape[1]
    zero_row = jnp.zeros((1, tq), i32)
    live = lax.broadcasted_iota(i32, (n_prow, tq), 0) < n_tiles * V7X_SUBLANES
    alive, n_above, thr_bits = jnp.where(live, i32(-1), 0), zero_row, zero_row
    for p in range(KEY_BITS):
        ones = alive & planes_ref[p]
        n_ones = _count_bits(ones)
        take = (n_above + n_ones) >= top_k
        alive = jnp.where(take, ones, alive ^ ones)
        n_above = jnp.where(take, n_above, n_above + n_ones)
        thr_bits = thr_bits | jnp.where(take, i32(1 << (KEY_BITS - 1 - p) if p else _INT_MIN), 0)
    thr_key = thr_bits ^ int_min
    n_equal = _count_bits(alive)
    found = thr_key > int_min
    lowest = jnp.float32(np.finfo(np.float32).min)
    thr = jnp.where(found, from_key(thr_key), lowest)
    tie = found & (n_above + n_equal > top_k)

    def fold(x):
        return x.reshape(tk // V7X_SUBLANES, V7X_SUBLANES, tq).sum(axis=0)

    def kept(mask_sums):
        n_masked = jnp.sum(mask_sums, axis=0, keepdims=True) * np.float32(1.0 / _MASKED)
        return n_tiles * tk - n_masked.astype(i32)

    def make_mask(thr_f):
        def body(j, acc):
            m = jnp.where(sc_ref[rows(j), :] >= thr_f, 0.0, _MASKED).astype(f32)
            madd_ref[rows(j), :] = m
            return acc + fold(m)
        return kept(lax.fori_loop(0, n_tiles, body, jnp.zeros((V7X_SUBLANES, tq), f32)))

    def key_idx(j):
        return j * tk + lax.broadcasted_iota(i32, (tk, tq), 0)

    def count_where(pred):
        def body(j, acc):
            return acc + fold(pred(sc_ref[rows(j), :], j).astype(i32))
        acc = lax.fori_loop(0, n_tiles, body, jnp.zeros((V7X_SUBLANES, tq), i32))
        return jnp.sum(acc, axis=0, keepdims=True)

    def last_kept(n_keep, count_before):
        def index_pass(p, last):
            cand = last + lax.shift_left(i32(1), seq_bits - 1 - p)
            return jnp.where(count_before(cand) < n_keep, cand, last)
        return lax.fori_loop(0, seq_bits, index_pass, zero_row)

    def remask(base, delta, tied, last):
        def body(j, acc):
            above = sc_ref[rows(j), :] - base
            keep = (above > delta) | ((above == delta) & (jnp.logical_not(tied) | (key_idx(j) <= last)))
            m = jnp.where(keep, 0.0, _MASKED).astype(f32)
            madd_ref[rows(j), :] = m
            return acc + fold(m)
        return kept(lax.fori_loop(0, n_tiles, body, jnp.zeros((V7X_SUBLANES, tq), f32)))

    qry = qi * tq + lax.broadcasted_iota(i32, (1, tq), 1)
    n_visible = (qry // CHUNK + 1) * CHUNK
    n_kept = make_mask(thr)

    agree = jnp.max((n_kept != jnp.where(found, n_above + n_equal, n_visible)).astype(i32)) == 0
    aux_ref[...] = jnp.broadcast_to(n_kept, aux_ref.shape)

    @pl.when(agree & (jnp.max(tie.astype(i32)) > 0))
    def _():
        word = lax.broadcasted_iota(i32, (n_prow, tq), 0)
        log2_sublanes = V7X_SUBLANES.bit_length() - 1
        first_key = (word >> log2_sublanes) * tk + (word & (V7X_SUBLANES - 1))

        def tied_before(cand):
            n_bits = jnp.clip((cand - first_key + (V7X_SUBLANES - 1)) >> log2_sublanes, 0, KEY_BITS)
            top_bits = jnp.where(n_bits <= 0, i32(0),
                                 lax.shift_left(i32(-1), jnp.minimum(KEY_BITS - n_bits, KEY_BITS - 1)))
            return _count_bits(alive & top_bits)

        last = last_kept(top_k - n_above, tied_before)
        aux_ref[...] = jnp.broadcast_to(remask(thr, jnp.float32(0.0), tie, last), aux_ref.shape)

    settled = agree & (jnp.max((aux_ref[0:1, :] != jnp.where(found, top_k, n_visible)).astype(i32)) == 0)

    @pl.when(jnp.logical_not(settled))
    def _():
        def value_pass(p, prefix):
            cand = prefix + lax.shift_left(i32(1), KEY_BITS - 1 - p)
            cand_f = from_key(cand)
            cnt = count_where(lambda st, j: st >= cand_f)
            return jnp.where(cnt >= top_k, cand, prefix)

        base_key = lax.fori_loop(0, KEY_BITS, value_pass, jnp.full((1, tq), _INT_MIN, i32))
        short = (n_visible < top_k) | (base_key == int_min)
        base = jnp.where(short, lowest, from_key(base_key))
        gap = from_key(base_key + 1) - base

        def gap_pass(p, delta):
            cand = delta + gap / lax.shift_left(i32(1), p + 1).astype(f32)
            cnt = count_where(lambda st, j: (st - base) >= cand)
            return jnp.where(cnt >= top_k, cand, delta)

        delta = lax.fori_loop(0, GAP_BITS, gap_pass, jnp.zeros((1, tq), f32))
        delta = jnp.where(short, 0.0, delta)
        n_keep = top_k - count_where(lambda st, j: (st - base) > delta)
        tied = (count_where(lambda st, j: (st - base) >= delta) > top_k) & jnp.logical_not(short)
        last = last_kept(n_keep, lambda cand: count_where(
            lambda st, j: ((st - base) == delta) & (key_idx(j) < cand)))
        remask(base, delta, tied, last)

    heads = [slice(h * hd, (h + 1) * hd) for h in range(N_HEADS_B)]

    bounds = []
    for h, hs in enumerate(heads):
        qt = q_ref[:, hs].astype(f32).T
        longest_key = jnp.sqrt(jnp.max(ksq_ref[h:h + 1, :]))
        bounds.append(jnp.sqrt(jnp.sum(qt * qt, axis=0, keepdims=True)) * longest_key + bmax_ref[h])

    def qk(j, h):
        return lax.dot_general(k_ref[rows(j), heads[h]], q_ref[:, heads[h]], _NT,
                               preferred_element_type=f32)

    def logits(j, h, near):
        s = qk(j, h) + madd_ref[rows(j), :]
        return s if near is None else s + bias_ref[h, near]

    acc_ref[...] = jnp.zeros(acc_ref.shape, f32)
    l_ref[...] = jnp.zeros(l_ref.shape, f32)

    def issue(j, buf):
        for h in range(N_HEADS_B):
            buf[h] = qk(j, h)

    def consume(j, buf, near):
        madd = madd_ref[rows(j), :]
        for h in range(N_HEADS_B):
            x = buf[h] + madd
            if near is not None:
                x = x + bias_ref[h, near]
            e = jnp.exp2(x - bounds[h])
            l_ref[h] += fold(e)
            acc_ref[h] += jnp.dot(vt_ref[heads[h], rows(j)], e.astype(bf16),
                                  preferred_element_type=f32)

    issue(0, sa_ref)

    def far_pair(i, carry):
        j = 2 * i
        issue(j + 1, sb_ref)
        consume(j, sa_ref, None)
        issue(j + 2, sa_ref)
        consume(j + 1, sb_ref, None)
        return carry

    lax.fori_loop(0, n_far // 2, far_pair, 0)
    j0 = 2 * (n_far // 2)

    @pl.when(n_far % 2 == 1)
    def _():
        issue(j0 + 1, sb_ref)
        consume(j0, sa_ref, None)
        issue(j0 + 2, sa_ref)
        consume(j0 + 1, sb_ref, 1)
        consume(j0 + 2, sa_ref, 0)

    @pl.when((n_far % 2 == 0) & (qi >= 1))
    def _():
        issue(j0 + 1, sb_ref)
        consume(j0, sa_ref, 1)
        consume(j0 + 1, sb_ref, 0)

    @pl.when(qi == 0)
    def _():
        consume(0, sa_ref, 0)

    smallest = jnp.min(jnp.sum(l_ref[...], axis=1))

    @pl.when(jnp.logical_not(smallest > SUM_FLOOR))
    def _():
        acc_ref[...] = jnp.zeros(acc_ref.shape, f32)
        first_row = lax.broadcasted_iota(i32, (V7X_SUBLANES, tq), 0) == 0

        def attend_online(j, near):
            for h in range(N_HEADS_B):
                s = logits(j, h, near)
                m_old = m_ref[h]
                m_new = jnp.maximum(m_old, jnp.max(s, axis=0, keepdims=True))
                alpha = jnp.exp2(m_old - m_new)
                p = jnp.exp2(s - m_new)
                l_new = (alpha * jnp.sum(l_ref[h], axis=0, keepdims=True)
                         + jnp.sum(p, axis=0, keepdims=True))
                l_ref[h] = jnp.where(first_row, l_new, 0.0)
                pv = jnp.dot(vt_ref[heads[h], rows(j)], p.astype(bf16), preferred_element_type=f32)
                acc_ref[h] = alpha * acc_ref[h] + pv
                m_ref[h] = m_new

        m_ref[...] = jnp.full(m_ref.shape, _MASKED, f32)
        l_ref[...] = jnp.zeros(l_ref.shape, f32)

        def far_online(j, carry):
            attend_online(j, None)
            return carry

        lax.fori_loop(0, qi - 1, far_online, 0)

        @pl.when(qi >= 1)
        def _():
            attend_online(qi - 1, 1)

        attend_online(qi, 0)

    for h in range(N_HEADS_B):
        total = jnp.sum(l_ref[h], axis=0, keepdims=True)
        out_ref[:, heads[h]] = (acc_ref[h] / total).T.astype(out_ref.dtype)


def _dsa(bias_max, q, iq, iwt, k, vt, ik, ksq, bias_tabs, *, batch, tq, top_k):
    m, d_b = q.shape
    s = m // batch
    nq = s // tq
    hd = d_b // N_HEADS_B
    tile_rows = lambda bi, qi: (bi * nq + qi, 0)
    batch_rows = lambda bi, qi: (bi, 0)
    return pl.pallas_call(
        functools.partial(_dsa_body, tq=tq, top_k=top_k, seq_bits=int(np.log2(s))),
        grid=(batch, nq),
        in_specs=[
            pl.BlockSpec(memory_space=pltpu.SMEM),
            pl.BlockSpec((tq, d_b), tile_rows),
            pl.BlockSpec((IDX_HEADS, tq, IDX_DIM), lambda bi, qi: (0, bi * nq + qi, 0)),
            pl.BlockSpec((IDX_HEADS, tq), lambda bi, qi: (0, bi * nq + qi)),
            pl.BlockSpec((s, d_b), batch_rows),
            pl.BlockSpec((d_b, s), lambda bi, qi: (0, bi)),
            pl.BlockSpec((s, IDX_DIM), batch_rows),
            pl.BlockSpec((V7X_SUBLANES, s), lambda bi, qi: (0, bi)),
            pl.BlockSpec(bias_tabs.shape, lambda bi, qi: (0, 0, 0, 0)),
        ],
        out_specs=pl.BlockSpec((tq, d_b), tile_rows),
        out_shape=jax.ShapeDtypeStruct((m, d_b), bf16),
        scratch_shapes=[
            pltpu.VMEM((s, tq), f32),
            pltpu.VMEM((KEY_BITS, s // KEY_BITS, tq), i32),
            pltpu.VMEM((V7X_SUBLANES, tq), i32),
            pltpu.VMEM((s, tq), f32),
            pltpu.VMEM((N_HEADS_B, tq, tq), f32),
            pltpu.VMEM((N_HEADS_B, tq, tq), f32),
            pltpu.VMEM((N_HEADS_B, 1, tq), f32),
            pltpu.VMEM((N_HEADS_B, V7X_SUBLANES, tq), f32),
            pltpu.VMEM((N_HEADS_B, hd, tq), f32),
        ],
        compiler_params=pltpu.CompilerParams(
            dimension_semantics=("arbitrary", "arbitrary"), vmem_limit_bytes=V7X_VMEM_LIMIT_BYTES),
        name="dsa",
    )(bias_max, q, iq, iwt, k, vt, ik, ksq, bias_tabs)


def _out_ffn_body(x_ref, a_ref, b_ref, woa_ref, wob_ref, gpm_ref, gpf_ref, wg_ref, wu_ref,
                  wd_ref, gpo_ref, o_ref, *, ff_chunk):
    mix = (jnp.dot(a_ref[...], woa_ref[...], preferred_element_type=f32)
           + jnp.dot(b_ref[...], wob_ref[...], preferred_element_type=f32))
    x1 = x_ref[...] + _rms(mix, gpm_ref[...])
    h = _rms(x1, gpf_ref[...]).astype(bf16)
    d_ff = wg_ref.shape[1]
    f = jnp.zeros(x1.shape, f32)
    for c in range(d_ff // ff_chunk):
        cs = slice(c * ff_chunk, (c + 1) * ff_chunk)
        gate = jnp.dot(h, wg_ref[:, cs], preferred_element_type=f32)
        up = jnp.dot(h, wu_ref[:, cs], preferred_element_type=f32)
        act = (gate * jax.nn.sigmoid(gate) * up).astype(bf16)
        f = f + jnp.dot(act, wd_ref[cs, :], preferred_element_type=f32)
    o_ref[...] = x1 + _rms(f, gpo_ref[...])


def _out_ffn(x2, out_a, out_b, w_oa, w_ob, g_pm, g_pf, w_gate, w_up, w_down, g_po, *, tm, ff_chunk):
    m, d = x2.shape
    d_a, d_b, d_ff = out_a.shape[1], out_b.shape[1], w_gate.shape[1]
    row = lambda i: (i, 0)
    const = lambda i: (0, 0)
    resident = functools.partial(pl.BlockSpec, index_map=const, pipeline_mode=pl.Buffered(1))
    return pl.pallas_call(
        functools.partial(_out_ffn_body, ff_chunk=ff_chunk),
        grid=(m // tm,),
        in_specs=[
            pl.BlockSpec((tm, d), row),
            pl.BlockSpec((tm, d_a), row),
            pl.BlockSpec((tm, d_b), row),
            resident((d_a, d)),
            resident((d_b, d)),
            resident((1, d)),
            resident((1, d)),
            resident((d, d_ff)),
            resident((d, d_ff)),
            resident((d_ff, d)),
            resident((1, d)),
        ],
        out_specs=pl.BlockSpec((tm, d), row),
        out_shape=jax.ShapeDtypeStruct((m, d), f32),
        compiler_params=pltpu.CompilerParams(
            dimension_semantics=("arbitrary",), vmem_limit_bytes=V7X_VMEM_LIMIT_BYTES),
        name="out_ffn",
    )(x2, out_a, out_b, w_oa, w_ob, g_pm, g_pf, w_gate, w_up, w_down, g_po)


def _t5_bucket(rel):
    nb = NUM_BUCKETS // 2
    ret = (rel > 0).astype(jnp.int32) * nb
    n = jnp.abs(rel)
    max_exact = nb // 2
    nf = jnp.maximum(n, 1).astype(jnp.float32)
    large = max_exact + (jnp.log(nf / max_exact) / np.float32(np.log(MAX_DISTANCE / max_exact))
                         * (nb - max_exact)).astype(jnp.int32)
    large = jnp.minimum(large, nb - 1)
    return ret + jnp.where(n < max_exact, n, large)


def _near_bias(rel_bias, tq):
    assert tq + 1 >= MAX_DISTANCE
    rel0 = jnp.arange(tq, dtype=jnp.int32)[:, None] - jnp.arange(tq, dtype=jnp.int32)[None, :]
    buckets = _t5_bucket(jnp.stack([rel0, rel0 - tq])) & (NUM_BUCKETS - 1)
    shifted = (rel_bias - rel_bias[NUM_BUCKETS // 2 - 1][None, :]) * np.float32(LOG2_E)
    tabs = jnp.zeros((rel_bias.shape[1],) + buckets.shape, f32)
    for b in range(NUM_BUCKETS):
        tabs = jnp.where(buckets[None] == b, shifted[b][:, None, None, None], tabs)
    return tabs, jnp.maximum(jnp.max(shifted, axis=0), 0.0)


def kernel(x, g_pre_mix, w_in, sgu_ln_g, sgu_ln_b, sgu_w, sgu_b, rel_bias, w_o, g_post_mix,
           g_pre_ffn, w_gate, w_up, w_down, g_post_ffn):
    b, s, d = x.shape
    depth = w_in.shape[0]
    d_a = d // 2
    d_b = d - d_a
    d_main = 2 * d_a + 3 * d_b + IDX_HEADS * IDX_DIM
    top_k = min(MAX_TOPK, s // 4)
    tq = KEY_BITS * V7X_SUBLANES
    tm = TOKEN_TILE
    x2 = x.reshape(b * s, d)
    for l in range(depth):
        w_l = w_in[l]
        w_main = w_l[:, :d_main].astype(bf16)
        w_ik = jnp.pad(w_l[:, d_main:], ((0, 0), (0, V7X_LANES - (IDX_DIM + IDX_HEADS)))).astype(bf16)
        out_a, q, k, vt, iq, ik, iwt, ksq = _proj_sgu(
            x2, g_pre_mix[l][None], w_main, w_ik, sgu_ln_g[l][None], sgu_ln_b[l][None],
            sgu_w[l], jnp.transpose(sgu_b[l]), d_a=d_a, d_b=d_b, tm=tm)
        bias_tabs, bias_max = _near_bias(rel_bias, tq)
        out_b = _dsa(bias_max, q, iq, iwt, k, vt, ik, ksq, bias_tabs, batch=b, tq=tq, top_k=top_k)
        w_ol = w_o[l].astype(bf16)
        x2 = _out_ffn(x2, out_a, out_b, w_ol[:d_a], w_ol[d_a:],
                      g_post_mix[l][None], g_pre_ffn[l][None], w_gate[l].astype(bf16),
                      w_up[l].astype(bf16), w_down[l].astype(bf16), g_post_ffn[l][None],
                      tm=tm, ff_chunk=FF_CHUNK)
    return x2.reshape(b, s, d)
```

```python
import functools

import numpy as np
import jax
import jax.numpy as jnp
from jax import lax
from jax.experimental import pallas as pl
from jax.experimental.pallas import tpu as pltpu

CHUNK = 64
SGU_BLOCK = 128
SGU_GROUPS = 4
N_HEADS_B = 4
IDX_HEADS = 8
IDX_DIM = 64
MAX_TOPK = 256
NUM_BUCKETS = 32
MAX_DISTANCE = 128
EPS = 1e-6

V7X_LANES = 128
V7X_SUBLANES = 8
V7X_VMEM_LIMIT_BYTES = 56 * 1024 * 1024

TOKEN_TILE = 512
FF_CHUNK = 256
KEY_BITS = 32
GAP_BITS = 30
_INT_MIN = -(2 ** 31)
_INT_MAX = 2 ** 31 - 1
_MASKED = -(2.0 ** 100)
LOG2_E = float(np.log2(np.e))
SUM_FLOOR = 1e-30

_NT = (((1,), (1,)), ((), ()))

f32 = jnp.float32
bf16 = jnp.bfloat16
i32 = jnp.int32


def _rms(x, g):
    return x * lax.rsqrt(jnp.mean(x * x, axis=-1, keepdims=True) + EPS) * g


def _proj_sgu_body(x_ref, g_ref, w_ref, wik_ref, lng_ref, lnb_ref, sw_ref, sbt_ref,
                   outa_ref, q_ref, k_ref, vt_ref, iq_ref, ik_ref, iwt_ref, ksq_ref, *, d_a, d_b):
    tm = x_ref.shape[0]
    h = _rms(x_ref[...], g_ref[...]).astype(bf16)

    def proj(c0, width):
        return jnp.dot(h, w_ref[:, c0:c0 + width], preferred_element_type=f32)

    u = proj(0, d_a)
    a_v = proj(d_a, d_a)
    c0 = 2 * d_a
    q_ref[...] = (proj(c0, d_b) * np.float32((d_b // N_HEADS_B) ** -0.5 * LOG2_E)).astype(bf16)
    k = proj(c0 + d_b, d_b).astype(bf16)
    k_ref[...] = k
    kt = k.astype(f32).T
    hd = d_b // N_HEADS_B
    ksq = [jnp.sum(kt[a * hd:(a + 1) * hd] ** 2, axis=0, keepdims=True) for a in range(N_HEADS_B)]
    ksq_ref[...] = jnp.concatenate(ksq + [jnp.zeros((V7X_SUBLANES - N_HEADS_B, tm), f32)], axis=0)
    vt_ref[...] = proj(c0 + 2 * d_b, d_b).astype(bf16).T
    iq = (proj(c0 + 3 * d_b, IDX_HEADS * IDX_DIM) * np.float32(IDX_DIM ** -0.5)).astype(bf16)
    for hh in range(IDX_HEADS):
        iq_ref[hh] = iq[:, hh * IDX_DIM:(hh + 1) * IDX_DIM]
    ikw = jnp.dot(h, wik_ref[...], preferred_element_type=f32)
    ik_ref[...] = ikw[:, :IDX_DIM].astype(bf16)
    iwt_ref[...] = ikw.T[IDX_DIM:IDX_DIM + IDX_HEADS, :] * np.float32(IDX_HEADS ** -0.5)

    mu = jnp.mean(a_v, axis=-1, keepdims=True)
    cen = a_v - mu
    var = jnp.mean(cen * cen, axis=-1, keepdims=True)
    vn = (cen * lax.rsqrt(var + EPS) * lng_ref[...] + lnb_ref[...]).astype(bf16)

    row_chunk = lax.broadcasted_iota(i32, (SGU_BLOCK, SGU_BLOCK), 0) // CHUNK
    col_chunk = lax.broadcasted_iota(i32, (SGU_BLOCK, SGU_BLOCK), 1) // CHUNK
    causal = col_chunk <= row_chunk
    gdim = d_a // SGU_GROUPS
    for g in range(SGU_GROUPS):
        w_g = jnp.where(causal, sw_ref[g], 0.0).astype(bf16)
        b_g = sbt_ref[:, g:g + 1]
        for c in range(tm // SGU_BLOCK):
            rows = slice(c * SGU_BLOCK, (c + 1) * SGU_BLOCK)
            cols = slice(g * gdim, (g + 1) * gdim)
            mixed = jnp.dot(w_g, vn[rows, cols], preferred_element_type=f32) + b_g
            outa_ref[rows, cols] = (u[rows, cols] * mixed).astype(bf16)


def _proj_sgu(x2, g_pre, w_main, w_ik, ln_g, ln_b, sgu_w, sgu_bt, *, d_a, d_b, tm):
    m, d = x2.shape
    const = lambda i: (0, 0)
    row = lambda i: (i, 0)
    col = lambda i: (0, i)
    return pl.pallas_call(
        functools.partial(_proj_sgu_body, d_a=d_a, d_b=d_b),
        grid=(m // tm,),
        in_specs=[
            pl.BlockSpec((tm, d), row),
            pl.BlockSpec((1, d), const),
            pl.BlockSpec(w_main.shape, const),
            pl.BlockSpec(w_ik.shape, const),
            pl.BlockSpec((1, d_a), const),
            pl.BlockSpec((1, d_a), const),
            pl.BlockSpec(sgu_w.shape, lambda i: (0, 0, 0)),
            pl.BlockSpec(sgu_bt.shape, const),
        ],
        out_specs=[
            pl.BlockSpec((tm, d_a), row),
            pl.BlockSpec((tm, d_b), row),
            pl.BlockSpec((tm, d_b), row),
            pl.BlockSpec((d_b, tm), col),
            pl.BlockSpec((IDX_HEADS, tm, IDX_DIM), lambda i: (0, i, 0)),
            pl.BlockSpec((tm, IDX_DIM), row),
            pl.BlockSpec((IDX_HEADS, tm), col),
            pl.BlockSpec((V7X_SUBLANES, tm), col),
        ],
        out_shape=[
            jax.ShapeDtypeStruct((m, d_a), bf16),
            jax.ShapeDtypeStruct((m, d_b), bf16),
            jax.ShapeDtypeStruct((m, d_b), bf16),
            jax.ShapeDtypeStruct((d_b, m), bf16),
            jax.ShapeDtypeStruct((IDX_HEADS, m, IDX_DIM), bf16),
            jax.ShapeDtypeStruct((m, IDX_DIM), bf16),
            jax.ShapeDtypeStruct((IDX_HEADS, m), f32),
            jax.ShapeDtypeStruct((V7X_SUBLANES, m), f32),
        ],
        compiler_params=pltpu.CompilerParams(
            dimension_semantics=("arbitrary",), vmem_limit_bytes=V7X_VMEM_LIMIT_BYTES),
        name="proj_sgu",
    )(x2, g_pre, w_main, w_ik, ln_g, ln_b, sgu_w, sgu_bt)


def _bit_planes(words):
    a = list(words)
    j, m = 16, 0x0000FFFF
    while j:
        k = 0
        while k < KEY_BITS:
            t = (a[k] ^ lax.shift_right_logical(a[k + j], i32(j))) & i32(m)
            a[k] = a[k] ^ t
            a[k + j] = a[k + j] ^ lax.shift_left(t, i32(j))
            k = (k + j + 1) & ~j
        j >>= 1
        m = (m ^ (m << j)) & 0xFFFFFFFF
    return a


def _count_bits(words):
    return jnp.sum(lax.population_count(words), axis=0, keepdims=True)


def _dsa_body(bmax_ref, q_ref, iq_ref, iwt_ref, k_ref, vt_ref, ik_ref, ksq_ref, bias_ref, out_ref,
              sc_ref, planes_ref, aux_ref, madd_ref, sa_ref, sb_ref, m_ref, l_ref, acc_ref,
              *, tq, top_k, seq_bits):
    tk = tq
    assert tk == KEY_BITS * V7X_SUBLANES
    hd = q_ref.shape[1] // N_HEADS_B
    qi = pl.program_id(1)
    n_tiles = qi + 1
    int_min = i32(_INT_MIN)

    def rows(j):
        return pl.ds(pl.multiple_of(j * tk, tk), tk)

    @pl.when((pl.program_id(0) == 0) & (qi == 0))
    def _():
        planes_ref[...] = jnp.zeros(planes_ref.shape, i32)

    def to_key(x):
        bits = lax.bitcast_convert_type(x, i32)
        return bits ^ ((bits >> (KEY_BITS - 1)) & i32(_INT_MAX))

    def from_key(key):
        return lax.bitcast_convert_type(key ^ ((key >> (KEY_BITS - 1)) & i32(_INT_MAX)), f32)

    def score_tile(j, diagonal):
        ik_j = ik_ref[rows(j), :]
        sc = jnp.zeros((tk, tq), f32)
        for h in range(IDX_HEADS):
            d = lax.dot_general(ik_j, iq_ref[h], _NT, preferred_element_type=f32)
            sc = sc + iwt_ref[h:h + 1, :] * jnp.maximum(d, 0.0)
        key = to_key(sc)
        if diagonal:
            key_chunk = lax.broadcasted_iota(i32, (tk, tq), 0) // CHUNK
            qry_chunk = lax.broadcasted_iota(i32, (tk, tq), 1) // CHUNK
            visible = key_chunk <= qry_chunk
            sc = jnp.where(visible, sc, -jnp.inf)
            key = jnp.where(visible, key, int_min)
        sc_ref[rows(j), :] = sc
        return key

    def plane_tile(j, key):
        prow = pl.ds(pl.multiple_of(j * V7X_SUBLANES, V7X_SUBLANES), V7X_SUBLANES)
        for lt in range(tq // V7X_LANES):
            lanes = slice(lt * V7X_LANES, (lt + 1) * V7X_LANES)
            planes = _bit_planes([key[V7X_SUBLANES * i:V7X_SUBLANES * (i + 1), lanes]
                                  for i in range(KEY_BITS)])
            planes_ref[0, prow, lanes] = ~planes[0]
            for p in range(1, KEY_BITS):
                planes_ref[p, prow, lanes] = planes[p]

    def far_scores(j, prev):
        key = score_tile(j, False)
        plane_tile(j - 1, prev)
        return key

    def far_pair(i, prev):
        return far_scores(2 * i + 2, far_scores(2 * i + 1, prev))

    n_far = jnp.maximum(qi - 1, 0)
    prev = lax.fori_loop(0, n_far // 2, far_pair, score_tile(0, False))
    prev = lax.cond(n_far % 2 == 1, lambda p: far_scores(qi - 1, p), lambda p: p, prev)
    last = score_tile(qi, True)
    plane_tile(jnp.maximum(qi - 1, 0), prev)
    plane_tile(qi, last)

    n_prow = planes_ref.shape[1]
    zero_row = jnp.zeros((1, tq), i32)
    live = lax.broadcasted_iota(i32, (n_prow, tq), 0) < n_tiles * V7X_SUBLANES
    alive, n_above, thr_bits = jnp.where(live, i32(-1), 0), zero_row, zero_row
    for p in range(KEY_BITS):
        ones = alive & planes_ref[p]
        n_ones = _count_bits(ones)
        take = (n_above + n_ones) >= top_k
        alive = jnp.where(take, ones, alive ^ ones)
        n_above = jnp.where(take, n_above, n_above + n_ones)
        thr_bits = thr_bits | jnp.where(take, i32(1 << (KEY_BITS - 1 - p) if p else _INT_MIN), 0)
    thr_key = thr_bits ^ int_min
    n_equal = _count_bits(alive)
    found = thr_key > int_min
    lowest = jnp.float32(np.finfo(np.float32).min)
    thr = jnp.where(found, from_key(thr_key), lowest)
    tie = found & (n_above + n_equal > top_k)

    def fold(x):
        return x.reshape(tk // V7X_SUBLANES, V7X_SUBLANES, tq).sum(axis=0)

    def kept(mask_sums):
        n_masked = jnp.sum(mask_sums, axis=0, keepdims=True) * np.float32(1.0 / _MASKED)
        return n_tiles * tk - n_masked.astype(i32)

    def make_mask(thr_f):
        def body(j, acc):
            m = jnp.where(sc_ref[rows(j), :] >= thr_f, 0.0, _MASKED).astype(f32)
            madd_ref[rows(j), :] = m
            return acc + fold(m)
        return kept(lax.fori_loop(0, n_tiles, body, jnp.zeros((V7X_SUBLANES, tq), f32)))

    def key_idx(j):
        return j * tk + lax.broadcasted_iota(i32, (tk, tq), 0)

    def count_where(pred):
        def body(j, acc):
            return acc + fold(pred(sc_ref[rows(j), :], j).astype(i32))
        acc = lax.fori_loop(0, n_tiles, body, jnp.zeros((V7X_SUBLANES, tq), i32))
        return jnp.sum(acc, axis=0, keepdims=True)

    def last_kept(n_keep, count_before):
        def index_pass(p, last):
            cand = last + lax.shift_left(i32(1), seq_bits - 1 - p)
            return jnp.where(count_before(cand) < n_keep, cand, last)
        return lax.fori_loop(0, seq_bits, index_pass, zero_row)

    def remask(base, delta, tied, last):
        def body(j, acc):
            above = sc_ref[rows(j), :] - base
            keep = (above > delta) | ((above == delta) & (jnp.logical_not(tied) | (key_idx(j) <= last)))
            m = jnp.where(keep, 0.0, _MASKED).astype(f32)
            madd_ref[rows(j), :] = m
            return acc + fold(m)
        return kept(lax.fori_loop(0, n_tiles, body, jnp.zeros((V7X_SUBLANES, tq), f32)))

    qry = qi * tq + lax.broadcasted_iota(i32, (1, tq), 1)
    n_visible = (qry // CHUNK + 1) * CHUNK
    n_kept = make_mask(thr)

    agree = jnp.max((n_kept != jnp.where(found, n_above + n_equal, n_visible)).astype(i32)) == 0
    aux_ref[...] = jnp.broadcast_to(n_kept, aux_ref.shape)

    @pl.when(agree & (jnp.max(tie.astype(i32)) > 0))
    def _():
        word = lax.broadcasted_iota(i32, (n_prow, tq), 0)
        log2_sublanes = V7X_SUBLANES.bit_length() - 1
        first_key = (word >> log2_sublanes) * tk + (word & (V7X_SUBLANES - 1))

        def tied_before(cand):
            n_bits = jnp.clip((cand - first_key + (V7X_SUBLANES - 1)) >> log2_sublanes, 0, KEY_BITS)
            top_bits = jnp.where(n_bits <= 0, i32(0),
                                 lax.shift_left(i32(-1), jnp.minimum(KEY_BITS - n_bits, KEY_BITS - 1)))
            return _count_bits(alive & top_bits)

        last = last_kept(top_k - n_above, tied_before)
        aux_ref[...] = jnp.broadcast_to(remask(thr, jnp.float32(0.0), tie, last), aux_ref.shape)

    settled = agree & (jnp.max((aux_ref[0:1, :] != jnp.where(found, top_k, n_visible)).astype(i32)) == 0)

    @pl.when(jnp.logical_not(settled))
    def _():
        def value_pass(p, prefix):
            cand = prefix + lax.shift_left(i32(1), KEY_BITS - 1 - p)
            cand_f = from_key(cand)
            cnt = count_where(lambda st, j: st >= cand_f)
            return jnp.where(cnt >= top_k, cand, prefix)

        base_key = lax.fori_loop(0, KEY_BITS, value_pass, jnp.full((1, tq), _INT_MIN, i32))
        short = (n_visible < top_k) | (base_key == int_min)
        base = jnp.where(short, lowest, from_key(base_key))
        gap = from_key(base_key + 1) - base

        def gap_pass(p, delta):
            cand = delta + gap / lax.shift_left(i32(1), p + 1).astype(f32)
            cnt = count_where(lambda st, j: (st - base) >= cand)
            return jnp.where(cnt >= top_k, cand, delta)

        delta = lax.fori_loop(0, GAP_BITS, gap_pass, jnp.zeros((1, tq), f32))
        delta = jnp.where(short, 0.0, delta)
        n_keep = top_k - count_where(lambda st, j: (st - base) > delta)
        tied = (count_where(lambda st, j: (st - base) >= delta) > top_k) & jnp.logical_not(short)
        last = last_kept(n_keep, lambda cand: count_where(
            lambda st, j: ((st - base) == delta) & (key_idx(j) < cand)))
        remask(base, delta, tied, last)

    heads = [slice(h * hd, (h + 1) * hd) for h in range(N_HEADS_B)]

    bounds = []
    for h, hs in enumerate(heads):
        qt = q_ref[:, hs].astype(f32).T
        longest_key = jnp.sqrt(jnp.max(ksq_ref[h:h + 1, :]))
        bounds.append(jnp.sqrt(jnp.sum(qt * qt, axis=0, keepdims=True)) * longest_key + bmax_ref[h])

    def qk(j, h):
        return lax.dot_general(k_ref[rows(j), heads[h]], q_ref[:, heads[h]], _NT,
                               preferred_element_type=f32)

    def logits(j, h, near):
        s = qk(j, h) + madd_ref[rows(j), :]
        return s if near is None else s + bias_ref[h, near]

    acc_ref[...] = jnp.zeros(acc_ref.shape, f32)
    l_ref[...] = jnp.zeros(l_ref.shape, f32)

    def issue(j, buf):
        for h in range(N_HEADS_B):
            buf[h] = qk(j, h)

    def consume(j, buf, near):
        madd = madd_ref[rows(j), :]
        for h in range(N_HEADS_B):
            x = buf[h] + madd
            if near is not None:
                x = x + bias_ref[h, near]
            e = jnp.exp2(x - bounds[h])
            l_ref[h] += fold(e)
            acc_ref[h] += jnp.dot(vt_ref[heads[h], rows(j)], e.astype(bf16),
                                  preferred_element_type=f32)

    issue(0, sa_ref)

    def far_pair(i, carry):
        j = 2 * i
        issue(j + 1, sb_ref)
        consume(j, sa_ref, None)
        issue(j + 2, sa_ref)
        consume(j + 1, sb_ref, None)
        return carry

    lax.fori_loop(0, n_far // 2, far_pair, 0)
    j0 = 2 * (n_far // 2)

    @pl.when(n_far % 2 == 1)
    def _():
        issue(j0 + 1, sb_ref)
        consume(j0, sa_ref, None)
        issue(j0 + 2, sa_ref)
        consume(j0 + 1, sb_ref, 1)
        consume(j0 + 2, sa_ref, 0)

    @pl.when((n_far % 2 == 0) & (qi >= 1))
    def _():
        issue(j0 + 1, sb_ref)
        consume(j0, sa_ref, 1)
        consume(j0 + 1, sb_ref, 0)

    @pl.when(qi == 0)
    def _():
        consume(0, sa_ref, 0)

    smallest = jnp.min(jnp.sum(l_ref[...], axis=1))

    @pl.when(jnp.logical_not(smallest > SUM_FLOOR))
    def _():
        acc_ref[...] = jnp.zeros(acc_ref.shape, f32)
        first_row = lax.broadcasted_iota(i32, (V7X_SUBLANES, tq), 0) == 0

        def attend_online(j, near):
            for h in range(N_HEADS_B):
                s = logits(j, h, near)
                m_old = m_ref[h]
                m_new = jnp.maximum(m_old, jnp.max(s, axis=0, keepdims=True))
                alpha = jnp.exp2(m_old - m_new)
                p = jnp.exp2(s - m_new)
                l_new = (alpha * jnp.sum(l_ref[h], axis=0, keepdims=True)
                         + jnp.sum(p, axis=0, keepdims=True))
                l_ref[h] = jnp.where(first_row, l_new, 0.0)
                pv = jnp.dot(vt_ref[heads[h], rows(j)], p.astype(bf16), preferred_element_type=f32)
                acc_ref[h] = alpha * acc_ref[h] + pv
                m_ref[h] = m_new

        m_ref[...] = jnp.full(m_ref.shape, _MASKED, f32)
        l_ref[...] = jnp.zeros(l_ref.shape, f32)

        def far_online(j, carry):
            attend_online(j, None)
            return carry

        lax.fori_loop(0, qi - 1, far_online, 0)

        @pl.when(qi >= 1)
        def _():
            attend_online(qi - 1, 1)

        attend_online(qi, 0)

    for h in range(N_HEADS_B):
        scale = 1.0 / jnp.sum(l_ref[h], axis=0, keepdims=True)
        out_ref[:, heads[h]] = (acc_ref[h] * scale).T.astype(out_ref.dtype)


def _dsa(bias_max, q, iq, iwt, k, vt, ik, ksq, bias_tabs, *, batch, tq, top_k):
    m, d_b = q.shape
    s = m // batch
    nq = s // tq
    hd = d_b // N_HEADS_B
    tile_rows = lambda bi, qi: (bi * nq + qi, 0)
    batch_rows = lambda bi, qi: (bi, 0)
    return pl.pallas_call(
        functools.partial(_dsa_body, tq=tq, top_k=top_k, seq_bits=int(np.log2(s))),
        grid=(batch, nq),
        in_specs=[
            pl.BlockSpec(memory_space=pltpu.SMEM),
            pl.BlockSpec((tq, d_b), tile_rows),
            pl.BlockSpec((IDX_HEADS, tq, IDX_DIM), lambda bi, qi: (0, bi * nq + qi, 0)),
            pl.BlockSpec((IDX_HEADS, tq), lambda bi, qi: (0, bi * nq + qi)),
            pl.BlockSpec((s, d_b), batch_rows),
            pl.BlockSpec((d_b, s), lambda bi, qi: (0, bi)),
            pl.BlockSpec((s, IDX_DIM), batch_rows),
            pl.BlockSpec((V7X_SUBLANES, s), lambda bi, qi: (0, bi)),
            pl.BlockSpec(bias_tabs.shape, lambda bi, qi: (0, 0, 0, 0)),
        ],
        out_specs=pl.BlockSpec((tq, d_b), tile_rows),
        out_shape=jax.ShapeDtypeStruct((m, d_b), bf16),
        scratch_shapes=[
            pltpu.VMEM((s, tq), f32),
            pltpu.VMEM((KEY_BITS, s // KEY_BITS, tq), i32),
            pltpu.VMEM((V7X_SUBLANES, tq), i32),
            pltpu.VMEM((s, tq), f32),
            pltpu.VMEM((N_HEADS_B, tq, tq), f32),
            pltpu.VMEM((N_HEADS_B, tq, tq), f32),
            pltpu.VMEM((N_HEADS_B, 1, tq), f32),
            pltpu.VMEM((N_HEADS_B, V7X_SUBLANES, tq), f32),
            pltpu.VMEM((N_HEADS_B, hd, tq), f32),
        ],
        compiler_params=pltpu.CompilerParams(
            dimension_semantics=("arbitrary", "arbitrary"), vmem_limit_bytes=V7X_VMEM_LIMIT_BYTES),
        name="dsa",
    )(bias_max, q, iq, iwt, k, vt, ik, ksq, bias_tabs)


def _out_ffn_body(x_ref, a_ref, b_ref, woa_ref, wob_ref, gpm_ref, gpf_ref, wg_ref, wu_ref,
                  wd_ref, gpo_ref, o_ref, *, ff_chunk):
    mix = (jnp.dot(a_ref[...], woa_ref[...], preferred_element_type=f32)
           + jnp.dot(b_ref[...], wob_ref[...], preferred_element_type=f32))
    x1 = x_ref[...] + _rms(mix, gpm_ref[...])
    h = _rms(x1, gpf_ref[...]).astype(bf16)
    d_ff = wg_ref.shape[1]
    f = jnp.zeros(x1.shape, f32)
    for c in range(d_ff // ff_chunk):
        cs = slice(c * ff_chunk, (c + 1) * ff_chunk)
        gate = jnp.dot(h, wg_ref[:, cs], preferred_element_type=f32)
        up = jnp.dot(h, wu_ref[:, cs], preferred_element_type=f32)
        act = (gate * jax.nn.sigmoid(gate) * up).astype(bf16)
        f = f + jnp.dot(act, wd_ref[cs, :], preferred_element_type=f32)
    o_ref[...] = x1 + _rms(f, gpo_ref[...])


def _out_ffn(x2, out_a, out_b, w_oa, w_ob, g_pm, g_pf, w_gate, w_up, w_down, g_po, *, tm, ff_chunk):
    m, d = x2.shape
    d_a, d_b, d_ff = out_a.shape[1], out_b.shape[1], w_gate.shape[1]
    row = lambda i: (i, 0)
    const = lambda i: (0, 0)
    resident = functools.partial(pl.BlockSpec, index_map=const, pipeline_mode=pl.Buffered(1))
    return pl.pallas_call(
        functools.partial(_out_ffn_body, ff_chunk=ff_chunk),
        grid=(m // tm,),
        in_specs=[
            pl.BlockSpec((tm, d), row),
            pl.BlockSpec((tm, d_a), row),
            pl.BlockSpec((tm, d_b), row),
            resident((d_a, d)),
            resident((d_b, d)),
            resident((1, d)),
            resident((1, d)),
            resident((d, d_ff)),
            resident((d, d_ff)),
            resident((d_ff, d)),
            resident((1, d)),
        ],
        out_specs=pl.BlockSpec((tm, d), row),
        out_shape=jax.ShapeDtypeStruct((m, d), f32),
        compiler_params=pltpu.CompilerParams(
            dimension_semantics=("arbitrary",), vmem_limit_bytes=V7X_VMEM_LIMIT_BYTES),
        name="out_ffn",
    )(x2, out_a, out_b, w_oa, w_ob, g_pm, g_pf, w_gate, w_up, w_down, g_po)


def _t5_bucket(rel):
    nb = NUM_BUCKETS // 2
    ret = (rel > 0).astype(jnp.int32) * nb
    n = jnp.abs(rel)
    max_exact = nb // 2
    nf = jnp.maximum(n, 1).astype(jnp.float32)
    large = max_exact + (jnp.log(nf / max_exact) / np.float32(np.log(MAX_DISTANCE / max_exact))
                         * (nb - max_exact)).astype(jnp.int32)
    large = jnp.minimum(large, nb - 1)
    return ret + jnp.where(n < max_exact, n, large)


def _near_bias(rel_bias, tq):
    assert tq + 1 >= MAX_DISTANCE
    rel0 = jnp.arange(tq, dtype=jnp.int32)[:, None] - jnp.arange(tq, dtype=jnp.int32)[None, :]
    buckets = _t5_bucket(jnp.stack([rel0, rel0 - tq])) & (NUM_BUCKETS - 1)
    shifted = (rel_bias - rel_bias[NUM_BUCKETS // 2 - 1][None, :]) * np.float32(LOG2_E)
    tabs = jnp.zeros((rel_bias.shape[1],) + buckets.shape, f32)
    for b in range(NUM_BUCKETS):
        tabs = jnp.where(buckets[None] == b, shifted[b][:, None, None, None], tabs)
    return tabs, jnp.maximum(jnp.max(shifted, axis=0), 0.0)


def kernel(x, g_pre_mix, w_in, sgu_ln_g, sgu_ln_b, sgu_w, sgu_b, rel_bias, w_o, g_post_mix,
           g_pre_ffn, w_gate, w_up, w_down, g_post_ffn):
    b, s, d = x.shape
    depth = w_in.shape[0]
    d_a = d // 2
    d_b = d - d_a
    d_main = 2 * d_a + 3 * d_b + IDX_HEADS * IDX_DIM
    top_k = min(MAX_TOPK, s // 4)
    tq = KEY_BITS * V7X_SUBLANES
    tm = TOKEN_TILE
    x2 = x.reshape(b * s, d)
    for l in range(depth):
        w_l = w_in[l]
        w_main = w_l[:, :d_main].astype(bf16)
        w_ik = jnp.pad(w_l[:, d_main:], ((0, 0), (0, V7X_LANES - (IDX_DIM + IDX_HEADS)))).astype(bf16)
        out_a, q, k, vt, iq, ik, iwt, ksq = _proj_sgu(
            x2, g_pre_mix[l][None], w_main, w_ik, sgu_ln_g[l][None], sgu_ln_b[l][None],
            sgu_w[l], jnp.transpose(sgu_b[l]), d_a=d_a, d_b=d_b, tm=tm)
        bias_tabs, bias_max = _near_bias(rel_bias, tq)
        out_b = _dsa(bias_max, q, iq, iwt, k, vt, ik, ksq, bias_tabs, batch=b, tq=tq, top_k=top_k)
        w_ol = w_o[l].astype(bf16)
        x2 = _out_ffn(x2, out_a, out_b, w_ol[:d_a], w_ol[d_a:],
                      g_post_mix[l][None], g_pre_ffn[l][None], w_gate[l].astype(bf16),
                      w_up[l].astype(bf16), w_down[l].astype(bf16), g_post_ffn[l][None],
                      tm=tm, ff_chunk=FF_CHUNK)
    return x2.reshape(b, s, d)
```
